```python
import jax, jax.numpy as jnp
from jax import lax
import numpy as np

D_MODEL = 1024
BATCH = 8
SEQ = 8192
DEPTH = 1

MIX_WIDTH = D_MODEL
ATT_WIDTH = MIX_WIDTH // 2
ATT_HEAD_DIM = 64
ATT_HEADS = ATT_WIDTH // ATT_HEAD_DIM
KV_LATENT = 128
IDX_HEADS = 8
IDX_DIM = 64
TOPK_MAX = 256
Q_BLOCK = 128
HG_WIDTH = MIX_WIDTH - ATT_WIDTH
HG_HEADS = 8
HG_VAL_DIM = HG_WIDTH // HG_HEADS
HG_KEY_DIM = HG_VAL_DIM
HG_CHUNK = 32
PEER_HEADS = 8
PEER_N_KEYS = 128
PEER_N_EXPERTS = PEER_N_KEYS * PEER_N_KEYS
PEER_KEY_DIM = 256
PEER_TOPK = 16
PEER_BLOCK = 128
EPS = 1e-5
ALPHA = (2.0 * DEPTH) ** 0.25
BETA = (8.0 * DEPTH) ** -0.25

ATT_Q = ATT_HEADS * ATT_HEAD_DIM
HG_QK = HG_HEADS * HG_KEY_DIM
HG_V = HG_HEADS * HG_VAL_DIM
IN_SPLITS = (ATT_Q, KV_LATENT, IDX_HEADS * IDX_DIM, IDX_DIM, IDX_HEADS, HG_QK, HG_QK, HG_V, HG_V)
D_IN = ATT_Q + KV_LATENT + IDX_HEADS * IDX_DIM + IDX_DIM + IDX_HEADS + 2 * HG_QK + 2 * HG_V

kernel_name = 'hymba_dsa_hgrn2_peer_deepnorm'


def _layer_norm(x, g, b):
    xf = x.astype(jnp.float32)
    mu = jnp.mean(xf, axis=-1, keepdims=True)
    var = jnp.mean(jnp.square(xf - mu), axis=-1, keepdims=True)
    return ((xf - mu) * lax.rsqrt(var + EPS) * g + b).astype(x.dtype)


def _rms_norm(x, g):
    xf = x.astype(jnp.float32)
    return (xf * lax.rsqrt(jnp.mean(xf * xf, axis=-1, keepdims=True) + EPS) * g).astype(x.dtype)


def _split_columns(proj):
    points = [int(p) for p in np.cumsum(IN_SPLITS)[:-1]]
    return jnp.split(proj, points, axis=-1)


def _dsa_attention(q, c_kv, q_idx, k_idx, w_idx, w_uk, w_uv):
    B, L = q.shape[:2]
    top_k = min(TOPK_MAX, L // 4)
    nb = L // Q_BLOCK
    q_lat = jnp.einsum('blhd,hcd->blhc', q, w_uk) * (ATT_HEAD_DIM ** -0.5)
    key_pos = jnp.arange(L)

    def to_blocks(a):
        return jnp.moveaxis(a.reshape(B, nb, Q_BLOCK, *a.shape[2:]), 1, 0)

    def one_block(args):
        ql, qi, wi, qpos = args
        logits = jax.nn.relu(jnp.einsum('bqhd,bsd->bqhs', qi, k_idx))
        score = jnp.einsum('bqhs,bqh->bqs', logits, wi).astype(jnp.float32)
        causal = key_pos[None, :] <= qpos[:, None]
        score = jnp.where(causal[None], score, -jnp.inf)
        _, sel = lax.top_k(score, top_k)
        c_sel = jax.vmap(lambda c, i: c[i])(c_kv, sel)
        att = jnp.einsum('bqhc,bqkc->bqhk', ql, c_sel).astype(jnp.float32)
        valid = (sel <= qpos[None, :, None])[:, :, None, :]
        p = jax.nn.softmax(jnp.where(valid, att, -jnp.inf), axis=-1).astype(c_sel.dtype)
        return jnp.einsum('bqhk,bqkc->bqhc', p, c_sel)

    o_lat = lax.map(one_block, (to_blocks(q_lat), to_blocks(q_idx), to_blocks(w_idx),
                                jnp.arange(L).reshape(nb, Q_BLOCK)))
    o_lat = jnp.moveaxis(o_lat, 0, 1).reshape(B, L, ATT_HEADS, KV_LATENT)
    o = jnp.einsum('blhc,hcd->blhd', o_lat, w_uv)
    return o.reshape(B, L, ATT_Q)


def _hgrn2(q, f_logit, i, gate, lower_bound, g_norm):
    B, L = q.shape[:2]
    nc = L // HG_CHUNK
    f32 = jnp.float32
    lb = lower_bound.astype(f32)
    f = lb + (1.0 - lb) * jax.nn.sigmoid(f_logit.astype(f32))
    log_f = jnp.log(f)
    k = 1.0 - f
    qf = jax.nn.silu(q.astype(f32))

    def chunks(a, d):
        return a.reshape(B, nc, HG_CHUNK, HG_HEADS, d)

    qc, kc, gc = chunks(qf, HG_KEY_DIM), chunks(k, HG_KEY_DIM), chunks(log_f, HG_KEY_DIM)
    vc = chunks(i.astype(f32), HG_VAL_DIM)
    b = jnp.cumsum(gc, axis=2)
    b_last = b[:, :, -1:]
    q_t = qc * jnp.exp(b)
    k_t = kc * jnp.exp(-b)
    mask = jnp.tril(jnp.ones((HG_CHUNK, HG_CHUNK), dtype=bool))
    a = jnp.where(mask, jnp.einsum('bnthd,bnshd->bnhts', q_t, k_t), 0.0)
    o_intra = jnp.einsum('bnhts,bnshe->bnthe', a, vc)
    ds = jnp.einsum('bnshd,bnshe->bnhde', kc * jnp.exp(b_last - b), vc)
    decay = jnp.exp(b_last[:, :, 0])

    def step(s, inp):
        d, dsn = inp
        return d[..., None] * s + dsn, s

    s0 = jnp.zeros((B, HG_HEADS, HG_KEY_DIM, HG_VAL_DIM), f32)
    _, s_prev = lax.scan(step, s0, (jnp.moveaxis(decay, 1, 0), jnp.moveaxis(ds, 1, 0)))
    s_prev = jnp.moveaxis(s_prev, 0, 1)
    o_inter = jnp.einsum('bnthd,bnhde->bnthe', q_t, s_prev)
    o = (o_intra + o_inter).reshape(B, L, HG_HEADS, HG_VAL_DIM)
    o = _rms_norm(o, g_norm) * jax.nn.silu(gate.astype(f32).reshape(B, L, HG_HEADS, HG_VAL_DIM))
    return o.reshape(B, L, HG_V).astype(i.dtype)


def _peer(x, w_q, sub_keys, u, v):
    B, L, D = x.shape
    xt = x.reshape(-1, PEER_BLOCK, D)
    K = PEER_TOPK

    def one_block(xb):
        q = (xb @ w_q).reshape(PEER_BLOCK, PEER_HEADS, 2, PEER_KEY_DIM // 2)
        s = jnp.einsum('thpd,hpkd->thpk', q, sub_keys).astype(jnp.float32)
        s_top, i_top = lax.top_k(s, K)
        cand = s_top[:, :, 0, :, None] + s_top[:, :, 1, None, :]
        cand_idx = i_top[:, :, 0, :, None] * PEER_N_KEYS + i_top[:, :, 1, None, :]
        best, pos = lax.top_k(cand.reshape(PEER_BLOCK, PEER_HEADS, K * K), K)
        experts = jnp.take_along_axis(cand_idx.reshape(PEER_BLOCK, PEER_HEADS, K * K), pos, axis=-1)
        g = jax.nn.softmax(best, axis=-1).astype(xb.dtype)
        h = jax.nn.gelu(jnp.einsum('td,thkd->thk', xb, u[experts]))
        return jnp.einsum('thk,thkd->td', g * h, v[experts])

    return lax.map(one_block, xt).reshape(B, L, D)


def setup_inputs(seed: int = 0) -> dict:
    key = jax.random.key(seed)
    ks = jax.random.split(key, 16)
    f32 = jnp.float32

    def normal(k, shape, scale):
        return jax.random.normal(k, shape, f32) * scale

    hg_i_start = int(np.cumsum(IN_SPLITS)[6])
    col_scale = jnp.ones((D_IN,), f32).at[hg_i_start:hg_i_start + HG_V].set(BETA)
    return {
        'x': normal(ks[0], (BATCH, SEQ, D_MODEL), 1.0),
        'w_in': normal(ks[1], (DEPTH, D_MODEL, D_IN), D_MODEL ** -0.5) * col_scale,
        'kv_norm_g': 1.0 + normal(ks[2], (DEPTH, KV_LATENT), 0.02),
        'w_uk': normal(ks[3], (DEPTH, ATT_HEADS, KV_LATENT, ATT_HEAD_DIM), ATT_HEAD_DIM ** -0.5),
        'w_uv': normal(ks[4], (DEPTH, ATT_HEADS, KV_LATENT, ATT_HEAD_DIM), BETA * KV_LATENT ** -0.5),
        'hg_lb_logits': normal(ks[5], (DEPTH + 1, HG_QK), 0.5),
        'hg_norm_g': 1.0 + normal(ks[6], (DEPTH, HG_VAL_DIM), 0.02),
        'w_out': normal(ks[7], (DEPTH, MIX_WIDTH, D_MODEL), BETA * MIX_WIDTH ** -0.5),
        'ln1_g': 1.0 + normal(ks[8], (DEPTH, D_MODEL), 0.02),
        'ln1_b': normal(ks[9], (DEPTH, D_MODEL), 0.02),
        'peer_w_q': normal(ks[10], (DEPTH, D_MODEL, PEER_HEADS * PEER_KEY_DIM), D_MODEL ** -0.5),
        'peer_sub_keys': normal(ks[11], (DEPTH, PEER_HEADS, 2, PEER_N_KEYS, PEER_KEY_DIM // 2), (PEER_KEY_DIM // 2) ** -0.5),
        'peer_u': normal(ks[12], (DEPTH, PEER_N_EXPERTS, D_MODEL), BETA * D_MODEL ** -0.5),
        'peer_v': normal(ks[13], (DEPTH, PEER_N_EXPERTS, D_MODEL), BETA),
        'ln2_g': 1.0 + normal(ks[14], (DEPTH, D_MODEL), 0.02),
        'ln2_b': normal(ks[15], (DEPTH, D_MODEL), 0.02),
    }


def reference(x, w_in, kv_norm_g, w_uk, w_uv, hg_lb_logits, hg_norm_g, w_out, ln1_g, ln1_b,
              peer_w_q, peer_sub_keys, peer_u, peer_v, ln2_g, ln2_b):
    B, L, _ = x.shape
    lower_bounds = jnp.cumsum(jax.nn.softmax(hg_lb_logits.astype(jnp.float32), axis=0), axis=0)
    for layer in range(DEPTH):
        proj = x @ w_in[layer]
        q_att, c_kv, q_idx, k_idx, w_idx, hg_q, hg_f, hg_i, hg_gate = _split_columns(proj)
        att_out = _dsa_attention(
            q_att.reshape(B, L, ATT_HEADS, ATT_HEAD_DIM),
            _rms_norm(c_kv, kv_norm_g[layer]),
            q_idx.reshape(B, L, IDX_HEADS, IDX_DIM),
            k_idx, w_idx, w_uk[layer], w_uv[layer])
        hg_out = _hgrn2(hg_q, hg_f, hg_i, hg_gate, lower_bounds[layer], hg_norm_g[layer])
        mix = jnp.concatenate([att_out, hg_out], axis=-1) @ w_out[layer]
        h = _layer_norm(ALPHA * x + mix, ln1_g[layer], ln1_b[layer])
        ffn = _peer(h, peer_w_q[layer], peer_sub_keys[layer], peer_u[layer], peer_v[layer])
        x = _layer_norm(ALPHA * h + ffn, ln2_g[layer], ln2_b[layer])
    return x
```

```python
import functools

import jax
import jax.numpy as jnp
from jax import lax
from jax.experimental import pallas as pl
from jax.experimental.pallas import tpu as pltpu

F32 = jnp.float32
BF16 = jnp.bfloat16
I32 = jnp.int32

ATT_HEADS = 8
ATT_HEAD_DIM = 64
KV_LATENT = 128
IDX_HEADS = 8
IDX_DIM = 64
TOPK_MAX = 256
HG_HEADS = 8
HG_DIM = 64
HG_CHUNK = 32
PEER_HEADS = 8
PEER_N_KEYS = 128
PEER_TOPK = 16
EPS = 1e-5

ATT_Q = ATT_HEADS * ATT_HEAD_DIM
IDX_Q = IDX_HEADS * IDX_DIM
HG_W = HG_HEADS * HG_DIM

LANES = 128
VMEM_LIMIT_BYTES = 56 * 1024 * 1024

NEG_BIG = -1e30
INT_MIN = -(2 ** 31)
KEY_NEG_INF = INT_MIN + 0x7FFFFF


def _dot(a, b):
    return jnp.dot(a, b, preferred_element_type=F32)


def _dot_nt(a, b):
    return lax.dot_general(a, b, (((1,), (1,)), ((), ())), preferred_element_type=F32)


def _dot_exact(a, b):
    return jnp.dot(a, b, preferred_element_type=F32, precision=lax.Precision.HIGHEST)


def _proj_kernel(x_ref, wc_ref, wk_ref, whq_ref, whf_ref, whi_ref, whg_ref,
                 wtq_ref, wtqi_ref, wtwi_ref, wtc_ref, grow_ref, gcol_ref, wuk_ref,
                 c_ref, ct_ref, kidx_ref, qidxt_ref, widxt_ref, qlatt_ref,
                 hq_ref, hf_ref, hi_ref, hg_ref):
    xb = x_ref[0].astype(BF16)

    c = _dot(xb, wc_ref[...])
    c = c * lax.rsqrt(jnp.mean(c * c, axis=-1, keepdims=True) + EPS) * grow_ref[...]
    c_ref[0] = c.astype(BF16)
    ct = _dot_nt(wtc_ref[...], xb)
    ct = ct * lax.rsqrt(jnp.mean(ct * ct, axis=0, keepdims=True) + EPS) * gcol_ref[...]
    ct_ref[0] = ct.astype(BF16)

    kidx_ref[0] = _dot(xb, wk_ref[...]).astype(BF16)
    qidxt_ref[0] = _dot_nt(wtqi_ref[...], xb).astype(BF16)
    widxt_ref[0] = _dot_nt(wtwi_ref[...], xb)

    qt = _dot_nt(wtq_ref[...], xb).astype(BF16)
    scale = ATT_HEAD_DIM ** -0.5
    for h in range(ATT_HEADS):
        ql = _dot(wuk_ref[h], qt[h * ATT_HEAD_DIM:(h + 1) * ATT_HEAD_DIM, :]) * scale
        qlatt_ref[0, h * KV_LATENT:(h + 1) * KV_LATENT, :] = ql.astype(BF16)

    hq_ref[0] = _dot(xb, whq_ref[...])
    hf_ref[0] = _dot(xb, whf_ref[...])
    hi_ref[0] = _dot(xb, whi_ref[...])
    hg_ref[0] = _dot(xb, whg_ref[...])


def _proj(x, w, kv_g, w_uk, *, tb, interpret=False):
    B, L, D = x.shape
    o = 0
    cols = {}
    for name, width in (("q", ATT_Q), ("c", KV_LATENT), ("qi", IDX_Q), ("k", IDX_DIM), ("wi", IDX_HEADS),
                        ("hq", HG_W), ("hf", HG_W), ("hi", HG_W), ("hg", HG_W)):
        cols[name] = w[:, o:o + width].astype(BF16)
        o += width
    assert o == w.shape[1]
    wt = lambda n: cols[n].T
    ins = [x, cols["c"], cols["k"], cols["hq"], cols["hf"], cols["hi"], cols["hg"],
           wt("q"), wt("qi"), wt("wi"), wt("c"),
           kv_g.reshape(1, KV_LATENT).astype(F32), kv_g.reshape(KV_LATENT, 1).astype(F32),
           w_uk.astype(BF16)]

    def full(a):
        zeros = (0,) * a.ndim
        return pl.BlockSpec(a.shape, lambda b, t: zeros)

    in_specs = [pl.BlockSpec((1, tb, D), lambda b, t: (b, t, 0))] + [full(a) for a in ins[1:]]
    nat = lambda wd, dt: (jax.ShapeDtypeStruct((B, L, wd), dt), pl.BlockSpec((1, tb, wd), lambda b, t: (b, t, 0)))
    tr = lambda wd, dt: (jax.ShapeDtypeStruct((B, wd, L), dt), pl.BlockSpec((1, wd, tb), lambda b, t: (b, 0, t)))
    outs = [nat(KV_LATENT, BF16), tr(KV_LATENT, BF16), nat(IDX_DIM, BF16), tr(IDX_Q, BF16), tr(IDX_HEADS, F32),
            tr(ATT_HEADS * KV_LATENT, BF16), nat(HG_W, F32), nat(HG_W, F32), nat(HG_W, F32), nat(HG_W, F32)]
    return pl.pallas_call(
        _proj_kernel,
        out_shape=[s for s, _ in outs],
        grid=(B, L // tb),
        in_specs=in_specs,
        out_specs=[s for _, s in outs],
        compiler_params=pltpu.CompilerParams(dimension_semantics=("parallel", "parallel"),
                                             vmem_limit_bytes=VMEM_LIMIT_BYTES),
        name="proj",
        interpret=interpret,
    )(*ins)


def _dsa_kernel(kidx_ref, c_ref, ct_ref, qidxt_ref, widxt_ref, qlatt_ref, wuvt_ref,
                out_ref, keys_ref, acc_ref, m_ref, l_ref, jb_ref, ot_ref, *, tq, kc, top_k, seq):
    qi = pl.program_id(1)
    nk = (qi + 1) * (tq // kc)
    t_idx = qi * tq + lax.broadcasted_iota(I32, (kc, tq), 1)
    row_iota = lax.broadcasted_iota(I32, (kc, tq), 0)

    def score_chunk(k, carry):
        r0 = pl.multiple_of(k * kc, kc)
        kk = kidx_ref[0, pl.ds(r0, kc), :]
        s = jnp.zeros((kc, tq), F32)
        for h in range(IDX_HEADS):
            z = _dot(kk, qidxt_ref[0, h * IDX_DIM:(h + 1) * IDX_DIM, :])
            s = s + widxt_ref[0, h:h + 1, :] * jnp.maximum(z, 0.0)
        s = jnp.where(r0 + row_iota <= t_idx, s, -jnp.inf)
        bits = pltpu.bitcast(s, I32)
        keys_ref[pl.ds(r0, kc), :] = bits ^ ((bits >> 31) & 0x7FFFFFFF)
        return carry

    lax.fori_loop(0, nk, score_chunk, 0)

    def count(pred):
        def body(k, cnt):
            r0 = pl.multiple_of(k * kc, kc)
            blk = keys_ref[pl.ds(r0, kc), :]
            return cnt + jnp.sum(jnp.where(pred(blk, r0), 1, 0).astype(I32), axis=0, keepdims=True)
        return lax.fori_loop(0, nk, body, jnp.zeros((1, tq), I32))

    def bit_step(i, t):
        cand = t + lax.shift_left(jnp.int32(1), 31 - i)
        cnt = count(lambda blk, r0: blk >= cand)
        return jnp.where(cnt >= top_k, cand, t)

    thr = lax.fori_loop(0, 32, bit_step, jnp.full((1, tq), INT_MIN, I32))

    n_gt = count(lambda blk, r0: blk > thr)
    n_ge = count(lambda blk, r0: blk >= thr)
    need = top_k - n_gt
    has_tie = jnp.where(n_ge > top_k, jnp.where(thr > KEY_NEG_INF, 1, 0), 0)
    jb_ref[...] = jnp.full((1, tq), seq, I32)

    @pl.when(jnp.max(has_tie) > 0)
    def _():
        def idx_step(i, v):
            cand = v + lax.shift_left(jnp.int32(1), (seq.bit_length() - 1) - i)
            cnt = count(lambda blk, r0: (blk == thr) & (r0 + row_iota < cand))
            return jnp.where(cnt < need, cand, v)
        v = lax.fori_loop(0, seq.bit_length(), idx_step, jnp.zeros((1, tq), I32))
        jb_ref[...] = jnp.where(has_tie > 0, v, seq)

    jb = jb_ref[...]

    m_ref[...] = jnp.full(m_ref.shape, NEG_BIG, F32)
    l_ref[...] = jnp.zeros(l_ref.shape, F32)
    acc_ref[...] = jnp.zeros(acc_ref.shape, F32)

    def att_chunk(k, carry):
        r0 = pl.multiple_of(k * kc, kc)
        key = keys_ref[pl.ds(r0, kc), :]
        s_idx = r0 + row_iota
        keep = jnp.where(key > thr, 0.0, jnp.where(key == thr, jnp.where(s_idx <= jb, 0.0, NEG_BIG), NEG_BIG))
        bias = jnp.where(s_idx <= t_idx, keep, NEG_BIG)
        cc = c_ref[0, pl.ds(r0, kc), :]
        cct = ct_ref[0, :, pl.ds(r0, kc)]
        for h in range(ATT_HEADS):
            a = _dot(cc, qlatt_ref[0, h * KV_LATENT:(h + 1) * KV_LATENT, :]) + bias
            m_old = m_ref[h]
            m_new = jnp.maximum(m_old, jnp.max(a, axis=0, keepdims=True))
            p = jnp.exp(a - m_new)
            alpha = jnp.exp(m_old - m_new)
            l_ref[h] = alpha * l_ref[h] + jnp.sum(p, axis=0, keepdims=True)
            acc_ref[h] = alpha * acc_ref[h] + _dot(cct, p.astype(BF16))
            m_ref[h] = m_new
        return carry

    lax.fori_loop(0, nk, att_chunk, 0)

    for h in range(ATT_HEADS):
        o_lat = (acc_ref[h] / l_ref[h]).astype(BF16)
        ot_ref[h * ATT_HEAD_DIM:(h + 1) * ATT_HEAD_DIM, :] = _dot(wuvt_ref[h], o_lat)
    out_ref[0] = ot_ref[...].T.astype(BF16)


def _dsa(kidx, c, ct, qidxt, widxt, qlatt, w_uv, *, tq, kc, interpret=False):
    B, L, _ = c.shape
    top_k = min(TOPK_MAX, L // 4)
    wuvt = jnp.swapaxes(w_uv, 1, 2).astype(BF16)
    kernel = functools.partial(_dsa_kernel, tq=tq, kc=kc, top_k=top_k, seq=L)
    return pl.pallas_call(
        kernel,
        out_shape=jax.ShapeDtypeStruct((B, L, ATT_Q), BF16),
        grid=(B, L // tq),
        in_specs=[
            pl.BlockSpec((1, L, IDX_DIM), lambda b, q: (b, 0, 0)),
            pl.BlockSpec((1, L, KV_LATENT), lambda b, q: (b, 0, 0)),
            pl.BlockSpec((1, KV_LATENT, L), lambda b, q: (b, 0, 0)),
            pl.BlockSpec((1, IDX_Q, tq), lambda b, q: (b, 0, q)),
            pl.BlockSpec((1, IDX_HEADS, tq), lambda b, q: (b, 0, q)),
            pl.BlockSpec((1, ATT_HEADS * KV_LATENT, tq), lambda b, q: (b, 0, q)),
            pl.BlockSpec((ATT_HEADS, ATT_HEAD_DIM, KV_LATENT), lambda b, q: (0, 0, 0)),
        ],
        out_specs=pl.BlockSpec((1, tq, ATT_Q), lambda b, q: (b, q, 0)),
        scratch_shapes=[
            pltpu.VMEM((L, tq), I32),
            pltpu.VMEM((ATT_HEADS, KV_LATENT, tq), F32),
            pltpu.VMEM((ATT_HEADS, 1, tq), F32),
            pltpu.VMEM((ATT_HEADS, 1, tq), F32),
            pltpu.VMEM((1, tq), I32),
            pltpu.VMEM((ATT_Q, tq), F32),
        ],
        compiler_params=pltpu.CompilerParams(dimension_semantics=("parallel", "parallel"),
                                             vmem_limit_bytes=VMEM_LIMIT_BYTES),
        name="dsa",
        interpret=interpret,
    )(kidx, c, ct, qidxt, widxt, qlatt, wuvt)


HG_GROUP = 128
HG_PAIRS = HG_HEADS // 2


def _hgrn_kernel(lbl_ref, gn_ref, hq_ref, hf_ref, hi_ref, hg_ref, out_ref, st_ref, *, ts, layer):
    @pl.when(pl.program_id(1) == 0)
    def _():
        st_ref[...] = jnp.zeros(st_ref.shape, F32)

    logits = lbl_ref[...]
    e = jnp.exp(logits - jnp.max(logits, axis=0, keepdims=True))
    lb = jnp.sum(e[0:layer + 1, :], axis=0, keepdims=True) / jnp.sum(e, axis=0, keepdims=True)

    g = HG_GROUP
    chunk_shift = HG_CHUNK.bit_length() - 1
    head_shift = HG_DIM.bit_length() - 1
    r = lax.broadcasted_iota(I32, (g, g), 0)
    cidx = lax.broadcasted_iota(I32, (g, g), 1)
    same_chunk = (r >> chunk_shift) == (cidx >> chunk_shift)
    causal = same_chunk & (cidx <= r)
    tri = jnp.where(causal, 1.0, 0.0).astype(F32)
    ones_blk = jnp.where(same_chunk, 1.0, 0.0).astype(F32)
    same_head = (r >> head_shift) == (cidx >> head_shift)
    head_blk = jnp.where(same_head, 1.0, 0.0).astype(F32)
    lane_lo = lax.broadcasted_iota(I32, (g, LANES), 1) < HG_DIM
    col_chunk = cidx >> chunk_shift
    gn = gn_ref[...]

    def group(gi, carry):
        r0 = pl.multiple_of(gi * g, g)
        rows = pl.ds(r0, g)
        f = lb + (1.0 - lb) * jax.nn.sigmoid(hf_ref[0, rows, :])
        logf = jnp.log(f)
        kgate = 1.0 - f
        b = _dot_exact(tri, logf)
        btot = _dot_exact(ones_blk, logf)
        q_t = jax.nn.silu(hq_ref[0, rows, :]) * jnp.exp(b)
        k_t = kgate * jnp.exp(-b)
        k_dec = kgate * jnp.exp(btot - b)
        decay = jnp.exp(btot)
        v = hi_ref[0, rows, :]
        gate = hg_ref[0, rows, :]
        for p in range(HG_PAIRS):
            ls = slice(p * LANES, (p + 1) * LANES)
            qp = q_t[:, ls]
            kp = k_t[:, ls].astype(BF16)
            vp = v[:, ls].astype(BF16)
            kdp = k_dec[:, ls].astype(BF16)
            q_lo = jnp.where(lane_lo, qp, 0.0).astype(BF16)
            q_hi = jnp.where(lane_lo, 0.0, qp).astype(BF16)
            a_lo = jnp.where(causal, _dot_nt(q_lo, kp), 0.0).astype(BF16)
            a_hi = jnp.where(causal, _dot_nt(q_hi, kp), 0.0).astype(BF16)
            o = jnp.where(lane_lo, _dot(a_lo, vp), _dot(a_hi, vp))
            st = st_ref[p]
            qpb = qp.astype(BF16)
            vt = v[:, ls].T
            inter = []
            for ch in range(g // HG_CHUNK):
                rs = slice(ch * HG_CHUNK, (ch + 1) * HG_CHUNK)
                inter.append(_dot_nt(qpb[rs], st.astype(BF16)))
                vt_ch = jnp.where(col_chunk == ch, vt, 0.0).astype(BF16)
                ds = _dot(vt_ch, kdp)
                st = st * decay[ch * HG_CHUNK:ch * HG_CHUNK + 1, ls] + jnp.where(same_head, ds, 0.0)
            st_ref[p] = st
            o = o + jnp.concatenate(inter, axis=0)
            ms = _dot_exact(o * o, head_blk) * (1.0 / HG_DIM)
            gt = gate[:, ls]
            on = o * lax.rsqrt(ms + EPS) * gn * (gt * jax.nn.sigmoid(gt))
            out_ref[0, rows, ls] = on.astype(BF16)
        return carry

    lax.fori_loop(0, ts // g, group, 0)


def _hgrn(hq, hf, hi, hg, lb_logits, g_norm, *, ts, layer, interpret=False):
    B, L, W = hq.shape
    gn = jnp.tile(g_norm.reshape(1, HG_DIM).astype(F32), (1, 2))
    blk = pl.BlockSpec((1, ts, W), lambda b, t: (b, t, 0))
    return pl.pallas_call(
        functools.partial(_hgrn_kernel, ts=ts, layer=layer),
        out_shape=jax.ShapeDtypeStruct((B, L, W), BF16),
        grid=(B, L // ts),
        in_specs=[pl.BlockSpec(lb_logits.shape, lambda b, t: (0, 0)),
                  pl.BlockSpec((1, LANES), lambda b, t: (0, 0)), blk, blk, blk, blk],
        out_specs=blk,
        scratch_shapes=[pltpu.VMEM((HG_PAIRS, LANES, LANES), F32)],
        compiler_params=pltpu.CompilerParams(dimension_semantics=("parallel", "arbitrary"),
                                             vmem_limit_bytes=VMEM_LIMIT_BYTES),
        name="hgrn2",
        interpret=interpret,
    )(lb_logits.astype(F32), gn, hq, hf, hi, hg)


def _layer_norm(y, g, b):
    mu = jnp.mean(y, axis=-1, keepdims=True)
    d = y - mu
    var = jnp.mean(d * d, axis=-1, keepdims=True)
    return d * lax.rsqrt(var + EPS) * g + b


def _top_desc(s, n):
    vals = []
    cur = s
    for i in range(n):
        mx = jnp.max(cur, axis=0, keepdims=True)
        vals.append(mx)
        if i + 1 < n:
            cur = jnp.where(cur == mx, -jnp.inf, cur)
    return vals


_PAIR_RANKS = [(i, j) for i in range(1, PEER_TOPK + 2) for j in range(1, PEER_TOPK + 2) if i * j <= PEER_TOPK + 1]


def _route_kernel(att_ref, hg_ref, x_ref, woa_ref, wob_ref, g1_ref, b1_ref, wq_ref, sk_ref,
                  h_ref, hb_ref, ct_ref, at_ref, s2_ref, bm_ref, *, alpha):
    mix = _dot(att_ref[...], woa_ref[...]) + _dot(hg_ref[...], wob_ref[...])
    h = _layer_norm(alpha * x_ref[...] + mix, g1_ref[...], b1_ref[...])
    h_ref[...] = h
    hb = h.astype(BF16)
    hb_ref[...] = hb
    qp = _dot(hb, wq_ref[...]).astype(BF16)
    dk = PEER_N_KEYS
    for hd in range(PEER_HEADS):
        s1 = _dot_nt(sk_ref[hd, 0], qp[:, (2 * hd) * dk:(2 * hd + 1) * dk])
        s2 = _dot_nt(sk_ref[hd, 1], qp[:, (2 * hd + 1) * dk:(2 * hd + 2) * dk])
        a = _top_desc(s1, PEER_TOPK + 1)
        bb = _top_desc(s2, PEER_TOPK + 1)
        cands = [a[i - 1] + bb[j - 1] for i, j in _PAIR_RANKS]
        pad = (-len(cands)) % 8
        cand = jnp.concatenate(cands + [jnp.full_like(cands[0], -jnp.inf)] * pad, axis=0)
        top = _top_desc(cand, PEER_TOPK + 1)
        c_max, c_k, c_next = top[0], top[PEER_TOPK - 1], top[PEER_TOPK]
        thr = 0.5 * (c_k + c_next)
        z = jnp.sum(jnp.where(cand >= c_k, jnp.exp(cand - c_max), 0.0), axis=0, keepdims=True)
        ct_ref[hd] = thr - s1
        at_ref[hd] = jnp.exp(s1 - a[0]) / z
        s2_ref[hd] = s2
        bm_ref[hd] = jnp.exp(s2 - bb[0])


def _route(att, hgo, x2, w_out, g1, b1, w_q, sub_keys, *, tb, alpha, interpret=False):
    T, D = x2.shape
    woa = w_out[:ATT_Q].astype(BF16)
    wob = w_out[ATT_Q:].astype(BF16)

    def full(a):
        zeros = (0,) * a.ndim
        return pl.BlockSpec(a.shape, lambda t: zeros)

    ins = [att, hgo, x2, woa, wob, g1.reshape(1, D).astype(F32), b1.reshape(1, D).astype(F32),
           w_q.astype(BF16), sub_keys.astype(BF16)]
    in_specs = [pl.BlockSpec((tb, ATT_Q), lambda t: (t, 0)), pl.BlockSpec((tb, HG_W), lambda t: (t, 0)),
                pl.BlockSpec((tb, D), lambda t: (t, 0))] + [full(a) for a in ins[3:]]
    rt_shape = jax.ShapeDtypeStruct((PEER_HEADS, PEER_N_KEYS, T), F32)
    rt_spec = pl.BlockSpec((PEER_HEADS, PEER_N_KEYS, tb), lambda t: (0, 0, t))
    return pl.pallas_call(
        functools.partial(_route_kernel, alpha=alpha),
        out_shape=[jax.ShapeDtypeStruct((T, D), F32), jax.ShapeDtypeStruct((T, D), BF16),
                   rt_shape, rt_shape, rt_shape, rt_shape],
        grid=(T // tb,),
        in_specs=in_specs,
        out_specs=[pl.BlockSpec((tb, D), lambda t: (t, 0)), pl.BlockSpec((tb, D), lambda t: (t, 0)),
                   rt_spec, rt_spec, rt_spec, rt_spec],
        compiler_params=pltpu.CompilerParams(dimension_semantics=("parallel",),
                                             vmem_limit_bytes=VMEM_LIMIT_BYTES),
        name="route",
        interpret=interpret,
    )(*ins)


def _peer_kernel(hb_ref, h_ref, u_ref, vt_ref, ct_ref, at_ref, s2_ref, bm_ref, g2_ref, b2_ref,
                 out_ref, acc_ref, p_ref, *, ic, alpha):
    c = pl.program_id(1)

    @pl.when(c == 0)
    def _():
        acc_ref[...] = jnp.zeros(acc_ref.shape, F32)

    hb = hb_ref[...]
    nk = PEER_N_KEYS
    for ii in range(ic):
        pre = _dot_nt(u_ref[ii * nk:(ii + 1) * nk, :], hb)
        act = jax.nn.gelu(pre)
        gsum = jnp.zeros(pre.shape, F32)
        for hd in range(PEER_HEADS):
            sel = jnp.where(s2_ref[hd] >= ct_ref[hd, ii:ii + 1, :], bm_ref[hd], 0.0)
            gsum = gsum + sel * at_ref[hd, ii:ii + 1, :]
        p_ref[ii * nk:(ii + 1) * nk, :] = (gsum * act).astype(BF16)
    acc_ref[...] += _dot(vt_ref[...], p_ref[...])

    @pl.when(c == pl.num_programs(1) - 1)
    def _():
        y = alpha * h_ref[...] + acc_ref[...].T
        out_ref[...] = _layer_norm(y, g2_ref[...], b2_ref[...])


def _peer(hb, h, u, v, ct, at, s2, bm, g2, b2, *, tb, ic, alpha, interpret=False):
    T, D = h.shape
    n_exp = u.shape[0]
    ub = u.astype(BF16)
    vtb = v.astype(BF16).T
    ec = ic * PEER_N_KEYS
    rows = pl.BlockSpec((PEER_HEADS, ic, tb), lambda t, c: (0, c, t))
    whole = pl.BlockSpec((PEER_HEADS, PEER_N_KEYS, tb), lambda t, c: (0, 0, t))
    vec = pl.BlockSpec((1, D), lambda t, c: (0, 0))
    return pl.pallas_call(
        functools.partial(_peer_kernel, ic=ic, alpha=alpha),
        out_shape=jax.ShapeDtypeStruct((T, D), F32),
        grid=(T // tb, n_exp // ec),
        in_specs=[pl.BlockSpec((tb, D), lambda t, c: (t, 0)), pl.BlockSpec((tb, D), lambda t, c: (t, 0)),
                  pl.BlockSpec((ec, D), lambda t, c: (c, 0)), pl.BlockSpec((D, ec), lambda t, c: (0, c)),
                  rows, rows, whole, whole, vec, vec],
        out_specs=pl.BlockSpec((tb, D), lambda t, c: (t, 0)),
        scratch_shapes=[pltpu.VMEM((D, tb), F32), pltpu.VMEM((ec, tb), BF16)],
        compiler_params=pltpu.CompilerParams(dimension_semantics=("parallel", "arbitrary"),
                                             vmem_limit_bytes=VMEM_LIMIT_BYTES),
        name="peer",
        interpret=interpret,
    )(hb, h, ub, vtb, ct, at, s2, bm, g2.reshape(1, D).astype(F32), b2.reshape(1, D).astype(F32))


def _tile(n, pref):
    t = min(pref, n)
    assert n % t == 0, (n, t)
    return t


def kernel(x, w_in, kv_norm_g, w_uk, w_uv, hg_lb_logits, hg_norm_g, w_out, ln1_g, ln1_b,
           peer_w_q, peer_sub_keys, peer_u, peer_v, ln2_g, ln2_b):
    B, L, D = x.shape
    depth = w_in.shape[0]
    alpha = (2.0 * depth) ** 0.25
    T = B * L
    for layer in range(depth):
        (c, ct, kidx, qidxt, widxt, qlatt, hq, hf, hi, hg) = _proj(
            x, w_in[layer], kv_norm_g[layer], w_uk[layer], tb=_tile(L, 512))
        tq = _tile(L, 256)
        att = _dsa(kidx, c, ct, qidxt, widxt, qlatt, w_uv[layer], tq=tq, kc=min(128, tq))
        hgo = _hgrn(hq, hf, hi, hg, hg_lb_logits, hg_norm_g[layer], ts=_tile(L, 512), layer=layer)
        h, hb, rct, rat, rs2, rbm = _route(
            att.reshape(T, ATT_Q), hgo.reshape(T, HG_W), x.reshape(T, D), w_out[layer], ln1_g[layer], ln1_b[layer],
            peer_w_q[layer], peer_sub_keys[layer], tb=_tile(T, 512), alpha=alpha)
        y = _peer(hb, h, peer_u[layer], peer_v[layer], rct, rat, rs2, rbm, ln2_g[layer], ln2_b[layer],
                  tb=_tile(T, 512), ic=8, alpha=alpha)
        x = y.reshape(B, L, D)
    return x
```

```python
import functools

import jax
import jax.numpy as jnp
from jax import lax
from jax.experimental import pallas as pl
from jax.experimental.pallas import tpu as pltpu

F32 = jnp.float32
BF16 = jnp.bfloat16
I32 = jnp.int32

ATT_HEADS = 8
ATT_HEAD_DIM = 64
KV_LATENT = 128
IDX_HEADS = 8
IDX_DIM = 64
TOPK_MAX = 256
HG_HEADS = 8
HG_DIM = 64
HG_CHUNK = 32
PEER_HEADS = 8
PEER_N_KEYS = 128
PEER_TOPK = 16
EPS = 1e-5

ATT_Q = ATT_HEADS * ATT_HEAD_DIM
IDX_Q = IDX_HEADS * IDX_DIM
HG_W = HG_HEADS * HG_DIM

LANES = 128
VMEM_LIMIT_BYTES = 56 * 1024 * 1024

LOG2E = 1.4426950408889634
NEG_BIG = -1e30
INT_MIN = -(2 ** 31)
KEY_NEG_INF = INT_MIN + 0x7FFFFF


def _dot(a, b):
    return jnp.dot(a, b, preferred_element_type=F32)


def _dot_nt(a, b):
    return lax.dot_general(a, b, (((1,), (1,)), ((), ())), preferred_element_type=F32)


def _dot_exact(a, b):
    return jnp.dot(a, b, preferred_element_type=F32, precision=lax.Precision.HIGHEST)


def _proj_kernel(x_ref, wc_ref, wk_ref, whq_ref, whf_ref, whi_ref, whg_ref,
                 wtq_ref, wtqi_ref, wtwi_ref, wtc_ref, grow_ref, gcol_ref, wuk_ref,
                 c_ref, ct_ref, kidx_ref, qidxt_ref, widxt_ref, qlatt_ref,
                 hq_ref, hf_ref, hi_ref, hg_ref):
    xb = x_ref[0].astype(BF16)

    c = _dot(xb, wc_ref[...])
    c = c * lax.rsqrt(jnp.mean(c * c, axis=-1, keepdims=True) + EPS) * grow_ref[...]
    c_ref[0] = c.astype(BF16)
    ct = _dot_nt(wtc_ref[...], xb)
    ct = ct * lax.rsqrt(jnp.mean(ct * ct, axis=0, keepdims=True) + EPS) * gcol_ref[...]
    ct_ref[0] = ct.astype(BF16)

    kidx_ref[0] = _dot(xb, wk_ref[...]).astype(BF16)
    qidxt_ref[0] = _dot_nt(wtqi_ref[...], xb).astype(BF16)
    widxt_ref[0] = _dot_nt(wtwi_ref[...], xb)

    qt = _dot_nt(wtq_ref[...], xb).astype(BF16)
    scale = ATT_HEAD_DIM ** -0.5 * LOG2E
    for h in range(ATT_HEADS):
        ql = _dot(wuk_ref[h], qt[h * ATT_HEAD_DIM:(h + 1) * ATT_HEAD_DIM, :]) * scale
        qlatt_ref[0, h * KV_LATENT:(h + 1) * KV_LATENT, :] = ql.astype(BF16)

    hq_ref[0] = _dot(xb, whq_ref[...])
    hf_ref[0] = _dot(xb, whf_ref[...])
    hi_ref[0] = _dot(xb, whi_ref[...])
    hg_ref[0] = _dot(xb, whg_ref[...])


def _proj(x, w, kv_g, w_uk, *, tb, interpret=False):
    B, L, D = x.shape
    o = 0
    cols = {}
    for name, width in (("q", ATT_Q), ("c", KV_LATENT), ("qi", IDX_Q), ("k", IDX_DIM), ("wi", IDX_HEADS),
                        ("hq", HG_W), ("hf", HG_W), ("hi", HG_W), ("hg", HG_W)):
        cols[name] = w[:, o:o + width].astype(BF16)
        o += width
    assert o == w.shape[1]
    wt = lambda n: cols[n].T
    ins = [x, cols["c"], cols["k"], cols["hq"], cols["hf"], cols["hi"], cols["hg"],
           wt("q"), wt("qi"), wt("wi"), wt("c"),
           kv_g.reshape(1, KV_LATENT).astype(F32), kv_g.reshape(KV_LATENT, 1).astype(F32),
           w_uk.astype(BF16)]

    def full(a):
        zeros = (0,) * a.ndim
        return pl.BlockSpec(a.shape, lambda b, t: zeros)

    in_specs = [pl.BlockSpec((1, tb, D), lambda b, t: (b, t, 0))] + [full(a) for a in ins[1:]]
    nat = lambda wd, dt: (jax.ShapeDtypeStruct((B, L, wd), dt), pl.BlockSpec((1, tb, wd), lambda b, t: (b, t, 0)))
    tr = lambda wd, dt: (jax.ShapeDtypeStruct((B, wd, L), dt), pl.BlockSpec((1, wd, tb), lambda b, t: (b, 0, t)))
    outs = [nat(KV_LATENT, BF16), tr(KV_LATENT, BF16), nat(IDX_DIM, BF16), tr(IDX_Q, BF16), tr(IDX_HEADS, F32),
            tr(ATT_HEADS * KV_LATENT, BF16), nat(HG_W, F32), nat(HG_W, F32), nat(HG_W, F32), nat(HG_W, F32)]
    return pl.pallas_call(
        _proj_kernel,
        out_shape=[s for s, _ in outs],
        grid=(B, L // tb),
        in_specs=in_specs,
        out_specs=[s for _, s in outs],
        compiler_params=pltpu.CompilerParams(dimension_semantics=("parallel", "parallel"),
                                             vmem_limit_bytes=VMEM_LIMIT_BYTES),
        name="proj",
        interpret=interpret,
    )(*ins)


def _dsa_kernel(kidx_ref, c_ref, ct_ref, qidxt_ref, widxt_ref, qlatt_ref, wuvt_ref,
                out_ref, keys_ref, acc_ref, m_ref, l_ref, alpha_ref, jb_ref, ot_ref, a_ref, p_ref,
                *, tq, ks, kc, top_k, seq):
    qi = pl.program_id(1)
    n_keys = (qi + 1) * tq
    nks = (qi + 1) * (tq // ks)
    nkc = (qi + 1) * (tq // kc)
    q_pos = lambda rows: qi * tq + lax.broadcasted_iota(I32, (rows, tq), 1)
    k_off = lambda rows: lax.broadcasted_iota(I32, (rows, tq), 0)

    def score_chunk(k, carry):
        r0 = pl.multiple_of(k * ks, ks)
        kk = kidx_ref[0, pl.ds(r0, ks), :]
        s = jnp.zeros((ks, tq), F32)
        for h in range(IDX_HEADS):
            z = _dot(kk, qidxt_ref[0, h * IDX_DIM:(h + 1) * IDX_DIM, :])
            s = s + widxt_ref[0, h:h + 1, :] * jnp.maximum(z, 0.0)
        s = jnp.where(r0 + k_off(ks) <= q_pos(ks), s, -jnp.inf)
        bits = pltpu.bitcast(s, I32)
        keys_ref[pl.ds(r0, ks), :] = bits ^ ((bits >> 31) & 0x7FFFFFFF)
        return carry

    lax.fori_loop(0, nks, score_chunk, 0)

    def count(pred):
        def body(k, cnt):
            r0 = pl.multiple_of(k * kc, kc)
            blk = keys_ref[pl.ds(r0, kc), :]
            return cnt + jnp.sum(jnp.where(pred(blk, r0), 1, 0).astype(I32), axis=0, keepdims=True)
        return lax.fori_loop(0, nkc, body, jnp.zeros((1, tq), I32))

    def bit_cond(carry):
        i, _, _, open_lanes = carry
        return jnp.logical_and(i < 32, open_lanes > 0)

    def bit_step(carry):
        i, t, n_t, _ = carry
        cand = t + lax.shift_left(jnp.int32(1), 31 - i)
        cnt = count(lambda blk, r0: blk >= cand)
        take = cnt >= top_k
        t = jnp.where(take, cand, t)
        n_t = jnp.where(take, cnt, n_t)
        return i + 1, t, n_t, jnp.max(jnp.abs(n_t - top_k))

    _, thr, n_ge, _ = lax.while_loop(
        bit_cond, bit_step,
        (jnp.int32(0), jnp.full((1, tq), INT_MIN, I32), jnp.full((1, tq), n_keys, I32), jnp.int32(1)))

    has_tie = jnp.where(n_ge > top_k, jnp.where(thr > KEY_NEG_INF, 1, 0), 0)
    jb_ref[...] = jnp.full((1, tq), seq, I32)

    @pl.when(jnp.max(has_tie) > 0)
    def _():
        need = top_k - count(lambda blk, r0: blk > thr)

        def idx_step(i, v):
            cand = v + lax.shift_left(jnp.int32(1), (seq.bit_length() - 1) - i)
            cnt = count(lambda blk, r0: (blk == thr) & (r0 + k_off(kc) < cand))
            return jnp.where(cnt < need, cand, v)
        v = lax.fori_loop(0, seq.bit_length(), idx_step, jnp.zeros((1, tq), I32))
        jb_ref[...] = jnp.where(has_tie > 0, v, seq)

    jb = jb_ref[...]

    m_ref[...] = jnp.full(m_ref.shape, NEG_BIG, F32)
    l_ref[...] = jnp.zeros(l_ref.shape, F32)
    acc_ref[...] = jnp.zeros(acc_ref.shape, F32)

    def att_chunk(k, carry):
        r0 = pl.multiple_of(k * kc, kc)
        key = keys_ref[pl.ds(r0, kc), :]
        s_idx = r0 + k_off(kc)
        keep = jnp.where(key > thr, 0.0, jnp.where(key == thr, jnp.where(s_idx <= jb, 0.0, NEG_BIG), NEG_BIG))
        bias = jnp.where(s_idx <= q_pos(kc), keep, NEG_BIG)
        cc = c_ref[0, pl.ds(r0, kc), :]
        cct = ct_ref[0, :, pl.ds(r0, kc)]
        for h in range(ATT_HEADS):
            a = _dot(cc, qlatt_ref[0, h * KV_LATENT:(h + 1) * KV_LATENT, :]) + bias
            a_ref[h] = a
            m_old = m_ref[h]
            m_new = jnp.maximum(m_old, jnp.max(a, axis=0, keepdims=True))
            alpha_ref[h] = jnp.exp2(m_old - m_new)
            m_ref[h] = m_new
        for h in range(ATT_HEADS):
            p = jnp.exp2(a_ref[h] - m_ref[h])
            l_ref[h] = alpha_ref[h] * l_ref[h] + jnp.sum(p, axis=0, keepdims=True)
            p_ref[h] = p.astype(BF16)
        for h in range(ATT_HEADS):
            acc_ref[h] = alpha_ref[h] * acc_ref[h] + _dot(cct, p_ref[h])
        return carry

    lax.fori_loop(0, nkc, att_chunk, 0)

    for h in range(ATT_HEADS):
        o_lat = (acc_ref[h] / l_ref[h]).astype(BF16)
        ot_ref[h * ATT_HEAD_DIM:(h + 1) * ATT_HEAD_DIM, :] = _dot(wuvt_ref[h], o_lat)
    out_ref[0] = ot_ref[...].T.astype(BF16)


def _dsa(kidx, c, ct, qidxt, widxt, qlatt, w_uv, *, tq, ks, kc, interpret=False):
    B, L, _ = c.shape
    top_k = min(TOPK_MAX, L // 4)
    assert tq % ks == 0 and tq % kc == 0 and top_k <= tq
    wuvt = jnp.swapaxes(w_uv, 1, 2).astype(BF16)
    kernel = functools.partial(_dsa_kernel, tq=tq, ks=ks, kc=kc, top_k=top_k, seq=L)
    return pl.pallas_call(
        kernel,
        out_shape=jax.ShapeDtypeStruct((B, L, ATT_Q), BF16),
        grid=(B, L // tq),
        in_specs=[
            pl.BlockSpec((1, L, IDX_DIM), lambda b, q: (b, 0, 0)),
            pl.BlockSpec((1, L, KV_LATENT), lambda b, q: (b, 0, 0)),
            pl.BlockSpec((1, KV_LATENT, L), lambda b, q: (b, 0, 0)),
            pl.BlockSpec((1, IDX_Q, tq), lambda b, q: (b, 0, q)),
            pl.BlockSpec((1, IDX_HEADS, tq), lambda b, q: (b, 0, q)),
            pl.BlockSpec((1, ATT_HEADS * KV_LATENT, tq), lambda b, q: (b, 0, q)),
            pl.BlockSpec((ATT_HEADS, ATT_HEAD_DIM, KV_LATENT), lambda b, q: (0, 0, 0)),
        ],
        out_specs=pl.BlockSpec((1, tq, ATT_Q), lambda b, q: (b, q, 0)),
        scratch_shapes=[
            pltpu.VMEM((L, tq), I32),
            pltpu.VMEM((ATT_HEADS, KV_LATENT, tq), F32),
            pltpu.VMEM((ATT_HEADS, 1, tq), F32),
            pltpu.VMEM((ATT_HEADS, 1, tq), F32),
            pltpu.VMEM((ATT_HEADS, 1, tq), F32),
            pltpu.VMEM((1, tq), I32),
            pltpu.VMEM((ATT_Q, tq), F32),
            pltpu.VMEM((ATT_HEADS, kc, tq), F32),
            pltpu.VMEM((ATT_HEADS, kc, tq), BF16),
        ],
        compiler_params=pltpu.CompilerParams(dimension_semantics=("parallel", "parallel"),
                                             vmem_limit_bytes=VMEM_LIMIT_BYTES),
        name="dsa",
        interpret=interpret,
    )(kidx, c, ct, qidxt, widxt, qlatt, wuvt)


HG_GROUP = 128
HG_PAIRS = HG_HEADS // 2


def _hgrn_kernel(lbl_ref, gn_ref, hq_ref, hf_ref, hi_ref, hg_ref, out_ref, st_ref, *, ts, layer):
    @pl.when(pl.program_id(1) == 0)
    def _():
        st_ref[...] = jnp.zeros(st_ref.shape, F32)

    logits = lbl_ref[...]
    e = jnp.exp(logits - jnp.max(logits, axis=0, keepdims=True))
    lb = jnp.sum(e[0:layer + 1, :], axis=0, keepdims=True) / jnp.sum(e, axis=0, keepdims=True)

    g = HG_GROUP
    chunk_shift = HG_CHUNK.bit_length() - 1
    head_shift = HG_DIM.bit_length() - 1
    r = lax.broadcasted_iota(I32, (g, g), 0)
    cidx = lax.broadcasted_iota(I32, (g, g), 1)
    same_chunk = (r >> chunk_shift) == (cidx >> chunk_shift)
    causal = same_chunk & (cidx <= r)
    tri = jnp.where(causal, 1.0, 0.0).astype(F32)
    ones_blk = jnp.where(same_chunk, 1.0, 0.0).astype(F32)
    same_head = (r >> head_shift) == (cidx >> head_shift)
    head_blk = jnp.where(same_head, 1.0, 0.0).astype(F32)
    lane_lo = lax.broadcasted_iota(I32, (g, LANES), 1) < HG_DIM
    col_chunk = cidx >> chunk_shift
    gn = gn_ref[...]

    def group(gi, carry):
        r0 = pl.multiple_of(gi * g, g)
        rows = pl.ds(r0, g)
        f = lb + (1.0 - lb) * jax.nn.sigmoid(hf_ref[0, rows, :])
        logf = jnp.log(f)
        kgate = 1.0 - f
        b = _dot_exact(tri, logf)
        btot = _dot_exact(ones_blk, logf)
        q_t = jax.nn.silu(hq_ref[0, rows, :]) * jnp.exp(b)
        k_t = kgate * jnp.exp(-b)
        k_dec = kgate * jnp.exp(btot - b)
        decay = jnp.exp(btot)
        v = hi_ref[0, rows, :]
        gate = hg_ref[0, rows, :]
        for p in range(HG_PAIRS):
            ls = slice(p * LANES, (p + 1) * LANES)
            qp = q_t[:, ls]
            kp = k_t[:, ls].astype(BF16)
            vp = v[:, ls].astype(BF16)
            kdp = k_dec[:, ls].astype(BF16)
            q_lo = jnp.where(lane_lo, qp, 0.0).astype(BF16)
            q_hi = jnp.where(lane_lo, 0.0, qp).astype(BF16)
            a_lo = jnp.where(causal, _dot_nt(q_lo, kp), 0.0).astype(BF16)
            a_hi = jnp.where(causal, _dot_nt(q_hi, kp), 0.0).astype(BF16)
            o = jnp.where(lane_lo, _dot(a_lo, vp), _dot(a_hi, vp))
            st = st_ref[p]
            qpb = qp.astype(BF16)
            vt = v[:, ls].T
            inter = []
            for ch in range(g // HG_CHUNK):
                rs = slice(ch * HG_CHUNK, (ch + 1) * HG_CHUNK)
                inter.append(_dot_nt(qpb[rs], st.astype(BF16)))
                vt_ch = jnp.where(col_chunk == ch, vt, 0.0).astype(BF16)
                ds = _dot(vt_ch, kdp)
                st = st * decay[ch * HG_CHUNK:ch * HG_CHUNK + 1, ls] + jnp.where(same_head, ds, 0.0)
            st_ref[p] = st
            o = o + jnp.concatenate(inter, axis=0)
            ms = _dot_exact(o * o, head_blk) * (1.0 / HG_DIM)
            gt = gate[:, ls]
            on = o * lax.rsqrt(ms + EPS) * gn * (gt * jax.nn.sigmoid(gt))
            out_ref[0, rows, ls] = on.astype(BF16)
        return carry

    lax.fori_loop(0, ts // g, group, 0)


def _hgrn(hq, hf, hi, hg, lb_logits, g_norm, *, ts, layer, interpret=False):
    B, L, W = hq.shape
    gn = jnp.tile(g_norm.reshape(1, HG_DIM).astype(F32), (1, 2))
    blk = pl.BlockSpec((1, ts, W), lambda b, t: (b, t, 0))
    return pl.pallas_call(
        functools.partial(_hgrn_kernel, ts=ts, layer=layer),
        out_shape=jax.ShapeDtypeStruct((B, L, W), BF16),
        grid=(B, L // ts),
        in_specs=[pl.BlockSpec(lb_logits.shape, lambda b, t: (0, 0)),
                  pl.BlockSpec((1, LANES), lambda b, t: (0, 0)), blk, blk, blk, blk],
        out_specs=blk,
        scratch_shapes=[pltpu.VMEM((HG_PAIRS, LANES, LANES), F32)],
        compiler_params=pltpu.CompilerParams(dimension_semantics=("parallel", "arbitrary"),
                                             vmem_limit_bytes=VMEM_LIMIT_BYTES),
        name="hgrn2",
        interpret=interpret,
    )(lb_logits.astype(F32), gn, hq, hf, hi, hg)


def _layer_norm(y, g, b):
    mu = jnp.mean(y, axis=-1, keepdims=True)
    d = y - mu
    var = jnp.mean(d * d, axis=-1, keepdims=True)
    return d * lax.rsqrt(var + EPS) * g + b


def _top_desc(s, n, with_rank=False):
    vals = []
    cur = s
    rank = jnp.full(s.shape, float(n), F32) if with_rank else None
    for i in range(n):
        mx = jnp.max(cur, axis=0, keepdims=True)
        vals.append(mx)
        eq = cur == mx
        if with_rank:
            rank = jnp.where(eq, float(i), rank)
        if i + 1 < n:
            cur = jnp.where(eq, -jnp.inf, cur)
    return (vals, rank) if with_rank else vals


PEER_KEEP = PEER_TOPK + 1
_PAIR_RANKS = [(i, j) for i in range(1, PEER_KEEP + 1) for j in range(1, PEER_KEEP + 1) if i * j <= PEER_KEEP]


def _route_kernel(att_ref, hg_ref, x_ref, woa_ref, wob_ref, g1_ref, b1_ref, wq_ref, sk_ref,
                  h_ref, hb_ref, cnt_ref, at_ref, rk_ref, bm_ref, *, alpha):
    mix = _dot(att_ref[...], woa_ref[...]) + _dot(hg_ref[...], wob_ref[...])
    h = _layer_norm(alpha * x_ref[...] + mix, g1_ref[...], b1_ref[...])
    h_ref[...] = h
    hb = h.astype(BF16)
    hb_ref[...] = hb
    qp = _dot(hb, wq_ref[...]).astype(BF16)
    dk = PEER_N_KEYS
    for hd in range(PEER_HEADS):
        s1 = _dot_nt(sk_ref[hd, 0], qp[:, (2 * hd) * dk:(2 * hd + 1) * dk])
        s2 = _dot_nt(sk_ref[hd, 1], qp[:, (2 * hd + 1) * dk:(2 * hd + 2) * dk])
        a = _top_desc(s1, PEER_KEEP)
        bb, rank2 = _top_desc(s2, PEER_KEEP, with_rank=True)
        cands = [a[i - 1] + bb[j - 1] for i, j in _PAIR_RANKS]
        pad = (-len(cands)) % 8
        cand = jnp.concatenate(cands + [jnp.full_like(cands[0], -jnp.inf)] * pad, axis=0)
        top = _top_desc(cand, PEER_KEEP)
        c_max, c_k, c_next = top[0], top[PEER_TOPK - 1], top[PEER_TOPK]
        thr = 0.5 * (c_k + c_next)
        z = jnp.sum(jnp.where(cand >= c_k, jnp.exp(cand - c_max), 0.0), axis=0, keepdims=True)
        cnt = jnp.zeros(s1.shape, F32)
        for r in range(PEER_KEEP):
            cnt = cnt + jnp.where(s1 >= thr - bb[r], 1.0, 0.0)
        cnt_ref[hd] = cnt
        at_ref[hd] = jnp.exp(s1 - a[0]) / z
        rk_ref[hd] = rank2.astype(BF16)
        bm_ref[hd] = jnp.exp(s2 - bb[0]).astype(BF16)


def _route(att, hgo, x2, w_out, g1, b1, w_q, sub_keys, *, tb, alpha, interpret=False):
    T, D = x2.shape
    woa = w_out[:ATT_Q].astype(BF16)
    wob = w_out[ATT_Q:].astype(BF16)

    def full(a):
        zeros = (0,) * a.ndim
        return pl.BlockSpec(a.shape, lambda t: zeros)

    ins = [att, hgo, x2, woa, wob, g1.reshape(1, D).astype(F32), b1.reshape(1, D).astype(F32),
           w_q.astype(BF16), sub_keys.astype(BF16)]
    in_specs = [pl.BlockSpec((tb, ATT_Q), lambda t: (t, 0)), pl.BlockSpec((tb, HG_W), lambda t: (t, 0)),
                pl.BlockSpec((tb, D), lambda t: (t, 0))] + [full(a) for a in ins[3:]]
    rt_shape = lambda dt: jax.ShapeDtypeStruct((PEER_HEADS, PEER_N_KEYS, T), dt)
    rt_spec = pl.BlockSpec((PEER_HEADS, PEER_N_KEYS, tb), lambda t: (0, 0, t))
    return pl.pallas_call(
        functools.partial(_route_kernel, alpha=alpha),
        out_shape=[jax.ShapeDtypeStruct((T, D), F32), jax.ShapeDtypeStruct((T, D), BF16),
                   rt_shape(F32), rt_shape(F32), rt_shape(BF16), rt_shape(BF16)],
        grid=(T // tb,),
        in_specs=in_specs,
        out_specs=[pl.BlockSpec((tb, D), lambda t: (t, 0)), pl.BlockSpec((tb, D), lambda t: (t, 0)),
                   rt_spec, rt_spec, rt_spec, rt_spec],
        compiler_params=pltpu.CompilerParams(dimension_semantics=("parallel",),
                                             vmem_limit_bytes=VMEM_LIMIT_BYTES),
        name="route",
        interpret=interpret,
    )(*ins)


GELU_K = 2.0 * (2.0 / 3.141592653589793) ** 0.5
BF16_ROWS = 16


def _gelu_tanh(x):
    z = x * ((-GELU_K * LOG2E) + (-GELU_K * 0.044715 * LOG2E) * (x * x))
    return x / (1.0 + jnp.exp2(z))


def _peer_kernel(hb_ref, h_ref, u_ref, vt_ref, cnt_ref, at_ref, rk_ref, bm_ref, g2_ref, b2_ref,
                 out_ref, acc_ref, pre_ref, p_ref, *, ic, tw, alpha):
    c = pl.program_id(1)

    @pl.when(c == 0)
    def _():
        acc_ref[...] = jnp.zeros(acc_ref.shape, F32)

    nk = PEER_N_KEYS
    tb = hb_ref.shape[0]
    grp = nk // BF16_ROWS
    pre_ref[...] = _dot_nt(u_ref[...], hb_ref[...])
    for ii in range(ic):
        rows = slice(ii * nk, (ii + 1) * nk)
        for s in range(tb // tw):
            ls = slice(s * tw, (s + 1) * tw)
            gsum = jnp.zeros((grp, BF16_ROWS, tw), BF16)
            for hd in range(PEER_HEADS):
                cnt = jnp.broadcast_to(cnt_ref[hd, ii:ii + 1, ls], (BF16_ROWS, tw)).astype(BF16)
                wgt = jnp.broadcast_to(at_ref[hd, ii:ii + 1, ls], (BF16_ROWS, tw)).astype(BF16)
                rk = rk_ref[hd, :, ls].reshape(grp, BF16_ROWS, tw)
                bm = bm_ref[hd, :, ls].reshape(grp, BF16_ROWS, tw)
                gsum = gsum + jnp.where(rk < cnt[None], bm, jnp.zeros_like(bm)) * wgt[None]
            act = _gelu_tanh(pre_ref[rows, ls]).astype(BF16)
            p_ref[rows, ls] = gsum.reshape(nk, tw) * act
    acc_ref[...] += _dot(vt_ref[...], p_ref[...])

    @pl.when(c == pl.num_programs(1) - 1)
    def _():
        y = alpha * h_ref[...] + acc_ref[...].T
        out_ref[...] = _layer_norm(y, g2_ref[...], b2_ref[...])


def _peer(hb, h, u, v, cnt, at, rk, bm, g2, b2, *, tb, ic, tw, alpha, interpret=False):
    T, D = h.shape
    n_exp = u.shape[0]
    ub = u.astype(BF16)
    vtb = v.astype(BF16).T
    ec = ic * PEER_N_KEYS
    rows = pl.BlockSpec((PEER_HEADS, ic, tb), lambda t, c: (0, c, t))
    whole = pl.BlockSpec((PEER_HEADS, PEER_N_KEYS, tb), lambda t, c: (0, 0, t))
    vec = pl.BlockSpec((1, D), lambda t, c: (0, 0))
    return pl.pallas_call(
        functools.partial(_peer_kernel, ic=ic, tw=tw, alpha=alpha),
        out_shape=jax.ShapeDtypeStruct((T, D), F32),
        grid=(T // tb, n_exp // ec),
        in_specs=[pl.BlockSpec((tb, D), lambda t, c: (t, 0)), pl.BlockSpec((tb, D), lambda t, c: (t, 0)),
                  pl.BlockSpec((ec, D), lambda t, c: (c, 0)), pl.BlockSpec((D, ec), lambda t, c: (0, c)),
                  rows, rows, whole, whole, vec, vec],
        out_specs=pl.BlockSpec((tb, D), lambda t, c: (t, 0)),
        scratch_shapes=[pltpu.VMEM((D, tb), F32), pltpu.VMEM((ec, tb), F32), pltpu.VMEM((ec, tb), BF16)],
        compiler_params=pltpu.CompilerParams(dimension_semantics=("parallel", "arbitrary"),
                                             vmem_limit_bytes=VMEM_LIMIT_BYTES),
        name="peer",
        interpret=interpret,
    )(hb, h, ub, vtb, cnt, at, rk, bm, g2.reshape(1, D).astype(F32), b2.reshape(1, D).astype(F32))


def _tile(n, pref):
    t = min(pref, n)
    assert n % t == 0, (n, t)
    return t


def kernel(x, w_in, kv_norm_g, w_uk, w_uv, hg_lb_logits, hg_norm_g, w_out, ln1_g, ln1_b,
           peer_w_q, peer_sub_keys, peer_u, peer_v, ln2_g, ln2_b):
    B, L, D = x.shape
    depth = w_in.shape[0]
    alpha = (2.0 * depth) ** 0.25
    T = B * L
    for layer in range(depth):
        (c, ct, kidx, qidxt, widxt, qlatt, hq, hf, hi, hg) = _proj(
            x, w_in[layer], kv_norm_g[layer], w_uk[layer], tb=_tile(L, 512))
        tq = _tile(L, 256)
        att = _dsa(kidx, c, ct, qidxt, widxt, qlatt, w_uv[layer], tq=tq, ks=min(128, tq), kc=tq)
        hgo = _hgrn(hq, hf, hi, hg, hg_lb_logits, hg_norm_g[layer], ts=_tile(L, 512), layer=layer)
        h, hb, rcnt, rat, rrk, rbm = _route(
            att.reshape(T, ATT_Q), hgo.reshape(T, HG_W), x.reshape(T, D), w_out[layer], ln1_g[layer], ln1_b[layer],
            peer_w_q[layer], peer_sub_keys[layer], tb=_tile(T, 512), alpha=alpha)
        y = _peer(hb, h, peer_u[layer], peer_v[layer], rcnt, rat, rrk, rbm, ln2_g[layer], ln2_b[layer],
                  tb=_tile(T, 512), ic=8, tw=256, alpha=alpha)
        x = y.reshape(B, L, D)
    return x
```

```python
import functools

import jax
import jax.numpy as jnp
from jax import lax
from jax.experimental import pallas as pl
from jax.experimental.pallas import tpu as pltpu

F32 = jnp.float32
BF16 = jnp.bfloat16
I32 = jnp.int32
I16 = jnp.int16

ATT_HEADS = 8
ATT_HEAD_DIM = 64
KV_LATENT = 128
IDX_HEADS = 8
IDX_DIM = 64
TOPK_MAX = 256
HG_HEADS = 8
HG_DIM = 64
HG_CHUNK = 32
PEER_HEADS = 8
PEER_N_KEYS = 128
PEER_TOPK = 16
EPS = 1e-5

ATT_Q = ATT_HEADS * ATT_HEAD_DIM
IDX_Q = IDX_HEADS * IDX_DIM
HG_W = HG_HEADS * HG_DIM

LANES = 128
VMEM_LIMIT_BYTES = 56 * 1024 * 1024

LOG2E = 1.4426950408889634
NEG_BIG = -1e30
INT_MIN = -(2 ** 31)
HALF16 = 1 << 15
BF16_ROWS = 16
KEY_NEG_INF = INT_MIN + 0x7FFFFF


def _dot(a, b):
    return jnp.dot(a, b, preferred_element_type=F32)


def _dot_nt(a, b):
    return lax.dot_general(a, b, (((1,), (1,)), ((), ())), preferred_element_type=F32)


def _dot_exact(a, b):
    return jnp.dot(a, b, preferred_element_type=F32, precision=lax.Precision.HIGHEST)


def _proj_kernel(x_ref, wc_ref, wk_ref, whq_ref, whf_ref, whi_ref, whg_ref,
                 wtq_ref, wtqi_ref, wtwi_ref, wtc_ref, grow_ref, gcol_ref, wuk_ref,
                 c_ref, ct_ref, kidx_ref, qidxt_ref, widxt_ref, qlatt_ref,
                 hq_ref, hf_ref, hi_ref, hg_ref):
    xb = x_ref[0].astype(BF16)

    c = _dot(xb, wc_ref[...])
    c = c * lax.rsqrt(jnp.mean(c * c, axis=-1, keepdims=True) + EPS) * grow_ref[...]
    c_ref[0] = c.astype(BF16)
    ct = _dot_nt(wtc_ref[...], xb)
    ct = ct * lax.rsqrt(jnp.mean(ct * ct, axis=0, keepdims=True) + EPS) * gcol_ref[...]
    ct_ref[0] = ct.astype(BF16)

    kidx_ref[0] = _dot(xb, wk_ref[...]).astype(BF16)
    qidxt_ref[0] = _dot_nt(wtqi_ref[...], xb).astype(BF16)
    widxt_ref[0] = _dot_nt(wtwi_ref[...], xb)

    qt = _dot_nt(wtq_ref[...], xb).astype(BF16)
    scale = ATT_HEAD_DIM ** -0.5 * LOG2E
    for h in range(ATT_HEADS):
        ql = _dot(wuk_ref[h], qt[h * ATT_HEAD_DIM:(h + 1) * ATT_HEAD_DIM, :]) * scale
        qlatt_ref[0, h * KV_LATENT:(h + 1) * KV_LATENT, :] = ql.astype(BF16)

    hq_ref[0] = _dot(xb, whq_ref[...])
    hf_ref[0] = _dot(xb, whf_ref[...])
    hi_ref[0] = _dot(xb, whi_ref[...])
    hg_ref[0] = _dot(xb, whg_ref[...])


def _proj(x, w, kv_g, w_uk, *, tb, interpret=False):
    B, L, D = x.shape
    o = 0
    cols = {}
    for name, width in (("q", ATT_Q), ("c", KV_LATENT), ("qi", IDX_Q), ("k", IDX_DIM), ("wi", IDX_HEADS),
                        ("hq", HG_W), ("hf", HG_W), ("hi", HG_W), ("hg", HG_W)):
        cols[name] = w[:, o:o + width].astype(BF16)
        o += width
    assert o == w.shape[1]
    wt = lambda n: cols[n].T
    ins = [x, cols["c"], cols["k"], cols["hq"], cols["hf"], cols["hi"], cols["hg"],
           wt("q"), wt("qi"), wt("wi"), wt("c"),
           kv_g.reshape(1, KV_LATENT).astype(F32), kv_g.reshape(KV_LATENT, 1).astype(F32),
           w_uk.astype(BF16)]

    def full(a):
        zeros = (0,) * a.ndim
        return pl.BlockSpec(a.shape, lambda b, t: zeros)

    in_specs = [pl.BlockSpec((1, tb, D), lambda b, t: (b, t, 0))] + [full(a) for a in ins[1:]]
    nat = lambda wd, dt: (jax.ShapeDtypeStruct((B, L, wd), dt), pl.BlockSpec((1, tb, wd), lambda b, t: (b, t, 0)))
    tr = lambda wd, dt: (jax.ShapeDtypeStruct((B, wd, L), dt), pl.BlockSpec((1, wd, tb), lambda b, t: (b, 0, t)))
    outs = [nat(KV_LATENT, BF16), tr(KV_LATENT, BF16), nat(IDX_DIM, BF16), tr(IDX_Q, BF16), tr(IDX_HEADS, F32),
            tr(ATT_HEADS * KV_LATENT, BF16), nat(HG_W, F32), nat(HG_W, F32), nat(HG_W, F32), nat(HG_W, F32)]
    return pl.pallas_call(
        _proj_kernel,
        out_shape=[s for s, _ in outs],
        grid=(B, L // tb),
        in_specs=in_specs,
        out_specs=[s for _, s in outs],
        compiler_params=pltpu.CompilerParams(dimension_semantics=("parallel", "parallel"),
                                             vmem_limit_bytes=VMEM_LIMIT_BYTES),
        name="proj",
        interpret=interpret,
    )(*ins)


def _dsa_kernel(kidx_ref, c_ref, ct_ref, qidxt_ref, widxt_ref, qlatt_ref, wuvt_ref,
                out_ref, keys_ref, khi_ref, klo_ref, acc_ref, m_ref, l_ref, alpha_ref, jb_ref, ot_ref, a_ref, p_ref,
                *, tq, ks, kc, top_k, seq):
    qi = pl.program_id(1)
    n_keys = (qi + 1) * tq
    nks = (qi + 1) * (tq // ks)
    nkc = (qi + 1) * (tq // kc)
    q_pos = lambda rows: qi * tq + lax.broadcasted_iota(I32, (rows, tq), 1)
    k_off = lambda rows: lax.broadcasted_iota(I32, (rows, tq), 0)

    def score_chunk(k, carry):
        r0 = pl.multiple_of(k * ks, ks)
        kk = kidx_ref[0, pl.ds(r0, ks), :]
        s = jnp.zeros((ks, tq), F32)
        for h in range(IDX_HEADS):
            z = _dot(kk, qidxt_ref[0, h * IDX_DIM:(h + 1) * IDX_DIM, :])
            s = s + widxt_ref[0, h:h + 1, :] * jnp.maximum(z, 0.0)
        s = jnp.where(r0 + k_off(ks) <= q_pos(ks), s, -jnp.inf)
        bits = pltpu.bitcast(s, I32)
        key = bits ^ ((bits >> 31) & 0x7FFFFFFF)
        keys_ref[pl.ds(r0, ks), :] = key
        khi_ref[pl.ds(r0, ks), :] = (key >> 16).astype(I16)
        klo_ref[pl.ds(r0, ks), :] = ((key & 0xFFFF) - HALF16).astype(I16)
        return carry

    lax.fori_loop(0, nks, score_chunk, 0)

    def count(pred):
        def body(k, cnt):
            r0 = pl.multiple_of(k * kc, kc)
            blk = keys_ref[pl.ds(r0, kc), :]
            return cnt + jnp.sum(jnp.where(pred(blk, r0), 1, 0).astype(I32), axis=0, keepdims=True)
        return lax.fori_loop(0, nkc, body, jnp.zeros((1, tq), I32))

    def count16(ref, cand):
        c16 = cand.astype(I16)

        def body(k, cnt):
            r0 = pl.multiple_of(k * kc, kc)
            blk = ref[pl.ds(r0, kc), :]
            ones = jnp.where(blk >= c16, jnp.ones(blk.shape, BF16), jnp.zeros(blk.shape, BF16))
            parts = [ones[g * BF16_ROWS:(g + 1) * BF16_ROWS] for g in range(kc // BF16_ROWS)]
            while len(parts) > 1:
                parts = [parts[i] + parts[i + 1] for i in range(0, len(parts), 2)]
            return cnt + parts[0].astype(F32)
        cnt = lax.fori_loop(0, nkc, body, jnp.zeros((BF16_ROWS, tq), F32))
        return jnp.sum(cnt, axis=0, keepdims=True).astype(I32)

    def hi_step(i, carry):
        t, n_t = carry
        cand = t + lax.shift_left(jnp.int32(1), 15 - i)
        cnt = count16(khi_ref, cand)
        take = cnt >= top_k
        return jnp.where(take, cand, t), jnp.where(take, cnt, n_t)

    h, n_ge_h = lax.fori_loop(0, 16, hi_step,
                              (jnp.full((1, tq), -HALF16, I32), jnp.full((1, tq), n_keys, I32)))
    n_gt_h = jnp.where(h == HALF16 - 1, 0, count16(khi_ref, jnp.minimum(h + 1, HALF16 - 1)))
    need_lo = top_k - n_gt_h
    h16 = h.astype(I16)

    def mark_boundary(k, carry):
        rows = pl.ds(pl.multiple_of(k * kc, kc), kc)
        klo_ref[rows, :] = jnp.where(khi_ref[rows, :] == h16, klo_ref[rows, :], jnp.int16(-HALF16))
        return carry

    lax.fori_loop(0, nkc, mark_boundary, 0)

    def lo_cond(carry):
        i, _, _, open_lanes = carry
        return jnp.logical_and(i < 16, open_lanes > 0)

    def lo_step(carry):
        i, t, n_t, _ = carry
        cand = t + lax.shift_left(jnp.int32(1), 15 - i)
        cnt = count16(klo_ref, cand)
        take = cnt >= need_lo
        t = jnp.where(take, cand, t)
        n_t = jnp.where(take, cnt, n_t)
        return i + 1, t, n_t, jnp.max(jnp.abs(n_t - need_lo))

    n_boundary = n_ge_h - n_gt_h
    _, t_lo, n_lo, _ = lax.while_loop(
        lo_cond, lo_step,
        (jnp.int32(0), jnp.full((1, tq), -HALF16, I32), n_boundary, jnp.max(jnp.abs(n_boundary - need_lo))))
    thr = lax.shift_left(h, 16) + (t_lo + HALF16)
    n_ge = n_gt_h + n_lo

    has_tie = jnp.where(n_ge > top_k, jnp.where(thr > KEY_NEG_INF, 1, 0), 0)
    any_tie = jnp.max(has_tie)
    jb_ref[...] = jnp.full((1, tq), -1, I32)

    @pl.when(any_tie > 0)
    def _():
        need = top_k - count(lambda blk, r0: blk > thr)

        def idx_step(i, v):
            cand = v + lax.shift_left(jnp.int32(1), (seq.bit_length() - 1) - i)
            cnt = count(lambda blk, r0: (blk == thr) & (r0 + k_off(kc) < cand))
            return jnp.where(cnt < need, cand, v)
        v = lax.fori_loop(0, seq.bit_length(), idx_step, jnp.zeros((1, tq), I32))
        jb_ref[...] = jnp.where(has_tie > 0, v, -1)

    jb = jb_ref[...]
    gt_thr = jnp.where(has_tie > 0, thr, jnp.maximum(thr - 1, KEY_NEG_INF))

    m_ref[...] = jnp.full(m_ref.shape, NEG_BIG, F32)
    l_ref[...] = jnp.zeros(l_ref.shape, F32)
    acc_ref[...] = jnp.zeros(acc_ref.shape, F32)

    def att_chunk(k, carry):
        r0 = pl.multiple_of(k * kc, kc)
        key = keys_ref[pl.ds(r0, kc), :]
        bias = lax.cond(
            any_tie > 0,
            lambda: jnp.where(key > gt_thr, 0.0,
                              jnp.where(key == thr, jnp.where(r0 + k_off(kc) <= jb, 0.0, NEG_BIG), NEG_BIG)),
            lambda: jnp.where(key > gt_thr, 0.0, NEG_BIG))
        cc = c_ref[0, pl.ds(r0, kc), :]
        cct = ct_ref[0, :, pl.ds(r0, kc)]
        for h in range(ATT_HEADS):
            a = _dot(cc, qlatt_ref[0, h * KV_LATENT:(h + 1) * KV_LATENT, :]) + bias
            a_ref[h] = a
            m_old = m_ref[h]
            m_new = jnp.maximum(m_old, jnp.max(a, axis=0, keepdims=True))
            alpha_ref[h] = jnp.exp2(m_old - m_new)
            m_ref[h] = m_new
        for h in range(ATT_HEADS):
            p = jnp.exp2(a_ref[h] - m_ref[h])
            l_ref[h] = alpha_ref[h] * l_ref[h] + jnp.sum(p, axis=0, keepdims=True)
            p_ref[h] = p.astype(BF16)
        for h in range(ATT_HEADS):
            acc_ref[h] = alpha_ref[h] * acc_ref[h] + _dot(cct, p_ref[h])
        return carry

    lax.fori_loop(0, nkc, att_chunk, 0)

    for h in range(ATT_HEADS):
        o_lat = (acc_ref[h] / l_ref[h]).astype(BF16)
        ot_ref[h * ATT_HEAD_DIM:(h + 1) * ATT_HEAD_DIM, :] = _dot(wuvt_ref[h], o_lat)
    out_ref[0] = ot_ref[...].T.astype(BF16)


def _dsa(kidx, c, ct, qidxt, widxt, qlatt, w_uv, *, tq, ks, kc, interpret=False):
    B, L, _ = c.shape
    top_k = min(TOPK_MAX, L // 4)
    assert tq % ks == 0 and tq % kc == 0 and top_k <= tq
    wuvt = jnp.swapaxes(w_uv, 1, 2).astype(BF16)
    kernel = functools.partial(_dsa_kernel, tq=tq, ks=ks, kc=kc, top_k=top_k, seq=L)
    return pl.pallas_call(
        kernel,
        out_shape=jax.ShapeDtypeStruct((B, L, ATT_Q), BF16),
        grid=(B, L // tq),
        in_specs=[
            pl.BlockSpec((1, L, IDX_DIM), lambda b, q: (b, 0, 0)),
            pl.BlockSpec((1, L, KV_LATENT), lambda b, q: (b, 0, 0)),
            pl.BlockSpec((1, KV_LATENT, L), lambda b, q: (b, 0, 0)),
            pl.BlockSpec((1, IDX_Q, tq), lambda b, q: (b, 0, q)),
            pl.BlockSpec((1, IDX_HEADS, tq), lambda b, q: (b, 0, q)),
            pl.BlockSpec((1, ATT_HEADS * KV_LATENT, tq), lambda b, q: (b, 0, q)),
            pl.BlockSpec((ATT_HEADS, ATT_HEAD_DIM, KV_LATENT), lambda b, q: (0, 0, 0)),
        ],
        out_specs=pl.BlockSpec((1, tq, ATT_Q), lambda b, q: (b, q, 0)),
        scratch_shapes=[
            pltpu.VMEM((L, tq), I32),
            pltpu.VMEM((L, tq), I16),
            pltpu.VMEM((L, tq), I16),
            pltpu.VMEM((ATT_HEADS, KV_LATENT, tq), F32),
            pltpu.VMEM((ATT_HEADS, 1, tq), F32),
            pltpu.VMEM((ATT_HEADS, 1, tq), F32),
            pltpu.VMEM((ATT_HEADS, 1, tq), F32),
            pltpu.VMEM((1, tq), I32),
            pltpu.VMEM((ATT_Q, tq), F32),
            pltpu.VMEM((ATT_HEADS, kc, tq), F32),
            pltpu.VMEM((ATT_HEADS, kc, tq), BF16),
        ],
        compiler_params=pltpu.CompilerParams(dimension_semantics=("parallel", "parallel"),
                                             vmem_limit_bytes=VMEM_LIMIT_BYTES),
        name="dsa",
        interpret=interpret,
    )(kidx, c, ct, qidxt, widxt, qlatt, wuvt)


HG_GROUP = 128
HG_PAIRS = HG_HEADS // 2


def _hgrn_kernel(lbl_ref, gn_ref, hq_ref, hf_ref, hi_ref, hg_ref, out_ref, st_ref, *, ts, layer):
    @pl.when(pl.program_id(1) == 0)
    def _():
        st_ref[...] = jnp.zeros(st_ref.shape, F32)

    logits = lbl_ref[...]
    e = jnp.exp(logits - jnp.max(logits, axis=0, keepdims=True))
    lb = jnp.sum(e[0:layer + 1, :], axis=0, keepdims=True) / jnp.sum(e, axis=0, keepdims=True)

    g = HG_GROUP
    chunk_shift = HG_CHUNK.bit_length() - 1
    head_shift = HG_DIM.bit_length() - 1
    r = lax.broadcasted_iota(I32, (g, g), 0)
    cidx = lax.broadcasted_iota(I32, (g, g), 1)
    same_chunk = (r >> chunk_shift) == (cidx >> chunk_shift)
    causal = same_chunk & (cidx <= r)
    tri = jnp.where(causal, 1.0, 0.0).astype(F32)
    ones_blk = jnp.where(same_chunk, 1.0, 0.0).astype(F32)
    same_head = (r >> head_shift) == (cidx >> head_shift)
    head_blk = jnp.where(same_head, 1.0, 0.0).astype(F32)
    lane_lo = lax.broadcasted_iota(I32, (g, LANES), 1) < HG_DIM
    col_chunk = cidx >> chunk_shift
    gn = gn_ref[...]

    def group(gi, carry):
        r0 = pl.multiple_of(gi * g, g)
        rows = pl.ds(r0, g)
        f = lb + (1.0 - lb) * jax.nn.sigmoid(hf_ref[0, rows, :])
        logf = jnp.log(f)
        kgate = 1.0 - f
        b = _dot_exact(tri, logf)
        btot = _dot_exact(ones_blk, logf)
        q_t = jax.nn.silu(hq_ref[0, rows, :]) * jnp.exp(b)
        k_t = kgate * jnp.exp(-b)
        k_dec = kgate * jnp.exp(btot - b)
        decay = jnp.exp(btot)
        v = hi_ref[0, rows, :]
        gate = hg_ref[0, rows, :]
        for p in range(HG_PAIRS):
            ls = slice(p * LANES, (p + 1) * LANES)
            qp = q_t[:, ls]
            kp = k_t[:, ls].astype(BF16)
            vp = v[:, ls].astype(BF16)
            kdp = k_dec[:, ls].astype(BF16)
            q_lo = jnp.where(lane_lo, qp, 0.0).astype(BF16)
            q_hi = jnp.where(lane_lo, 0.0, qp).astype(BF16)
            a_lo = jnp.where(causal, _dot_nt(q_lo, kp), 0.0).astype(BF16)
            a_hi = jnp.where(causal, _dot_nt(q_hi, kp), 0.0).astype(BF16)
            o = jnp.where(lane_lo, _dot(a_lo, vp), _dot(a_hi, vp))
            st = st_ref[p]
            qpb = qp.astype(BF16)
            vt = v[:, ls].T
            inter = []
            for ch in range(g // HG_CHUNK):
                rs = slice(ch * HG_CHUNK, (ch + 1) * HG_CHUNK)
                inter.append(_dot_nt(qpb[rs], st.astype(BF16)))
                vt_ch = jnp.where(col_chunk == ch, vt, 0.0).astype(BF16)
                ds = _dot(vt_ch, kdp)
                st = st * decay[ch * HG_CHUNK:ch * HG_CHUNK + 1, ls] + jnp.where(same_head, ds, 0.0)
            st_ref[p] = st
            o = o + jnp.concatenate(inter, axis=0)
            ms = _dot_exact(o * o, head_blk) * (1.0 / HG_DIM)
            gt = gate[:, ls]
            on = o * lax.rsqrt(ms + EPS) * gn * (gt * jax.nn.sigmoid(gt))
            out_ref[0, rows, ls] = on.astype(BF16)
        return carry

    lax.fori_loop(0, ts // g, group, 0)


def _hgrn(hq, hf, hi, hg, lb_logits, g_norm, *, ts, layer, interpret=False):
    B, L, W = hq.shape
    gn = jnp.tile(g_norm.reshape(1, HG_DIM).astype(F32), (1, 2))
    blk = pl.BlockSpec((1, ts, W), lambda b, t: (b, t, 0))
    return pl.pallas_call(
        functools.partial(_hgrn_kernel, ts=ts, layer=layer),
        out_shape=jax.ShapeDtypeStruct((B, L, W), BF16),
        grid=(B, L // ts),
        in_specs=[pl.BlockSpec(lb_logits.shape, lambda b, t: (0, 0)),
                  pl.BlockSpec((1, LANES), lambda b, t: (0, 0)), blk, blk, blk, blk],
        out_specs=blk,
        scratch_shapes=[pltpu.VMEM((HG_PAIRS, LANES, LANES), F32)],
        compiler_params=pltpu.CompilerParams(dimension_semantics=("parallel", "arbitrary"),
                                             vmem_limit_bytes=VMEM_LIMIT_BYTES),
        name="hgrn2",
        interpret=interpret,
    )(lb_logits.astype(F32), gn, hq, hf, hi, hg)


def _layer_norm(y, g, b):
    mu = jnp.mean(y, axis=-1, keepdims=True)
    d = y - mu
    var = jnp.mean(d * d, axis=-1, keepdims=True)
    return d * lax.rsqrt(var + EPS) * g + b


def _top_desc(s, n, with_rank=False):
    vals = []
    cur = s
    rank = jnp.full(s.shape, float(n), F32) if with_rank else None
    for i in range(n):
        mx = jnp.max(cur, axis=0, keepdims=True)
        vals.append(mx)
        eq = cur == mx
        if with_rank:
            rank = jnp.where(eq, float(i), rank)
        if i + 1 < n:
            cur = jnp.where(eq, -jnp.inf, cur)
    return (vals, rank) if with_rank else vals


PEER_KEEP = PEER_TOPK + 1
_PAIR_RANKS = [(i, j) for i in range(1, PEER_KEEP + 1) for j in range(1, PEER_KEEP + 1) if i * j <= PEER_KEEP]


def _route_kernel(att_ref, hg_ref, x_ref, woa_ref, wob_ref, g1_ref, b1_ref, wq_ref, sk_ref,
                  h_ref, hb_ref, cnt_ref, at_ref, rk_ref, bm_ref, *, alpha):
    mix = _dot(att_ref[...], woa_ref[...]) + _dot(hg_ref[...], wob_ref[...])
    h = _layer_norm(alpha * x_ref[...] + mix, g1_ref[...], b1_ref[...])
    h_ref[...] = h
    hb = h.astype(BF16)
    hb_ref[...] = hb
    qp = _dot(hb, wq_ref[...]).astype(BF16)
    dk = PEER_N_KEYS
    for hd in range(PEER_HEADS):
        s1 = _dot_nt(sk_ref[hd, 0], qp[:, (2 * hd) * dk:(2 * hd + 1) * dk])
        s2 = _dot_nt(sk_ref[hd, 1], qp[:, (2 * hd + 1) * dk:(2 * hd + 2) * dk])
        a = _top_desc(s1, PEER_KEEP)
        bb, rank2 = _top_desc(s2, PEER_KEEP, with_rank=True)
        cands = [a[i - 1] + bb[j - 1] for i, j in _PAIR_RANKS]
        pad = (-len(cands)) % 8
        cand = jnp.concatenate(cands + [jnp.full_like(cands[0], -jnp.inf)] * pad, axis=0)
        top = _top_desc(cand, PEER_KEEP)
        c_max, c_k, c_next = top[0], top[PEER_TOPK - 1], top[PEER_TOPK]
        thr = 0.5 * (c_k + c_next)
        z = jnp.sum(jnp.where(cand >= c_k, jnp.exp(cand - c_max), 0.0), axis=0, keepdims=True)
        cnt = jnp.zeros(s1.shape, F32)
        for r in range(PEER_KEEP):
            cnt = cnt + jnp.where(s1 >= thr - bb[r], 1.0, 0.0)
        cnt_ref[hd] = cnt
        at_ref[hd] = jnp.exp(s1 - a[0]) / z
        rk_ref[hd] = rank2.astype(BF16)
        bm_ref[hd] = jnp.exp(s2 - bb[0]).astype(BF16)


def _route(att, hgo, x2, w_out, g1, b1, w_q, sub_keys, *, tb, alpha, interpret=False):
    T, D = x2.shape
    woa = w_out[:ATT_Q].astype(BF16)
    wob = w_out[ATT_Q:].astype(BF16)

    def full(a):
        zeros = (0,) * a.ndim
        return pl.BlockSpec(a.shape, lambda t: zeros)

    ins = [att, hgo, x2, woa, wob, g1.reshape(1, D).astype(F32), b1.reshape(1, D).astype(F32),
           w_q.astype(BF16), sub_keys.astype(BF16)]
    in_specs = [pl.BlockSpec((tb, ATT_Q), lambda t: (t, 0)), pl.BlockSpec((tb, HG_W), lambda t: (t, 0)),
                pl.BlockSpec((tb, D), lambda t: (t, 0))] + [full(a) for a in ins[3:]]
    rt_shape = lambda dt: jax.ShapeDtypeStruct((PEER_HEADS, PEER_N_KEYS, T), dt)
    rt_spec = pl.BlockSpec((PEER_HEADS, PEER_N_KEYS, tb), lambda t: (0, 0, t))
    return pl.pallas_call(
        functools.partial(_route_kernel, alpha=alpha),
        out_shape=[jax.ShapeDtypeStruct((T, D), F32), jax.ShapeDtypeStruct((T, D), BF16),
                   rt_shape(F32), rt_shape(F32), rt_shape(BF16), rt_shape(BF16)],
        grid=(T // tb,),
        in_specs=in_specs,
        out_specs=[pl.BlockSpec((tb, D), lambda t: (t, 0)), pl.BlockSpec((tb, D), lambda t: (t, 0)),
                   rt_spec, rt_spec, rt_spec, rt_spec],
        compiler_params=pltpu.CompilerParams(dimension_semantics=("parallel",),
                                             vmem_limit_bytes=VMEM_LIMIT_BYTES),
        name="route",
        interpret=interpret,
    )(*ins)


GELU_K = 2.0 * (2.0 / 3.141592653589793) ** 0.5


def _gelu_tanh(x):
    z = x * ((-GELU_K * LOG2E) + (-GELU_K * 0.044715 * LOG2E) * (x * x))
    return x / (1.0 + jnp.exp2(z))


def _peer_kernel(hb_ref, h_ref, u_ref, vt_ref, cnt_ref, at_ref, rk_ref, bm_ref, g2_ref, b2_ref,
                 out_ref, acc_ref, pre_ref, p_ref, *, ic, tw, alpha):
    c = pl.program_id(1)

    @pl.when(c == 0)
    def _():
        acc_ref[...] = jnp.zeros(acc_ref.shape, F32)

    nk = PEER_N_KEYS
    tb = hb_ref.shape[0]
    grp = nk // BF16_ROWS
    pre_ref[...] = _dot_nt(u_ref[...], hb_ref[...])
    for ii in range(ic):
        rows = slice(ii * nk, (ii + 1) * nk)
        for s in range(tb // tw):
            ls = slice(s * tw, (s + 1) * tw)
            gsum = jnp.zeros((grp, BF16_ROWS, tw), BF16)
            for hd in range(PEER_HEADS):
                cnt = jnp.broadcast_to(cnt_ref[hd, ii:ii + 1, ls], (BF16_ROWS, tw)).astype(BF16)
                wgt = jnp.broadcast_to(at_ref[hd, ii:ii + 1, ls], (BF16_ROWS, tw)).astype(BF16)
                rk = rk_ref[hd, :, ls].reshape(grp, BF16_ROWS, tw)
                bm = bm_ref[hd, :, ls].reshape(grp, BF16_ROWS, tw)
                gsum = gsum + jnp.where(rk < cnt[None], bm, jnp.zeros_like(bm)) * wgt[None]
            act = _gelu_tanh(pre_ref[rows, ls]).astype(BF16)
            p_ref[rows, ls] = gsum.reshape(nk, tw) * act
    acc_ref[...] += _dot(vt_ref[...], p_ref[...])

    @pl.when(c == pl.num_programs(1) - 1)
    def _():
        y = alpha * h_ref[...] + acc_ref[...].T
        out_ref[...] = _layer_norm(y, g2_ref[...], b2_ref[...])


def _peer(hb, h, u, v, cnt, at, rk, bm, g2, b2, *, tb, ic, tw, alpha, interpret=False):
    T, D = h.shape
    n_exp = u.shape[0]
    ub = u.astype(BF16)
    vtb = v.astype(BF16).T
    ec = ic * PEER_N_KEYS
    rows = pl.BlockSpec((PEER_HEADS, ic, tb), lambda t, c: (0, c, t))
    whole = pl.BlockSpec((PEER_HEADS, PEER_N_KEYS, tb), lambda t, c: (0, 0, t))
    vec = pl.BlockSpec((1, D), lambda t, c: (0, 0))
    return pl.pallas_call(
        functools.partial(_peer_kernel, ic=ic, tw=tw, alpha=alpha),
        out_shape=jax.ShapeDtypeStruct((T, D), F32),
        grid=(T // tb, n_exp // ec),
        in_specs=[pl.BlockSpec((tb, D), lambda t, c: (t, 0)), pl.BlockSpec((tb, D), lambda t, c: (t, 0)),
                  pl.BlockSpec((ec, D), lambda t, c: (c, 0)), pl.BlockSpec((D, ec), lambda t, c: (0, c)),
                  rows, rows, whole, whole, vec, vec],
        out_specs=pl.BlockSpec((tb, D), lambda t, c: (t, 0)),
        scratch_shapes=[pltpu.VMEM((D, tb), F32), pltpu.VMEM((ec, tb), F32), pltpu.VMEM((ec, tb), BF16)],
        compiler_params=pltpu.CompilerParams(dimension_semantics=("parallel", "arbitrary"),
                                             vmem_limit_bytes=VMEM_LIMIT_BYTES),
        name="peer",
        interpret=interpret,
    )(hb, h, ub, vtb, cnt, at, rk, bm, g2.reshape(1, D).astype(F32), b2.reshape(1, D).astype(F32))


def _tile(n, pref):
    t = min(pref, n)
    assert n % t == 0, (n, t)
    return t


def kernel(x, w_in, kv_norm_g, w_uk, w_uv, hg_lb_logits, hg_norm_g, w_out, ln1_g, ln1_b,
           peer_w_q, peer_sub_keys, peer_u, peer_v, ln2_g, ln2_b):
    B, L, D = x.shape
    depth = w_in.shape[0]
    alpha = (2.0 * depth) ** 0.25
    T = B * L
    for layer in range(depth):
        (c, ct, kidx, qidxt, widxt, qlatt, hq, hf, hi, hg) = _proj(
            x, w_in[layer], kv_norm_g[layer], w_uk[layer], tb=_tile(L, 512))
        tq = _tile(L, 256)
        att = _dsa(kidx, c, ct, qidxt, widxt, qlatt, w_uv[layer], tq=tq, ks=min(128, tq), kc=tq)
        hgo = _hgrn(hq, hf, hi, hg, hg_lb_logits, hg_norm_g[layer], ts=_tile(L, 512), layer=layer)
        h, hb, rcnt, rat, rrk, rbm = _route(
            att.reshape(T, ATT_Q), hgo.reshape(T, HG_W), x.reshape(T, D), w_out[layer], ln1_g[layer], ln1_b[layer],
            peer_w_q[layer], peer_sub_keys[layer], tb=_tile(T, 512), alpha=alpha)
        y = _peer(hb, h, peer_u[layer], peer_v[layer], rcnt, rat, rrk, rbm, ln2_g[layer], ln2_b[layer],
                  tb=_tile(T, 512), ic=8, tw=256, alpha=alpha)
        x = y.reshape(B, L, D)
    return x
```

```python
import functools

import jax
import jax.numpy as jnp
from jax import lax
from jax.experimental import pallas as pl
from jax.experimental.pallas import tpu as pltpu

F32 = jnp.float32
BF16 = jnp.bfloat16
I32 = jnp.int32
I16 = jnp.int16

ATT_HEADS = 8
ATT_HEAD_DIM = 64
KV_LATENT = 128
IDX_HEADS = 8
IDX_DIM = 64
TOPK_MAX = 256
HG_HEADS = 8
HG_DIM = 64
HG_CHUNK = 32
PEER_HEADS = 8
PEER_N_KEYS = 128
PEER_TOPK = 16
EPS = 1e-5

ATT_Q = ATT_HEADS * ATT_HEAD_DIM
IDX_Q = IDX_HEADS * IDX_DIM
HG_W = HG_HEADS * HG_DIM

LANES = 128
VMEM_LIMIT_BYTES = 56 * 1024 * 1024

LOG2E = 1.4426950408889634
NEG_BIG = -1e30
INT_MIN = -(2 ** 31)
HALF16 = 1 << 15
BF16_ROWS = 16
KEY_NEG_INF = INT_MIN + 0x7FFFFF


def _dot(a, b):
    return jnp.dot(a, b, preferred_element_type=F32)


def _dot_nt(a, b):
    return lax.dot_general(a, b, (((1,), (1,)), ((), ())), preferred_element_type=F32)


def _dot_exact(a, b):
    return jnp.dot(a, b, preferred_element_type=F32, precision=lax.Precision.HIGHEST)


def _proj_kernel(x_ref, wc_ref, wk_ref, whq_ref, whf_ref, whi_ref, whg_ref,
                 wtq_ref, wtqi_ref, wtwi_ref, wtc_ref, grow_ref, gcol_ref, wuk_ref,
                 c_ref, ct_ref, kidx_ref, qidxt_ref, widxt_ref, qlatt_ref,
                 hq_ref, hf_ref, hi_ref, hg_ref):
    xb = x_ref[0].astype(BF16)

    c = _dot(xb, wc_ref[...])
    c = c * lax.rsqrt(jnp.mean(c * c, axis=-1, keepdims=True) + EPS) * grow_ref[...]
    c_ref[0] = c.astype(BF16)
    ct = _dot_nt(wtc_ref[...], xb)
    ct = ct * lax.rsqrt(jnp.mean(ct * ct, axis=0, keepdims=True) + EPS) * gcol_ref[...]
    ct_ref[0] = ct.astype(BF16)

    kidx_ref[0] = _dot(xb, wk_ref[...]).astype(BF16)
    qidxt_ref[0] = _dot_nt(wtqi_ref[...], xb).astype(BF16)
    widxt_ref[0] = _dot_nt(wtwi_ref[...], xb)

    qt = _dot_nt(wtq_ref[...], xb).astype(BF16)
    scale = ATT_HEAD_DIM ** -0.5 * LOG2E
    for h in range(ATT_HEADS):
        ql = _dot(wuk_ref[h], qt[h * ATT_HEAD_DIM:(h + 1) * ATT_HEAD_DIM, :]) * scale
        qlatt_ref[0, h * KV_LATENT:(h + 1) * KV_LATENT, :] = ql.astype(BF16)

    hq_ref[0] = _dot(xb, whq_ref[...])
    hf_ref[0] = _dot(xb, whf_ref[...])
    hi_ref[0] = _dot(xb, whi_ref[...])
    hg_ref[0] = _dot(xb, whg_ref[...])


def _proj(x, w, kv_g, w_uk, *, tb, interpret=False):
    B, L, D = x.shape
    o = 0
    cols = {}
    for name, width in (("q", ATT_Q), ("c", KV_LATENT), ("qi", IDX_Q), ("k", IDX_DIM), ("wi", IDX_HEADS),
                        ("hq", HG_W), ("hf", HG_W), ("hi", HG_W), ("hg", HG_W)):
        cols[name] = w[:, o:o + width].astype(BF16)
        o += width
    assert o == w.shape[1]
    wt = lambda n: cols[n].T
    ins = [x, cols["c"], cols["k"], cols["hq"], cols["hf"], cols["hi"], cols["hg"],
           wt("q"), wt("qi"), wt("wi"), wt("c"),
           kv_g.reshape(1, KV_LATENT).astype(F32), kv_g.reshape(KV_LATENT, 1).astype(F32),
           w_uk.astype(BF16)]

    def full(a):
        zeros = (0,) * a.ndim
        return pl.BlockSpec(a.shape, lambda b, t: zeros)

    in_specs = [pl.BlockSpec((1, tb, D), lambda b, t: (b, t, 0))] + [full(a) for a in ins[1:]]
    nat = lambda wd, dt: (jax.ShapeDtypeStruct((B, L, wd), dt), pl.BlockSpec((1, tb, wd), lambda b, t: (b, t, 0)))
    tr = lambda wd, dt: (jax.ShapeDtypeStruct((B, wd, L), dt), pl.BlockSpec((1, wd, tb), lambda b, t: (b, 0, t)))
    outs = [nat(KV_LATENT, BF16), tr(KV_LATENT, BF16), nat(IDX_DIM, BF16), tr(IDX_Q, BF16), tr(IDX_HEADS, F32),
            tr(ATT_HEADS * KV_LATENT, BF16), nat(HG_W, F32), nat(HG_W, F32), nat(HG_W, F32), nat(HG_W, F32)]
    return pl.pallas_call(
        _proj_kernel,
        out_shape=[s for s, _ in outs],
        grid=(B, L // tb),
        in_specs=in_specs,
        out_specs=[s for _, s in outs],
        compiler_params=pltpu.CompilerParams(dimension_semantics=("parallel", "parallel"),
                                             vmem_limit_bytes=VMEM_LIMIT_BYTES),
        name="proj",
        interpret=interpret,
    )(*ins)


def _dsa_kernel(kidx_ref, c_ref, ct_ref, qidxt_ref, widxt_ref, qlatt_ref, wuvt_ref,
                out_ref, keys_ref, khi_ref, klo_ref, acc_ref, m_ref, l_ref, alpha_ref, jb_ref, ot_ref, a_ref, p_ref,
                *, tq, ks, kc, top_k, seq):
    qi = pl.program_id(1)
    n_keys = (qi + 1) * tq
    nks = (qi + 1) * (tq // ks)
    nkc = (qi + 1) * (tq // kc)
    q_pos = lambda rows: qi * tq + lax.broadcasted_iota(I32, (rows, tq), 1)
    k_off = lambda rows: lax.broadcasted_iota(I32, (rows, tq), 0)

    def score_chunk(k, carry):
        r0 = pl.multiple_of(k * ks, ks)
        kk = kidx_ref[0, pl.ds(r0, ks), :]
        s = jnp.zeros((ks, tq), F32)
        for h in range(IDX_HEADS):
            z = _dot(kk, qidxt_ref[0, h * IDX_DIM:(h + 1) * IDX_DIM, :])
            s = s + widxt_ref[0, h:h + 1, :] * jnp.maximum(z, 0.0)
        s = jnp.where(r0 + k_off(ks) <= q_pos(ks), s, -jnp.inf)
        bits = pltpu.bitcast(s, I32)
        key = bits ^ ((bits >> 31) & 0x7FFFFFFF)
        keys_ref[pl.ds(r0, ks), :] = key
        khi_ref[pl.ds(r0, ks), :] = (key >> 16).astype(I16)
        klo_ref[pl.ds(r0, ks), :] = ((key & 0xFFFF) - HALF16).astype(I16)
        return carry

    lax.fori_loop(0, nks, score_chunk, 0)

    def count(pred):
        def body(k, cnt):
            r0 = pl.multiple_of(k * kc, kc)
            blk = keys_ref[pl.ds(r0, kc), :]
            return cnt + jnp.sum(jnp.where(pred(blk, r0), 1, 0).astype(I32), axis=0, keepdims=True)
        return lax.fori_loop(0, nkc, body, jnp.zeros((1, tq), I32))

    def count16(ref, cand):
        c16 = cand.astype(I16)

        def body(k, cnt):
            r0 = pl.multiple_of(k * kc, kc)
            blk = ref[pl.ds(r0, kc), :]
            ones = jnp.where(blk >= c16, jnp.ones(blk.shape, BF16), jnp.zeros(blk.shape, BF16))
            parts = [ones[g * BF16_ROWS:(g + 1) * BF16_ROWS] for g in range(kc // BF16_ROWS)]
            while len(parts) > 1:
                parts = [parts[i] + parts[i + 1] for i in range(0, len(parts), 2)]
            return cnt + parts[0].astype(F32)
        cnt = lax.fori_loop(0, nkc, body, jnp.zeros((BF16_ROWS, tq), F32))
        return jnp.sum(cnt, axis=0, keepdims=True).astype(I32)

    def hi_step(i, carry):
        t, n_t = carry
        cand = t + lax.shift_left(jnp.int32(1), 15 - i)
        cnt = count16(khi_ref, cand)
        take = cnt >= top_k
        return jnp.where(take, cand, t), jnp.where(take, cnt, n_t)

    h, n_ge_h = lax.fori_loop(0, 16, hi_step,
                              (jnp.full((1, tq), -HALF16, I32), jnp.full((1, tq), n_keys, I32)))
    h16 = h.astype(I16)

    def mark_boundary(k, carry):
        rows = pl.ds(pl.multiple_of(k * kc, kc), kc)
        hi = khi_ref[rows, :]
        klo_ref[rows, :] = jnp.where(hi > h16, jnp.int16(HALF16 - 1),
                                     jnp.where(hi == h16, klo_ref[rows, :], jnp.int16(-HALF16)))
        return carry

    lax.fori_loop(0, nkc, mark_boundary, 0)

    def lo_bit(i, carry):
        t, n_t = carry
        cand = t + lax.shift_left(jnp.int32(1), 15 - i)
        cnt = count16(klo_ref, cand)
        take = cnt >= top_k
        return jnp.where(take, cand, t), jnp.where(take, cnt, n_t)

    def lo_cond(carry):
        r, _, _, open_lanes = carry
        return jnp.logical_and(r < 4, open_lanes > 0)

    def lo_round(carry):
        r, t, n_t, _ = carry
        t, n_t = lax.fori_loop(r * 4, r * 4 + 4, lo_bit, (t, n_t))
        return r + 1, t, n_t, jnp.max(jnp.abs(n_t - top_k))

    _, t_lo, n_ge, _ = lax.while_loop(
        lo_cond, lo_round,
        (jnp.int32(0), jnp.full((1, tq), -HALF16, I32), n_ge_h, jnp.max(jnp.abs(n_ge_h - top_k))))
    thr = lax.shift_left(h, 16) + (t_lo + HALF16)

    has_tie = jnp.where(n_ge > top_k, jnp.where(thr > KEY_NEG_INF, 1, 0), 0)
    any_tie = jnp.max(has_tie)
    jb_ref[...] = jnp.full((1, tq), -1, I32)

    @pl.when(any_tie > 0)
    def _():
        need = top_k - count(lambda blk, r0: blk > thr)

        def idx_step(i, v):
            cand = v + lax.shift_left(jnp.int32(1), (seq.bit_length() - 1) - i)
            cnt = count(lambda blk, r0: (blk == thr) & (r0 + k_off(kc) < cand))
            return jnp.where(cnt < need, cand, v)
        v = lax.fori_loop(0, seq.bit_length(), idx_step, jnp.zeros((1, tq), I32))
        jb_ref[...] = jnp.where(has_tie > 0, v, -1)

    jb = jb_ref[...]
    gt_thr = jnp.where(has_tie > 0, thr, jnp.maximum(thr - 1, KEY_NEG_INF))

    m_ref[...] = jnp.full(m_ref.shape, NEG_BIG, F32)
    l_ref[...] = jnp.zeros(l_ref.shape, F32)
    acc_ref[...] = jnp.zeros(acc_ref.shape, F32)

    def att_chunk(k, carry):
        r0 = pl.multiple_of(k * kc, kc)
        key = keys_ref[pl.ds(r0, kc), :]
        bias = lax.cond(
            any_tie > 0,
            lambda: jnp.where(key > gt_thr, 0.0,
                              jnp.where(key == thr, jnp.where(r0 + k_off(kc) <= jb, 0.0, NEG_BIG), NEG_BIG)),
            lambda: jnp.where(key > gt_thr, 0.0, NEG_BIG))
        cc = c_ref[0, pl.ds(r0, kc), :]
        cct = ct_ref[0, :, pl.ds(r0, kc)]
        for h in range(ATT_HEADS):
            a = _dot(cc, qlatt_ref[0, h * KV_LATENT:(h + 1) * KV_LATENT, :]) + bias
            a_ref[h] = a
            m_old = m_ref[h]
            m_new = jnp.maximum(m_old, jnp.max(a, axis=0, keepdims=True))
            alpha_ref[h] = jnp.exp2(m_old - m_new)
            m_ref[h] = m_new
        for h in range(ATT_HEADS):
            p = jnp.exp2(a_ref[h] - m_ref[h])
            l_ref[h] = alpha_ref[h] * l_ref[h] + jnp.sum(p, axis=0, keepdims=True)
            p_ref[h] = p.astype(BF16)
        for h in range(ATT_HEADS):
            acc_ref[h] = alpha_ref[h] * acc_ref[h] + _dot(cct, p_ref[h])
        return carry

    lax.fori_loop(0, nkc, att_chunk, 0)

    for h in range(ATT_HEADS):
        o_lat = (acc_ref[h] / l_ref[h]).astype(BF16)
        ot_ref[h * ATT_HEAD_DIM:(h + 1) * ATT_HEAD_DIM, :] = _dot(wuvt_ref[h], o_lat)
    out_ref[0] = ot_ref[...].T.astype(BF16)


def _dsa(kidx, c, ct, qidxt, widxt, qlatt, w_uv, *, tq, ks, kc, interpret=False):
    B, L, _ = c.shape
    top_k = min(TOPK_MAX, L // 4)
    assert tq % ks == 0 and tq % kc == 0 and top_k <= tq
    wuvt = jnp.swapaxes(w_uv, 1, 2).astype(BF16)
    kernel = functools.partial(_dsa_kernel, tq=tq, ks=ks, kc=kc, top_k=top_k, seq=L)
    return pl.pallas_call(
        kernel,
        out_shape=jax.ShapeDtypeStruct((B, L, ATT_Q), BF16),
        grid=(B, L // tq),
        in_specs=[
            pl.BlockSpec((1, L, IDX_DIM), lambda b, q: (b, 0, 0)),
            pl.BlockSpec((1, L, KV_LATENT), lambda b, q: (b, 0, 0)),
            pl.BlockSpec((1, KV_LATENT, L), lambda b, q: (b, 0, 0)),
            pl.BlockSpec((1, IDX_Q, tq), lambda b, q: (b, 0, q)),
            pl.BlockSpec((1, IDX_HEADS, tq), lambda b, q: (b, 0, q)),
            pl.BlockSpec((1, ATT_HEADS * KV_LATENT, tq), lambda b, q: (b, 0, q)),
            pl.BlockSpec((ATT_HEADS, ATT_HEAD_DIM, KV_LATENT), lambda b, q: (0, 0, 0)),
        ],
        out_specs=pl.BlockSpec((1, tq, ATT_Q), lambda b, q: (b, q, 0)),
        scratch_shapes=[
            pltpu.VMEM((L, tq), I32),
            pltpu.VMEM((L, tq), I16),
            pltpu.VMEM((L, tq), I16),
            pltpu.VMEM((ATT_HEADS, KV_LATENT, tq), F32),
            pltpu.VMEM((ATT_HEADS, 1, tq), F32),
            pltpu.VMEM((ATT_HEADS, 1, tq), F32),
            pltpu.VMEM((ATT_HEADS, 1, tq), F32),
            pltpu.VMEM((1, tq), I32),
            pltpu.VMEM((ATT_Q, tq), F32),
            pltpu.VMEM((ATT_HEADS, kc, tq), F32),
            pltpu.VMEM((ATT_HEADS, kc, tq), BF16),
        ],
        compiler_params=pltpu.CompilerParams(dimension_semantics=("parallel", "parallel"),
                                             vmem_limit_bytes=VMEM_LIMIT_BYTES),
        name="dsa",
        interpret=interpret,
    )(kidx, c, ct, qidxt, widxt, qlatt, wuvt)


HG_GROUP = 128
HG_PAIRS = HG_HEADS // 2


def _hgrn_kernel(lbl_ref, gn_ref, hq_ref, hf_ref, hi_ref, hg_ref, out_ref, st_ref, *, ts, layer):
    @pl.when(pl.program_id(1) == 0)
    def _():
        st_ref[...] = jnp.zeros(st_ref.shape, F32)

    logits = lbl_ref[...]
    e = jnp.exp(logits - jnp.max(logits, axis=0, keepdims=True))
    lb = jnp.sum(e[0:layer + 1, :], axis=0, keepdims=True) / jnp.sum(e, axis=0, keepdims=True)

    g = HG_GROUP
    chunk_shift = HG_CHUNK.bit_length() - 1
    head_shift = HG_DIM.bit_length() - 1
    r = lax.broadcasted_iota(I32, (g, g), 0)
    cidx = lax.broadcasted_iota(I32, (g, g), 1)
    same_chunk = (r >> chunk_shift) == (cidx >> chunk_shift)
    causal = same_chunk & (cidx <= r)
    tri = jnp.where(causal, 1.0, 0.0).astype(F32)
    ones_blk = jnp.where(same_chunk, 1.0, 0.0).astype(F32)
    same_head = (r >> head_shift) == (cidx >> head_shift)
    head_blk = jnp.where(same_head, 1.0, 0.0).astype(F32)
    lane_lo = lax.broadcasted_iota(I32, (g, LANES), 1) < HG_DIM
    col_chunk = cidx >> chunk_shift
    gn = gn_ref[...]

    def group(gi, carry):
        r0 = pl.multiple_of(gi * g, g)
        rows = pl.ds(r0, g)
        f = lb + (1.0 - lb) * jax.nn.sigmoid(hf_ref[0, rows, :])
        logf = jnp.log(f)
        kgate = 1.0 - f
        b = _dot_exact(tri, logf)
        btot = _dot_exact(ones_blk, logf)
        q_t = jax.nn.silu(hq_ref[0, rows, :]) * jnp.exp(b)
        k_t = kgate * jnp.exp(-b)
        k_dec = kgate * jnp.exp(btot - b)
        decay = jnp.exp(btot)
        v = hi_ref[0, rows, :]
        gate = hg_ref[0, rows, :]
        for p in range(HG_PAIRS):
            ls = slice(p * LANES, (p + 1) * LANES)
            qp = q_t[:, ls]
            kp = k_t[:, ls].astype(BF16)
            vp = v[:, ls].astype(BF16)
            kdp = k_dec[:, ls].astype(BF16)
            q_lo = jnp.where(lane_lo, qp, 0.0).astype(BF16)
            q_hi = jnp.where(lane_lo, 0.0, qp).astype(BF16)
            a_lo = jnp.where(causal, _dot_nt(q_lo, kp), 0.0).astype(BF16)
            a_hi = jnp.where(causal, _dot_nt(q_hi, kp), 0.0).astype(BF16)
            o = jnp.where(lane_lo, _dot(a_lo, vp), _dot(a_hi, vp))
            st = st_ref[p]
            qpb = qp.astype(BF16)
            vt = v[:, ls].T
            inter = []
            for ch in range(g // HG_CHUNK):
                rs = slice(ch * HG_CHUNK, (ch + 1) * HG_CHUNK)
                inter.append(_dot_nt(qpb[rs], st.astype(BF16)))
                vt_ch = jnp.where(col_chunk == ch, vt, 0.0).astype(BF16)
                ds = _dot(vt_ch, kdp)
                st = st * decay[ch * HG_CHUNK:ch * HG_CHUNK + 1, ls] + jnp.where(same_head, ds, 0.0)
            st_ref[p] = st
            o = o + jnp.concatenate(inter, axis=0)
            ms = _dot_exact(o * o, head_blk) * (1.0 / HG_DIM)
            gt = gate[:, ls]
            on = o * lax.rsqrt(ms + EPS) * gn * (gt * jax.nn.sigmoid(gt))
            out_ref[0, rows, ls] = on.astype(BF16)
        return carry

    lax.fori_loop(0, ts // g, group, 0)


def _hgrn(hq, hf, hi, hg, lb_logits, g_norm, *, ts, layer, interpret=False):
    B, L, W = hq.shape
    gn = jnp.tile(g_norm.reshape(1, HG_DIM).astype(F32), (1, 2))
    blk = pl.BlockSpec((1, ts, W), lambda b, t: (b, t, 0))
    return pl.pallas_call(
        functools.partial(_hgrn_kernel, ts=ts, layer=layer),
        out_shape=jax.ShapeDtypeStruct((B, L, W), BF16),
        grid=(B, L // ts),
        in_specs=[pl.BlockSpec(lb_logits.shape, lambda b, t: (0, 0)),
                  pl.BlockSpec((1, LANES), lambda b, t: (0, 0)), blk, blk, blk, blk],
        out_specs=blk,
        scratch_shapes=[pltpu.VMEM((HG_PAIRS, LANES, LANES), F32)],
        compiler_params=pltpu.CompilerParams(dimension_semantics=("parallel", "arbitrary"),
                                             vmem_limit_bytes=VMEM_LIMIT_BYTES),
        name="hgrn2",
        interpret=interpret,
    )(lb_logits.astype(F32), gn, hq, hf, hi, hg)


def _layer_norm(y, g, b):
    mu = jnp.mean(y, axis=-1, keepdims=True)
    d = y - mu
    var = jnp.mean(d * d, axis=-1, keepdims=True)
    return d * lax.rsqrt(var + EPS) * g + b


def _top_desc(s, n, with_rank=False):
    vals = []
    cur = s
    rank = jnp.full(s.shape, float(n), F32) if with_rank else None
    for i in range(n):
        mx = jnp.max(cur, axis=0, keepdims=True)
        vals.append(mx)
        eq = cur == mx
        if with_rank:
            rank = jnp.where(eq, float(i), rank)
        if i + 1 < n:
            cur = jnp.where(eq, -jnp.inf, cur)
    return (vals, rank) if with_rank else vals


PEER_KEEP = PEER_TOPK + 1
_PAIR_RANKS = [(i, j) for i in range(1, PEER_KEEP + 1) for j in range(1, PEER_KEEP + 1) if i * j <= PEER_KEEP]


def _route_kernel(att_ref, hg_ref, x_ref, woa_ref, wob_ref, g1_ref, b1_ref, wq_ref, sk_ref,
                  h_ref, hb_ref, cnt_ref, at_ref, rk_ref, bm_ref, *, alpha):
    mix = _dot(att_ref[...], woa_ref[...]) + _dot(hg_ref[...], wob_ref[...])
    h = _layer_norm(alpha * x_ref[...] + mix, g1_ref[...], b1_ref[...])
    h_ref[...] = h
    hb = h.astype(BF16)
    hb_ref[...] = hb
    qp = _dot(hb, wq_ref[...]).astype(BF16)
    dk = PEER_N_KEYS
    for hd in range(PEER_HEADS):
        s1 = _dot_nt(sk_ref[hd, 0], qp[:, (2 * hd) * dk:(2 * hd + 1) * dk])
        s2 = _dot_nt(sk_ref[hd, 1], qp[:, (2 * hd + 1) * dk:(2 * hd + 2) * dk])
        a = _top_desc(s1, PEER_KEEP)
        bb, rank2 = _top_desc(s2, PEER_KEEP, with_rank=True)
        cands = [a[i - 1] + bb[j - 1] for i, j in _PAIR_RANKS]
        pad = (-len(cands)) % 8
        cand = jnp.concatenate(cands + [jnp.full_like(cands[0], -jnp.inf)] * pad, axis=0)
        top = _top_desc(cand, PEER_KEEP)
        c_max, c_k, c_next = top[0], top[PEER_TOPK - 1], top[PEER_TOPK]
        thr = 0.5 * (c_k + c_next)
        z = jnp.sum(jnp.where(cand >= c_k, jnp.exp(cand - c_max), 0.0), axis=0, keepdims=True)
        cnt = jnp.zeros(s1.shape, F32)
        for r in range(PEER_KEEP):
            cnt = cnt + jnp.where(s1 >= thr - bb[r], 1.0, 0.0)
        cnt_ref[hd] = cnt
        at_ref[hd] = jnp.exp(s1 - a[0]) / z
        rk_ref[hd] = rank2.astype(BF16)
        bm_ref[hd] = jnp.exp(s2 - bb[0]).astype(BF16)


def _route(att, hgo, x2, w_out, g1, b1, w_q, sub_keys, *, tb, alpha, interpret=False):
    T, D = x2.shape
    woa = w_out[:ATT_Q].astype(BF16)
    wob = w_out[ATT_Q:].astype(BF16)

    def full(a):
        zeros = (0,) * a.ndim
        return pl.BlockSpec(a.shape, lambda t: zeros)

    ins = [att, hgo, x2, woa, wob, g1.reshape(1, D).astype(F32), b1.reshape(1, D).astype(F32),
           w_q.astype(BF16), sub_keys.astype(BF16)]
    in_specs = [pl.BlockSpec((tb, ATT_Q), lambda t: (t, 0)), pl.BlockSpec((tb, HG_W), lambda t: (t, 0)),
                pl.BlockSpec((tb, D), lambda t: (t, 0))] + [full(a) for a in ins[3:]]
    rt_shape = lambda dt: jax.ShapeDtypeStruct((PEER_HEADS, PEER_N_KEYS, T), dt)
    rt_spec = pl.BlockSpec((PEER_HEADS, PEER_N_KEYS, tb), lambda t: (0, 0, t))
    return pl.pallas_call(
        functools.partial(_route_kernel, alpha=alpha),
        out_shape=[jax.ShapeDtypeStruct((T, D), F32), jax.ShapeDtypeStruct((T, D), BF16),
                   rt_shape(F32), rt_shape(F32), rt_shape(BF16), rt_shape(BF16)],
        grid=(T // tb,),
        in_specs=in_specs,
        out_specs=[pl.BlockSpec((tb, D), lambda t: (t, 0)), pl.BlockSpec((tb, D), lambda t: (t, 0)),
                   rt_spec, rt_spec, rt_spec, rt_spec],
        compiler_params=pltpu.CompilerParams(dimension_semantics=("parallel",),
                                             vmem_limit_bytes=VMEM_LIMIT_BYTES),
        name="route",
        interpret=interpret,
    )(*ins)


GELU_K = 2.0 * (2.0 / 3.141592653589793) ** 0.5


def _gelu_tanh(x):
    z = x * ((-GELU_K * LOG2E) + (-GELU_K * 0.044715 * LOG2E) * (x * x))
    return x / (1.0 + jnp.exp2(z))


def _peer_kernel(hb_ref, h_ref, u_ref, vt_ref, cnt_ref, at_ref, rk_ref, bm_ref, g2_ref, b2_ref,
                 out_ref, acc_ref, pre_ref, p_ref, *, ic, tw, alpha):
    c = pl.program_id(1)

    @pl.when(c == 0)
    def _():
        acc_ref[...] = jnp.zeros(acc_ref.shape, F32)

    nk = PEER_N_KEYS
    tb = hb_ref.shape[0]
    grp = nk // BF16_ROWS
    pre_ref[...] = _dot_nt(u_ref[...], hb_ref[...])
    for ii in range(ic):
        rows = slice(ii * nk, (ii + 1) * nk)
        for s in range(tb // tw):
            ls = slice(s * tw, (s + 1) * tw)
            gsum = jnp.zeros((grp, BF16_ROWS, tw), BF16)
            for hd in range(PEER_HEADS):
                cnt = jnp.broadcast_to(cnt_ref[hd, ii:ii + 1, ls], (BF16_ROWS, tw)).astype(BF16)
                wgt = jnp.broadcast_to(at_ref[hd, ii:ii + 1, ls], (BF16_ROWS, tw)).astype(BF16)
                rk = rk_ref[hd, :, ls].reshape(grp, BF16_ROWS, tw)
                bm = bm_ref[hd, :, ls].reshape(grp, BF16_ROWS, tw)
                gsum = gsum + jnp.where(rk < cnt[None], bm, jnp.zeros_like(bm)) * wgt[None]
            act = _gelu_tanh(pre_ref[rows, ls]).astype(BF16)
            p_ref[rows, ls] = gsum.reshape(nk, tw) * act
    acc_ref[...] += _dot(vt_ref[...], p_ref[...])

    @pl.when(c == pl.num_programs(1) - 1)
    def _():
        y = alpha * h_ref[...] + acc_ref[...].T
        out_ref[...] = _layer_norm(y, g2_ref[...], b2_ref[...])


def _peer(hb, h, u, v, cnt, at, rk, bm, g2, b2, *, tb, ic, tw, alpha, interpret=False):
    T, D = h.shape
    n_exp = u.shape[0]
    ub = u.astype(BF16)
    vtb = v.astype(BF16).T
    ec = ic * PEER_N_KEYS
    rows = pl.BlockSpec((PEER_HEADS, ic, tb), lambda t, c: (0, c, t))
    whole = pl.BlockSpec((PEER_HEADS, PEER_N_KEYS, tb), lambda t, c: (0, 0, t))
    vec = pl.BlockSpec((1, D), lambda t, c: (0, 0))
    return pl.pallas_call(
        functools.partial(_peer_kernel, ic=ic, tw=tw, alpha=alpha),
        out_shape=jax.ShapeDtypeStruct((T, D), F32),
        grid=(T // tb, n_exp // ec),
        in_specs=[pl.BlockSpec((tb, D), lambda t, c: (t, 0)), pl.BlockSpec((tb, D), lambda t, c: (t, 0)),
                  pl.BlockSpec((ec, D), lambda t, c: (c, 0)), pl.BlockSpec((D, ec), lambda t, c: (0, c)),
                  rows, rows, whole, whole, vec, vec],
        out_specs=pl.BlockSpec((tb, D), lambda t, c: (t, 0)),
        scratch_shapes=[pltpu.VMEM((D, tb), F32), pltpu.VMEM((ec, tb), F32), pltpu.VMEM((ec, tb), BF16)],
        compiler_params=pltpu.CompilerParams(dimension_semantics=("parallel", "arbitrary"),
                                             vmem_limit_bytes=VMEM_LIMIT_BYTES),
        name="peer",
        interpret=interpret,
    )(hb, h, ub, vtb, cnt, at, rk, bm, g2.reshape(1, D).astype(F32), b2.reshape(1, D).astype(F32))


def _tile(n, pref):
    t = min(pref, n)
    assert n % t == 0, (n, t)
    return t


def kernel(x, w_in, kv_norm_g, w_uk, w_uv, hg_lb_logits, hg_norm_g, w_out, ln1_g, ln1_b,
           peer_w_q, peer_sub_keys, peer_u, peer_v, ln2_g, ln2_b):
    B, L, D = x.shape
    depth = w_in.shape[0]
    alpha = (2.0 * depth) ** 0.25
    T = B * L
    for layer in range(depth):
        (c, ct, kidx, qidxt, widxt, qlatt, hq, hf, hi, hg) = _proj(
            x, w_in[layer], kv_norm_g[layer], w_uk[layer], tb=_tile(L, 512))
        tq = _tile(L, 256)
        att = _dsa(kidx, c, ct, qidxt, widxt, qlatt, w_uv[layer], tq=tq, ks=min(128, tq), kc=tq)
        hgo = _hgrn(hq, hf, hi, hg, hg_lb_logits, hg_norm_g[layer], ts=_tile(L, 512), layer=layer)
        h, hb, rcnt, rat, rrk, rbm = _route(
            att.reshape(T, ATT_Q), hgo.reshape(T, HG_W), x.reshape(T, D), w_out[layer], ln1_g[layer], ln1_b[layer],
            peer_w_q[layer], peer_sub_keys[layer], tb=_tile(T, 512), alpha=alpha)
        y = _peer(hb, h, peer_u[layer], peer_v[layer], rcnt, rat, rrk, rbm, ln2_g[layer], ln2_b[layer],
                  tb=_tile(T, 512), ic=8, tw=256, alpha=alpha)
        x = y.reshape(B, L, D)
    return x
```

```python
import functools

import jax
import jax.numpy as jnp
from jax import lax
from jax.experimental import pallas as pl
from jax.experimental.pallas import tpu as pltpu

F32 = jnp.float32
BF16 = jnp.bfloat16
I32 = jnp.int32
I16 = jnp.int16

ATT_HEADS = 8
ATT_HEAD_DIM = 64
KV_LATENT = 128
IDX_HEADS = 8
IDX_DIM = 64
TOPK_MAX = 256
HG_HEADS = 8
HG_DIM = 64
HG_CHUNK = 32
PEER_HEADS = 8
PEER_N_KEYS = 128
PEER_TOPK = 16
EPS = 1e-5

ATT_Q = ATT_HEADS * ATT_HEAD_DIM
IDX_Q = IDX_HEADS * IDX_DIM
HG_W = HG_HEADS * HG_DIM

LANES = 128
VMEM_LIMIT_BYTES = 56 * 1024 * 1024

LOG2E = 1.4426950408889634
NEG_BIG = -1e30
INT_MIN = -(2 ** 31)
HALF16 = 1 << 15
BF16_ROWS = 16
KEY_NEG_INF = INT_MIN + 0x7FFFFF


def _dot(a, b):
    return jnp.dot(a, b, preferred_element_type=F32)


def _dot_nt(a, b):
    return lax.dot_general(a, b, (((1,), (1,)), ((), ())), preferred_element_type=F32)


def _dot_exact(a, b):
    return jnp.dot(a, b, preferred_element_type=F32, precision=lax.Precision.HIGHEST)


def _proj_kernel(x_ref, wc_ref, wk_ref, whq_ref, whf_ref, whi_ref, whg_ref,
                 wtq_ref, wtqi_ref, wtwi_ref, wtc_ref, grow_ref, gcol_ref, wuk_ref,
                 c_ref, ct_ref, kidx_ref, qidxt_ref, widxt_ref, qlatt_ref,
                 hq_ref, hf_ref, hi_ref, hg_ref):
    xb = x_ref[0].astype(BF16)

    c = _dot(xb, wc_ref[...])
    c = c * lax.rsqrt(jnp.mean(c * c, axis=-1, keepdims=True) + EPS) * grow_ref[...]
    c_ref[0] = c.astype(BF16)
    ct = _dot_nt(wtc_ref[...], xb)
    ct = ct * lax.rsqrt(jnp.mean(ct * ct, axis=0, keepdims=True) + EPS) * gcol_ref[...]
    ct_ref[0] = ct.astype(BF16)

    kidx_ref[0] = _dot(xb, wk_ref[...]).astype(BF16)
    qidxt_ref[0] = _dot_nt(wtqi_ref[...], xb).astype(BF16)
    widxt_ref[0] = _dot_nt(wtwi_ref[...], xb)

    qt = _dot_nt(wtq_ref[...], xb).astype(BF16)
    scale = ATT_HEAD_DIM ** -0.5 * LOG2E
    for h in range(ATT_HEADS):
        ql = _dot(wuk_ref[h], qt[h * ATT_HEAD_DIM:(h + 1) * ATT_HEAD_DIM, :]) * scale
        qlatt_ref[0, h * KV_LATENT:(h + 1) * KV_LATENT, :] = ql.astype(BF16)

    hq_ref[0] = _dot(xb, whq_ref[...])
    hf_ref[0] = _dot(xb, whf_ref[...])
    hi_ref[0] = _dot(xb, whi_ref[...])
    hg_ref[0] = _dot(xb, whg_ref[...])


def _proj(x, w, kv_g, w_uk, *, tb, interpret=False):
    B, L, D = x.shape
    o = 0
    cols = {}
    for name, width in (("q", ATT_Q), ("c", KV_LATENT), ("qi", IDX_Q), ("k", IDX_DIM), ("wi", IDX_HEADS),
                        ("hq", HG_W), ("hf", HG_W), ("hi", HG_W), ("hg", HG_W)):
        cols[name] = w[:, o:o + width].astype(BF16)
        o += width
    assert o == w.shape[1]
    wt = lambda n: cols[n].T
    ins = [x, cols["c"], cols["k"], cols["hq"], cols["hf"], cols["hi"], cols["hg"],
           wt("q"), wt("qi"), wt("wi"), wt("c"),
           kv_g.reshape(1, KV_LATENT).astype(F32), kv_g.reshape(KV_LATENT, 1).astype(F32),
           w_uk.astype(BF16)]

    def full(a):
        zeros = (0,) * a.ndim
        return pl.BlockSpec(a.shape, lambda b, t: zeros)

    in_specs = [pl.BlockSpec((1, tb, D), lambda b, t: (b, t, 0))] + [full(a) for a in ins[1:]]
    nat = lambda wd, dt: (jax.ShapeDtypeStruct((B, L, wd), dt), pl.BlockSpec((1, tb, wd), lambda b, t: (b, t, 0)))
    tr = lambda wd, dt: (jax.ShapeDtypeStruct((B, wd, L), dt), pl.BlockSpec((1, wd, tb), lambda b, t: (b, 0, t)))
    outs = [nat(KV_LATENT, BF16), tr(KV_LATENT, BF16), nat(IDX_DIM, BF16), tr(IDX_Q, BF16), tr(IDX_HEADS, F32),
            tr(ATT_HEADS * KV_LATENT, BF16), nat(HG_W, F32), nat(HG_W, F32), nat(HG_W, F32), nat(HG_W, F32)]
    return pl.pallas_call(
        _proj_kernel,
        out_shape=[s for s, _ in outs],
        grid=(B, L // tb),
        in_specs=in_specs,
        out_specs=[s for _, s in outs],
        compiler_params=pltpu.CompilerParams(dimension_semantics=("parallel", "parallel"),
                                             vmem_limit_bytes=VMEM_LIMIT_BYTES),
        name="proj",
        interpret=interpret,
    )(*ins)


def _dsa_kernel(kidx_ref, c_ref, ct_ref, qidxt_ref, widxt_ref, qlatt_ref, wuvt_ref,
                out_ref, keys_ref, khi_ref, klo_ref, acc_ref, m_ref, l_ref, alpha_ref, jb_ref, ot_ref, a_ref, p_ref,
                *, tq, ks, kc, top_k, seq):
    qi = pl.program_id(1)
    n_keys = (qi + 1) * tq
    nks = (qi + 1) * (tq // ks)
    nkc = (qi + 1) * (tq // kc)
    q_pos = lambda rows: qi * tq + lax.broadcasted_iota(I32, (rows, tq), 1)
    k_off = lambda rows: lax.broadcasted_iota(I32, (rows, tq), 0)

    def score_chunk(k, carry):
        r0 = pl.multiple_of(k * ks, ks)
        kk = kidx_ref[0, pl.ds(r0, ks), :]
        s = jnp.zeros((ks, tq), F32)
        for h in range(IDX_HEADS):
            z = _dot(kk, qidxt_ref[0, h * IDX_DIM:(h + 1) * IDX_DIM, :])
            s = s + widxt_ref[0, h:h + 1, :] * jnp.maximum(z, 0.0)
        s = jnp.where(r0 + k_off(ks) <= q_pos(ks), s, -jnp.inf)
        bits = pltpu.bitcast(s, I32)
        key = bits ^ ((bits >> 31) & 0x7FFFFFFF)
        keys_ref[pl.ds(r0, ks), :] = key
        khi_ref[pl.ds(r0, ks), :] = (key >> 16).astype(I16)
        klo_ref[pl.ds(r0, ks), :] = ((key & 0xFFFF) - HALF16).astype(I16)
        return carry

    lax.fori_loop(0, nks, score_chunk, 0)

    kcc = 2 * kc
    nkcc = (nkc + 1) >> 1

    @pl.when(n_keys < seq)
    def _():
        khi_ref[pl.ds(pl.multiple_of(n_keys, kc), kc), :] = jnp.full((kc, tq), -HALF16, I16)

    def count(pred):
        def body(k, cnt):
            r0 = pl.multiple_of(k * kc, kc)
            blk = keys_ref[pl.ds(r0, kc), :]
            return cnt + jnp.sum(jnp.where(pred(blk, r0), 1, 0).astype(I32), axis=0, keepdims=True)
        return lax.fori_loop(0, nkc, body, jnp.zeros((1, tq), I32))

    def count16(ref, cand):
        c16 = cand.astype(I16)

        def body(k, cnt):
            r0 = pl.multiple_of(k * kcc, kcc)
            blk = ref[pl.ds(r0, kcc), :]
            ones = jnp.where(blk >= c16, jnp.ones(blk.shape, BF16), jnp.zeros(blk.shape, BF16))
            parts = [ones[g * BF16_ROWS:(g + 1) * BF16_ROWS] for g in range(kcc // BF16_ROWS)]
            while len(parts) > 1:
                parts = [parts[i] + parts[i + 1] for i in range(0, len(parts), 2)]
            return cnt + parts[0].astype(F32)
        cnt = lax.fori_loop(0, nkcc, body, jnp.zeros((BF16_ROWS, tq), F32))
        return jnp.sum(cnt, axis=0, keepdims=True).astype(I32)

    def hi_step(i, carry):
        t, n_t = carry
        cand = t + lax.shift_left(jnp.int32(1), 15 - i)
        cnt = count16(khi_ref, cand)
        take = cnt >= top_k
        return jnp.where(take, cand, t), jnp.where(take, cnt, n_t)

    h, n_ge_h = lax.fori_loop(0, 16, hi_step,
                              (jnp.full((1, tq), -HALF16, I32), jnp.full((1, tq), n_keys, I32)))
    h16 = h.astype(I16)

    def mark_boundary(k, carry):
        rows = pl.ds(pl.multiple_of(k * kcc, kcc), kcc)
        hi = khi_ref[rows, :]
        klo_ref[rows, :] = jnp.where(hi > h16, jnp.int16(HALF16 - 1),
                                     jnp.where(hi == h16, klo_ref[rows, :], jnp.int16(-HALF16)))
        return carry

    lax.fori_loop(0, nkcc, mark_boundary, 0)

    def lo_bit(i, carry):
        t, n_t = carry
        cand = t + lax.shift_left(jnp.int32(1), 15 - i)
        cnt = count16(klo_ref, cand)
        take = cnt >= top_k
        return jnp.where(take, cand, t), jnp.where(take, cnt, n_t)

    def lo_cond(carry):
        r, _, _, open_lanes = carry
        return jnp.logical_and(r < 4, open_lanes > 0)

    def lo_round(carry):
        r, t, n_t, _ = carry
        t, n_t = lax.fori_loop(r * 4, r * 4 + 4, lo_bit, (t, n_t))
        return r + 1, t, n_t, jnp.max(jnp.abs(n_t - top_k))

    _, t_lo, n_ge, _ = lax.while_loop(
        lo_cond, lo_round,
        (jnp.int32(0), jnp.full((1, tq), -HALF16, I32), n_ge_h, jnp.max(jnp.abs(n_ge_h - top_k))))
    thr = lax.shift_left(h, 16) + (t_lo + HALF16)

    has_tie = jnp.where(n_ge > top_k, jnp.where(thr > KEY_NEG_INF, 1, 0), 0)
    any_tie = jnp.max(has_tie)
    jb_ref[...] = jnp.full((1, tq), -1, I32)

    @pl.when(any_tie > 0)
    def _():
        need = top_k - count(lambda blk, r0: blk > thr)

        def idx_step(i, v):
            cand = v + lax.shift_left(jnp.int32(1), (seq.bit_length() - 1) - i)
            cnt = count(lambda blk, r0: (blk == thr) & (r0 + k_off(kc) < cand))
            return jnp.where(cnt < need, cand, v)
        v = lax.fori_loop(0, seq.bit_length(), idx_step, jnp.zeros((1, tq), I32))
        jb_ref[...] = jnp.where(has_tie > 0, v, -1)

    jb = jb_ref[...]
    gt_thr = jnp.where(has_tie > 0, thr, jnp.maximum(thr - 1, KEY_NEG_INF))

    l_ref[...] = jnp.zeros(l_ref.shape, F32)
    acc_ref[...] = jnp.zeros(acc_ref.shape, F32)

    def masked_bias(r0):
        key = keys_ref[pl.ds(r0, kc), :]
        return lax.cond(
            any_tie > 0,
            lambda: jnp.where(key > gt_thr, 0.0,
                              jnp.where(key == thr, jnp.where(r0 + k_off(kc) <= jb, 0.0, NEG_BIG), NEG_BIG)),
            lambda: jnp.where(key > gt_thr, 0.0, NEG_BIG))

    def logits(h, cc, bias, m_prev, slot):
        a = _dot(cc, qlatt_ref[0, h * KV_LATENT:(h + 1) * KV_LATENT, :]) + bias
        a_ref[slot, h] = a
        m_new = jnp.maximum(m_prev, jnp.max(a, axis=0, keepdims=True))
        alpha_ref[slot, h] = jnp.exp2(m_prev - m_new)
        m_ref[slot, h] = m_new

    bias0 = masked_bias(0)
    cc0 = c_ref[0, pl.ds(0, kc), :]
    for h in range(ATT_HEADS):
        logits(h, cc0, bias0, jnp.full((1, tq), NEG_BIG, F32), 0)

    def att_step(k, cur):
        nxt = 1 - cur
        r0 = pl.multiple_of(k * kc, kc)
        rn = pl.multiple_of(jnp.minimum(k + 1, nkc - 1) * kc, kc)
        bias_n = masked_bias(rn)
        cc_n = c_ref[0, pl.ds(rn, kc), :]
        cct = ct_ref[0, :, pl.ds(r0, kc)]
        for h in range(ATT_HEADS):
            m_cur = m_ref[cur, h]
            logits(h, cc_n, bias_n, m_cur, nxt)
            p = jnp.exp2(a_ref[cur, h] - m_cur)
            l_ref[h] = alpha_ref[cur, h] * l_ref[h] + jnp.sum(p, axis=0, keepdims=True)
            p_ref[h] = p.astype(BF16)
            acc_ref[h] = alpha_ref[cur, h] * acc_ref[h] + _dot(cct, p_ref[h])

    def att_chunk(k, carry):
        for parity in range(2):
            pl.when(k % 2 == parity)(functools.partial(att_step, k, parity))
        return carry

    lax.fori_loop(0, nkc, att_chunk, 0)

    for h in range(ATT_HEADS):
        o_lat = (acc_ref[h] / l_ref[h]).astype(BF16)
        ot_ref[h * ATT_HEAD_DIM:(h + 1) * ATT_HEAD_DIM, :] = _dot(wuvt_ref[h], o_lat)
    out_ref[0] = ot_ref[...].T.astype(BF16)


def _dsa(kidx, c, ct, qidxt, widxt, qlatt, w_uv, *, tq, ks, kc, interpret=False):
    B, L, _ = c.shape
    top_k = min(TOPK_MAX, L // 4)
    assert tq % ks == 0 and tq % kc == 0 and top_k <= tq and L % (2 * kc) == 0 and 2 * kc // BF16_ROWS <= 256
    wuvt = jnp.swapaxes(w_uv, 1, 2).astype(BF16)
    kernel = functools.partial(_dsa_kernel, tq=tq, ks=ks, kc=kc, top_k=top_k, seq=L)
    return pl.pallas_call(
        kernel,
        out_shape=jax.ShapeDtypeStruct((B, L, ATT_Q), BF16),
        grid=(B, L // tq),
        in_specs=[
            pl.BlockSpec((1, L, IDX_DIM), lambda b, q: (b, 0, 0)),
            pl.BlockSpec((1, L, KV_LATENT), lambda b, q: (b, 0, 0)),
            pl.BlockSpec((1, KV_LATENT, L), lambda b, q: (b, 0, 0)),
            pl.BlockSpec((1, IDX_Q, tq), lambda b, q: (b, 0, q)),
            pl.BlockSpec((1, IDX_HEADS, tq), lambda b, q: (b, 0, q)),
            pl.BlockSpec((1, ATT_HEADS * KV_LATENT, tq), lambda b, q: (b, 0, q)),
            pl.BlockSpec((ATT_HEADS, ATT_HEAD_DIM, KV_LATENT), lambda b, q: (0, 0, 0)),
        ],
        out_specs=pl.BlockSpec((1, tq, ATT_Q), lambda b, q: (b, q, 0)),
        scratch_shapes=[
            pltpu.VMEM((L, tq), I32),
            pltpu.VMEM((L, tq), I16),
            pltpu.VMEM((L, tq), I16),
            pltpu.VMEM((ATT_HEADS, KV_LATENT, tq), F32),
            pltpu.VMEM((2, ATT_HEADS, 1, tq), F32),
            pltpu.VMEM((ATT_HEADS, 1, tq), F32),
            pltpu.VMEM((2, ATT_HEADS, 1, tq), F32),
            pltpu.VMEM((1, tq), I32),
            pltpu.VMEM((ATT_Q, tq), F32),
            pltpu.VMEM((2, ATT_HEADS, kc, tq), F32),
            pltpu.VMEM((ATT_HEADS, kc, tq), BF16),
        ],
        compiler_params=pltpu.CompilerParams(dimension_semantics=("parallel", "parallel"),
                                             vmem_limit_bytes=VMEM_LIMIT_BYTES),
        name="dsa",
        interpret=interpret,
    )(kidx, c, ct, qidxt, widxt, qlatt, wuvt)


HG_GROUP = 128
HG_PAIRS = HG_HEADS // 2


def _hgrn_kernel(lbl_ref, gn_ref, hq_ref, hf_ref, hi_ref, hg_ref, out_ref, st_ref, *, ts, layer):
    @pl.when(pl.program_id(1) == 0)
    def _():
        st_ref[...] = jnp.zeros(st_ref.shape, F32)

    logits = lbl_ref[...]
    e = jnp.exp(logits - jnp.max(logits, axis=0, keepdims=True))
    lb = jnp.sum(e[0:layer + 1, :], axis=0, keepdims=True) / jnp.sum(e, axis=0, keepdims=True)

    g = HG_GROUP
    chunk_shift = HG_CHUNK.bit_length() - 1
    head_shift = HG_DIM.bit_length() - 1
    r = lax.broadcasted_iota(I32, (g, g), 0)
    cidx = lax.broadcasted_iota(I32, (g, g), 1)
    same_chunk = (r >> chunk_shift) == (cidx >> chunk_shift)
    causal = same_chunk & (cidx <= r)
    tri = jnp.where(causal, 1.0, 0.0).astype(F32)
    ones_blk = jnp.where(same_chunk, 1.0, 0.0).astype(F32)
    same_head = (r >> head_shift) == (cidx >> head_shift)
    head_blk = jnp.where(same_head, 1.0, 0.0).astype(F32)
    lane_lo = lax.broadcasted_iota(I32, (g, LANES), 1) < HG_DIM
    col_chunk = cidx >> chunk_shift
    gn = gn_ref[...]

    def group(gi, carry):
        r0 = pl.multiple_of(gi * g, g)
        rows = pl.ds(r0, g)
        f = lb + (1.0 - lb) * jax.nn.sigmoid(hf_ref[0, rows, :])
        logf = jnp.log(f)
        kgate = 1.0 - f
        b = _dot_exact(tri, logf)
        btot = _dot_exact(ones_blk, logf)
        q_t = jax.nn.silu(hq_ref[0, rows, :]) * jnp.exp(b)
        k_t = kgate * jnp.exp(-b)
        k_dec = kgate * jnp.exp(btot - b)
        decay = jnp.exp(btot)
        v = hi_ref[0, rows, :]
        gate = hg_ref[0, rows, :]
        for p in range(HG_PAIRS):
            ls = slice(p * LANES, (p + 1) * LANES)
            qp = q_t[:, ls]
            kp = k_t[:, ls].astype(BF16)
            vp = v[:, ls].astype(BF16)
            kdp = k_dec[:, ls].astype(BF16)
            q_lo = jnp.where(lane_lo, qp, 0.0).astype(BF16)
            q_hi = jnp.where(lane_lo, 0.0, qp).astype(BF16)
            a_lo = jnp.where(causal, _dot_nt(q_lo, kp), 0.0).astype(BF16)
            a_hi = jnp.where(causal, _dot_nt(q_hi, kp), 0.0).astype(BF16)
            o = jnp.where(lane_lo, _dot(a_lo, vp), _dot(a_hi, vp))
            st = st_ref[p]
            qpb = qp.astype(BF16)
            vt = v[:, ls].T
            inter = []
            for ch in range(g // HG_CHUNK):
                rs = slice(ch * HG_CHUNK, (ch + 1) * HG_CHUNK)
                inter.append(_dot_nt(qpb[rs], st.astype(BF16)))
                vt_ch = jnp.where(col_chunk == ch, vt, 0.0).astype(BF16)
                ds = _dot(vt_ch, kdp)
                st = st * decay[ch * HG_CHUNK:ch * HG_CHUNK + 1, ls] + jnp.where(same_head, ds, 0.0)
            st_ref[p] = st
            o = o + jnp.concatenate(inter, axis=0)
            ms = _dot_exact(o * o, head_blk) * (1.0 / HG_DIM)
            gt = gate[:, ls]
            on = o * lax.rsqrt(ms + EPS) * gn * (gt * jax.nn.sigmoid(gt))
            out_ref[0, rows, ls] = on.astype(BF16)
        return carry

    lax.fori_loop(0, ts // g, group, 0)


def _hgrn(hq, hf, hi, hg, lb_logits, g_norm, *, ts, layer, interpret=False):
    B, L, W = hq.shape
    gn = jnp.tile(g_norm.reshape(1, HG_DIM).astype(F32), (1, 2))
    blk = pl.BlockSpec((1, ts, W), lambda b, t: (b, t, 0))
    return pl.pallas_call(
        functools.partial(_hgrn_kernel, ts=ts, layer=layer),
        out_shape=jax.ShapeDtypeStruct((B, L, W), BF16),
        grid=(B, L // ts),
        in_specs=[pl.BlockSpec(lb_logits.shape, lambda b, t: (0, 0)),
                  pl.BlockSpec((1, LANES), lambda b, t: (0, 0)), blk, blk, blk, blk],
        out_specs=blk,
        scratch_shapes=[pltpu.VMEM((HG_PAIRS, LANES, LANES), F32)],
        compiler_params=pltpu.CompilerParams(dimension_semantics=("parallel", "arbitrary"),
                                             vmem_limit_bytes=VMEM_LIMIT_BYTES),
        name="hgrn2",
        interpret=interpret,
    )(lb_logits.astype(F32), gn, hq, hf, hi, hg)


def _layer_norm(y, g, b):
    mu = jnp.mean(y, axis=-1, keepdims=True)
    d = y - mu
    var = jnp.mean(d * d, axis=-1, keepdims=True)
    return d * lax.rsqrt(var + EPS) * g + b


def _top_desc(s, n, with_rank=False):
    vals = []
    cur = s
    rank = jnp.full(s.shape, float(n), F32) if with_rank else None
    for i in range(n):
        mx = jnp.max(cur, axis=0, keepdims=True)
        vals.append(mx)
        eq = cur == mx
        if with_rank:
            rank = jnp.where(eq, float(i), rank)
        if i + 1 < n:
            cur = jnp.where(eq, -jnp.inf, cur)
    return (vals, rank) if with_rank else vals


PEER_KEEP = PEER_TOPK + 1
_PAIR_RANKS = [(i, j) for i in range(1, PEER_KEEP + 1) for j in range(1, PEER_KEEP + 1) if i * j <= PEER_KEEP]


def _route_kernel(att_ref, hg_ref, x_ref, woa_ref, wob_ref, g1_ref, b1_ref, wq_ref, sk_ref,
                  h_ref, hb_ref, cnt_ref, at_ref, rk_ref, bm_ref, *, alpha):
    mix = _dot(att_ref[...], woa_ref[...]) + _dot(hg_ref[...], wob_ref[...])
    h = _layer_norm(alpha * x_ref[...] + mix, g1_ref[...], b1_ref[...])
    h_ref[...] = h
    hb = h.astype(BF16)
    hb_ref[...] = hb
    qp = _dot(hb, wq_ref[...]).astype(BF16)
    dk = PEER_N_KEYS
    for hd in range(PEER_HEADS):
        s1 = _dot_nt(sk_ref[hd, 0], qp[:, (2 * hd) * dk:(2 * hd + 1) * dk])
        s2 = _dot_nt(sk_ref[hd, 1], qp[:, (2 * hd + 1) * dk:(2 * hd + 2) * dk])
        a = _top_desc(s1, PEER_KEEP)
        bb, rank2 = _top_desc(s2, PEER_KEEP, with_rank=True)
        cands = [a[i - 1] + bb[j - 1] for i, j in _PAIR_RANKS]
        pad = (-len(cands)) % 8
        cand = jnp.concatenate(cands + [jnp.full_like(cands[0], -jnp.inf)] * pad, axis=0)
        top = _top_desc(cand, PEER_KEEP)
        c_max, c_k, c_next = top[0], top[PEER_TOPK - 1], top[PEER_TOPK]
        thr = 0.5 * (c_k + c_next)
        z = jnp.sum(jnp.where(cand >= c_k, jnp.exp(cand - c_max), 0.0), axis=0, keepdims=True)
        cnt = jnp.zeros(s1.shape, F32)
        for r in range(PEER_KEEP):
            cnt = cnt + jnp.where(s1 >= thr - bb[r], 1.0, 0.0)
        cnt_ref[hd] = cnt
        at_ref[hd] = jnp.exp(s1 - a[0]) / z
        rk_ref[hd] = rank2.astype(BF16)
        bm_ref[hd] = jnp.exp(s2 - bb[0]).astype(BF16)


def _route(att, hgo, x2, w_out, g1, b1, w_q, sub_keys, *, tb, alpha, interpret=False):
    T, D = x2.shape
    woa = w_out[:ATT_Q].astype(BF16)
    wob = w_out[ATT_Q:].astype(BF16)

    def full(a):
        zeros = (0,) * a.ndim
        return pl.BlockSpec(a.shape, lambda t: zeros)

    ins = [att, hgo, x2, woa, wob, g1.reshape(1, D).astype(F32), b1.reshape(1, D).astype(F32),
           w_q.astype(BF16), sub_keys.astype(BF16)]
    in_specs = [pl.BlockSpec((tb, ATT_Q), lambda t: (t, 0)), pl.BlockSpec((tb, HG_W), lambda t: (t, 0)),
                pl.BlockSpec((tb, D), lambda t: (t, 0))] + [full(a) for a in ins[3:]]
    rt_shape = lambda dt: jax.ShapeDtypeStruct((PEER_HEADS, PEER_N_KEYS, T), dt)
    rt_spec = pl.BlockSpec((PEER_HEADS, PEER_N_KEYS, tb), lambda t: (0, 0, t))
    return pl.pallas_call(
        functools.partial(_route_kernel, alpha=alpha),
        out_shape=[jax.ShapeDtypeStruct((T, D), F32), jax.ShapeDtypeStruct((T, D), BF16),
                   rt_shape(F32), rt_shape(F32), rt_shape(BF16), rt_shape(BF16)],
        grid=(T // tb,),
        in_specs=in_specs,
        out_specs=[pl.BlockSpec((tb, D), lambda t: (t, 0)), pl.BlockSpec((tb, D), lambda t: (t, 0)),
                   rt_spec, rt_spec, rt_spec, rt_spec],
        compiler_params=pltpu.CompilerParams(dimension_semantics=("parallel",),
                                             vmem_limit_bytes=VMEM_LIMIT_BYTES),
        name="route",
        interpret=interpret,
    )(*ins)


GELU_K = 2.0 * (2.0 / 3.141592653589793) ** 0.5


def _gelu_tanh(x):
    z = x * ((-GELU_K * LOG2E) + (-GELU_K * 0.044715 * LOG2E) * (x * x))
    return x / (1.0 + jnp.exp2(z))


def _peer_kernel(hb_ref, h_ref, u_ref, vt_ref, cnt_ref, at_ref, rk_ref, bm_ref, g2_ref, b2_ref,
                 out_ref, acc_ref, pre_ref, p_ref, *, ic, tw, alpha):
    c = pl.program_id(1)

    @pl.when(c == 0)
    def _():
        acc_ref[...] = jnp.zeros(acc_ref.shape, F32)

    nk = PEER_N_KEYS
    tb = hb_ref.shape[0]
    grp = nk // BF16_ROWS
    pre_ref[...] = _dot_nt(u_ref[...], hb_ref[...])
    for ii in range(ic):
        rows = slice(ii * nk, (ii + 1) * nk)
        for s in range(tb // tw):
            ls = slice(s * tw, (s + 1) * tw)
            gsum = jnp.zeros((grp, BF16_ROWS, tw), BF16)
            for hd in range(PEER_HEADS):
                cnt = jnp.broadcast_to(cnt_ref[hd, ii:ii + 1, ls], (BF16_ROWS, tw)).astype(BF16)
                wgt = jnp.broadcast_to(at_ref[hd, ii:ii + 1, ls], (BF16_ROWS, tw)).astype(BF16)
                rk = rk_ref[hd, :, ls].reshape(grp, BF16_ROWS, tw)
                bm = bm_ref[hd, :, ls].reshape(grp, BF16_ROWS, tw)
                gsum = gsum + jnp.where(rk < cnt[None], bm, jnp.zeros_like(bm)) * wgt[None]
            act = _gelu_tanh(pre_ref[rows, ls]).astype(BF16)
            p_ref[rows, ls] = gsum.reshape(nk, tw) * act
    acc_ref[...] += _dot(vt_ref[...], p_ref[...])

    @pl.when(c == pl.num_programs(1) - 1)
    def _():
        y = alpha * h_ref[...] + acc_ref[...].T
        out_ref[...] = _layer_norm(y, g2_ref[...], b2_ref[...])


def _peer(hb, h, u, v, cnt, at, rk, bm, g2, b2, *, tb, ic, tw, alpha, interpret=False):
    T, D = h.shape
    n_exp = u.shape[0]
    ub = u.astype(BF16)
    vtb = v.astype(BF16).T
    ec = ic * PEER_N_KEYS
    rows = pl.BlockSpec((PEER_HEADS, ic, tb), lambda t, c: (0, c, t))
    whole = pl.BlockSpec((PEER_HEADS, PEER_N_KEYS, tb), lambda t, c: (0, 0, t))
    vec = pl.BlockSpec((1, D), lambda t, c: (0, 0))
    return pl.pallas_call(
        functools.partial(_peer_kernel, ic=ic, tw=tw, alpha=alpha),
        out_shape=jax.ShapeDtypeStruct((T, D), F32),
        grid=(T // tb, n_exp // ec),
        in_specs=[pl.BlockSpec((tb, D), lambda t, c: (t, 0)), pl.BlockSpec((tb, D), lambda t, c: (t, 0)),
                  pl.BlockSpec((ec, D), lambda t, c: (c, 0)), pl.BlockSpec((D, ec), lambda t, c: (0, c)),
                  rows, rows, whole, whole, vec, vec],
        out_specs=pl.BlockSpec((tb, D), lambda t, c: (t, 0)),
        scratch_shapes=[pltpu.VMEM((D, tb), F32), pltpu.VMEM((ec, tb), F32), pltpu.VMEM((ec, tb), BF16)],
        compiler_params=pltpu.CompilerParams(dimension_semantics=("parallel", "arbitrary"),
                                             vmem_limit_bytes=VMEM_LIMIT_BYTES),
        name="peer",
        interpret=interpret,
    )(hb, h, ub, vtb, cnt, at, rk, bm, g2.reshape(1, D).astype(F32), b2.reshape(1, D).astype(F32))


def _tile(n, pref):
    t = min(pref, n)
    assert n % t == 0, (n, t)
    return t


def kernel(x, w_in, kv_norm_g, w_uk, w_uv, hg_lb_logits, hg_norm_g, w_out, ln1_g, ln1_b,
           peer_w_q, peer_sub_keys, peer_u, peer_v, ln2_g, ln2_b):
    B, L, D = x.shape
    depth = w_in.shape[0]
    alpha = (2.0 * depth) ** 0.25
    T = B * L
    for layer in range(depth):
        (c, ct, kidx, qidxt, widxt, qlatt, hq, hf, hi, hg) = _proj(
            x, w_in[layer], kv_norm_g[layer], w_uk[layer], tb=_tile(L, 512))
        tq = _tile(L, 256)
        att = _dsa(kidx, c, ct, qidxt, widxt, qlatt, w_uv[layer], tq=tq, ks=min(128, tq), kc=tq)
        hgo = _hgrn(hq, hf, hi, hg, hg_lb_logits, hg_norm_g[layer], ts=_tile(L, 512), layer=layer)
        h, hb, rcnt, rat, rrk, rbm = _route(
            att.reshape(T, ATT_Q), hgo.reshape(T, HG_W), x.reshape(T, D), w_out[layer], ln1_g[layer], ln1_b[layer],
            peer_w_q[layer], peer_sub_keys[layer], tb=_tile(T, 512), alpha=alpha)
        y = _peer(hb, h, peer_u[layer], peer_v[layer], rcnt, rat, rrk, rbm, ln2_g[layer], ln2_b[layer],
                  tb=_tile(T, 512), ic=8, tw=256, alpha=alpha)
        x = y.reshape(B, L, D)
    return x
```

```python
import functools

import jax
import jax.numpy as jnp
from jax import lax
from jax.experimental import pallas as pl
from jax.experimental.pallas import tpu as pltpu

F32 = jnp.float32
BF16 = jnp.bfloat16
I32 = jnp.int32
I16 = jnp.int16

ATT_HEADS = 8
ATT_HEAD_DIM = 64
KV_LATENT = 128
IDX_HEADS = 8
IDX_DIM = 64
TOPK_MAX = 256
HG_HEADS = 8
HG_DIM = 64
HG_CHUNK = 32
PEER_HEADS = 8
PEER_N_KEYS = 128
PEER_TOPK = 16
EPS = 1e-5

ATT_Q = ATT_HEADS * ATT_HEAD_DIM
IDX_Q = IDX_HEADS * IDX_DIM
HG_W = HG_HEADS * HG_DIM

LANES = 128
VMEM_LIMIT_BYTES = 56 * 1024 * 1024

LOG2E = 1.4426950408889634
NEG_BIG = -1e30
INT_MIN = -(2 ** 31)
HALF16 = 1 << 15
BF16_ROWS = 16
KEY_NEG_INF = INT_MIN + 0x7FFFFF


def _dot(a, b):
    return jnp.dot(a, b, preferred_element_type=F32)


def _dot_nt(a, b):
    return lax.dot_general(a, b, (((1,), (1,)), ((), ())), preferred_element_type=F32)


def _dot_exact(a, b):
    return jnp.dot(a, b, preferred_element_type=F32, precision=lax.Precision.HIGHEST)


def _proj_kernel(x_ref, wc_ref, wk_ref, whq_ref, whf_ref, whi_ref, whg_ref,
                 wtq_ref, wtqi_ref, wtwi_ref, wtc_ref, grow_ref, gcol_ref, wuk_ref,
                 c_ref, ct_ref, kidx_ref, qidxt_ref, widxt_ref, qlatt_ref,
                 hq_ref, hf_ref, hi_ref, hg_ref):
    xb = x_ref[0].astype(BF16)

    c = _dot(xb, wc_ref[...])
    c = c * lax.rsqrt(jnp.mean(c * c, axis=-1, keepdims=True) + EPS) * grow_ref[...]
    c_ref[0] = c.astype(BF16)
    ct = _dot_nt(wtc_ref[...], xb)
    ct = ct * lax.rsqrt(jnp.mean(ct * ct, axis=0, keepdims=True) + EPS) * gcol_ref[...]
    ct_ref[0] = ct.astype(BF16)

    kidx_ref[0] = _dot(xb, wk_ref[...]).astype(BF16)
    qidxt_ref[0] = _dot_nt(wtqi_ref[...], xb).astype(BF16)
    widxt_ref[0] = _dot_nt(wtwi_ref[...], xb)

    qt = _dot_nt(wtq_ref[...], xb).astype(BF16)
    scale = ATT_HEAD_DIM ** -0.5 * LOG2E
    for h in range(ATT_HEADS):
        ql = _dot(wuk_ref[h], qt[h * ATT_HEAD_DIM:(h + 1) * ATT_HEAD_DIM, :]) * scale
        qlatt_ref[0, h * KV_LATENT:(h + 1) * KV_LATENT, :] = ql.astype(BF16)

    hq_ref[0] = _dot(xb, whq_ref[...])
    hf_ref[0] = _dot(xb, whf_ref[...])
    hi_ref[0] = _dot(xb, whi_ref[...])
    hg_ref[0] = _dot(xb, whg_ref[...])


def _proj(x, w, kv_g, w_uk, *, tb, interpret=False):
    B, L, D = x.shape
    o = 0
    cols = {}
    for name, width in (("q", ATT_Q), ("c", KV_LATENT), ("qi", IDX_Q), ("k", IDX_DIM), ("wi", IDX_HEADS),
                        ("hq", HG_W), ("hf", HG_W), ("hi", HG_W), ("hg", HG_W)):
        cols[name] = w[:, o:o + width].astype(BF16)
        o += width
    assert o == w.shape[1]
    wt = lambda n: cols[n].T
    ins = [x, cols["c"], cols["k"], cols["hq"], cols["hf"], cols["hi"], cols["hg"],
           wt("q"), wt("qi"), wt("wi"), wt("c"),
           kv_g.reshape(1, KV_LATENT).astype(F32), kv_g.reshape(KV_LATENT, 1).astype(F32),
           w_uk.astype(BF16)]

    def full(a):
        zeros = (0,) * a.ndim
        return pl.BlockSpec(a.shape, lambda b, t: zeros)

    in_specs = [pl.BlockSpec((1, tb, D), lambda b, t: (b, t, 0))] + [full(a) for a in ins[1:]]
    nat = lambda wd, dt: (jax.ShapeDtypeStruct((B, L, wd), dt), pl.BlockSpec((1, tb, wd), lambda b, t: (b, t, 0)))
    tr = lambda wd, dt: (jax.ShapeDtypeStruct((B, wd, L), dt), pl.BlockSpec((1, wd, tb), lambda b, t: (b, 0, t)))
    outs = [nat(KV_LATENT, BF16), tr(KV_LATENT, BF16), nat(IDX_DIM, BF16), tr(IDX_Q, BF16), tr(IDX_HEADS, F32),
            tr(ATT_HEADS * KV_LATENT, BF16), nat(HG_W, F32), nat(HG_W, F32), nat(HG_W, F32), nat(HG_W, F32)]
    return pl.pallas_call(
        _proj_kernel,
        out_shape=[s for s, _ in outs],
        grid=(B, L // tb),
        in_specs=in_specs,
        out_specs=[s for _, s in outs],
        compiler_params=pltpu.CompilerParams(dimension_semantics=("parallel", "parallel"),
                                             vmem_limit_bytes=VMEM_LIMIT_BYTES),
        name="proj",
        interpret=interpret,
    )(*ins)


def _dsa_kernel(kidx_ref, c_ref, ct_ref, qidxt_ref, widxt_ref, qlatt_ref, wuvt_ref,
                out_ref, keys_ref, khi_ref, klo_ref, acc_ref, m_ref, l_ref, alpha_ref, jb_ref, ot_ref, a_ref, p_ref,
                *, tq, ks, kc, top_k, seq):
    qi = pl.program_id(1)
    n_keys = (qi + 1) * tq
    nks = (qi + 1) * (tq // ks)
    nkc = (qi + 1) * (tq // kc)
    q_pos = lambda rows: qi * tq + lax.broadcasted_iota(I32, (rows, tq), 1)
    k_off = lambda rows: lax.broadcasted_iota(I32, (rows, tq), 0)

    def score_chunk(k, carry):
        r0 = pl.multiple_of(k * ks, ks)
        kk = kidx_ref[0, pl.ds(r0, ks), :]
        s = jnp.zeros((ks, tq), F32)
        for h in range(IDX_HEADS):
            z = _dot(kk, qidxt_ref[0, h * IDX_DIM:(h + 1) * IDX_DIM, :])
            s = s + widxt_ref[0, h:h + 1, :] * jnp.maximum(z, 0.0)
        s = jnp.where(r0 + k_off(ks) <= q_pos(ks), s, -jnp.inf)
        bits = pltpu.bitcast(s, I32)
        key = bits ^ ((bits >> 31) & 0x7FFFFFFF)
        keys_ref[pl.ds(r0, ks), :] = key
        khi_ref[pl.ds(r0, ks), :] = (key >> 16).astype(I16)
        klo_ref[pl.ds(r0, ks), :] = ((key & 0xFFFF) - HALF16).astype(I16)
        return carry

    lax.fori_loop(0, nks, score_chunk, 0)

    kcc = 2 * kc
    nkcc = (nkc + 1) >> 1

    @pl.when(n_keys < seq)
    def _():
        khi_ref[pl.ds(pl.multiple_of(n_keys, kc), kc), :] = jnp.full((kc, tq), -HALF16, I16)

    def count(pred):
        def body(k, cnt):
            r0 = pl.multiple_of(k * kc, kc)
            blk = keys_ref[pl.ds(r0, kc), :]
            return cnt + jnp.sum(jnp.where(pred(blk, r0), 1, 0).astype(I32), axis=0, keepdims=True)
        return lax.fori_loop(0, nkc, body, jnp.zeros((1, tq), I32))

    def count16(ref, cand):
        c16 = cand.astype(I16)

        def body(k, cnt):
            r0 = pl.multiple_of(k * kcc, kcc)
            blk = ref[pl.ds(r0, kcc), :]
            ones = jnp.where(blk >= c16, jnp.ones(blk.shape, BF16), jnp.zeros(blk.shape, BF16))
            parts = [ones[g * BF16_ROWS:(g + 1) * BF16_ROWS] for g in range(kcc // BF16_ROWS)]
            while len(parts) > 1:
                parts = [parts[i] + parts[i + 1] for i in range(0, len(parts), 2)]
            return cnt + parts[0].astype(F32)
        cnt = lax.fori_loop(0, nkcc, body, jnp.zeros((BF16_ROWS, tq), F32))
        return jnp.sum(cnt, axis=0, keepdims=True).astype(I32)

    def hi_step(i, carry):
        t, n_t = carry
        cand = t + lax.shift_left(jnp.int32(1), 15 - i)
        cnt = count16(khi_ref, cand)
        take = cnt >= top_k
        return jnp.where(take, cand, t), jnp.where(take, cnt, n_t)

    h, n_ge_h = lax.fori_loop(0, 16, hi_step,
                              (jnp.full((1, tq), -HALF16, I32), jnp.full((1, tq), n_keys, I32)))
    h16 = h.astype(I16)

    def mark_boundary(k, carry):
        rows = pl.ds(pl.multiple_of(k * kcc, kcc), kcc)
        hi = khi_ref[rows, :]
        klo_ref[rows, :] = jnp.where(hi > h16, jnp.int16(HALF16 - 1),
                                     jnp.where(hi == h16, klo_ref[rows, :], jnp.int16(-HALF16)))
        return carry

    lax.fori_loop(0, nkcc, mark_boundary, 0)

    def lo_bit(i, carry):
        t, n_t = carry
        cand = t + lax.shift_left(jnp.int32(1), 15 - i)
        cnt = count16(klo_ref, cand)
        take = cnt >= top_k
        return jnp.where(take, cand, t), jnp.where(take, cnt, n_t)

    def lo_cond(carry):
        r, _, _, open_lanes = carry
        return jnp.logical_and(r < 4, open_lanes > 0)

    def lo_round(carry):
        r, t, n_t, _ = carry
        t, n_t = lax.fori_loop(r * 4, r * 4 + 4, lo_bit, (t, n_t))
        return r + 1, t, n_t, jnp.max(jnp.abs(n_t - top_k))

    _, t_lo, n_ge, _ = lax.while_loop(
        lo_cond, lo_round,
        (jnp.int32(0), jnp.full((1, tq), -HALF16, I32), n_ge_h, jnp.max(jnp.abs(n_ge_h - top_k))))
    thr = lax.shift_left(h, 16) + (t_lo + HALF16)

    has_tie = jnp.where(n_ge > top_k, jnp.where(thr > KEY_NEG_INF, 1, 0), 0)
    any_tie = jnp.max(has_tie)
    jb_ref[...] = jnp.full((1, tq), -1, I32)

    @pl.when(any_tie > 0)
    def _():
        need = top_k - count(lambda blk, r0: blk > thr)

        def idx_step(i, v):
            cand = v + lax.shift_left(jnp.int32(1), (seq.bit_length() - 1) - i)
            cnt = count(lambda blk, r0: (blk == thr) & (r0 + k_off(kc) < cand))
            return jnp.where(cnt < need, cand, v)
        v = lax.fori_loop(0, seq.bit_length(), idx_step, jnp.zeros((1, tq), I32))
        jb_ref[...] = jnp.where(has_tie > 0, v, -1)

    jb = jb_ref[...]
    gt_thr = jnp.where(has_tie > 0, thr, jnp.maximum(thr - 1, KEY_NEG_INF))

    l_ref[...] = jnp.zeros(l_ref.shape, F32)
    acc_ref[...] = jnp.zeros(acc_ref.shape, F32)

    def masked_bias(r0):
        key = keys_ref[pl.ds(r0, kc), :]
        return lax.cond(
            any_tie > 0,
            lambda: jnp.where(key > gt_thr, 0.0,
                              jnp.where(key == thr, jnp.where(r0 + k_off(kc) <= jb, 0.0, NEG_BIG), NEG_BIG)),
            lambda: jnp.where(key > gt_thr, 0.0, NEG_BIG))

    def logits(h, cc, bias, m_prev, slot):
        a = _dot(cc, qlatt_ref[0, h * KV_LATENT:(h + 1) * KV_LATENT, :]) + bias
        a_ref[slot, h] = a
        m_new = jnp.maximum(m_prev, jnp.max(a, axis=0, keepdims=True))
        alpha_ref[slot, h] = jnp.exp2(m_prev - m_new)
        m_ref[slot, h] = m_new

    bias0 = masked_bias(0)
    cc0 = c_ref[0, pl.ds(0, kc), :]
    for h in range(ATT_HEADS):
        logits(h, cc0, bias0, jnp.full((1, tq), NEG_BIG, F32), 0)

    def att_step(k, cur):
        nxt = 1 - cur
        r0 = pl.multiple_of(k * kc, kc)
        rn = pl.multiple_of(jnp.minimum(k + 1, nkc - 1) * kc, kc)
        bias_n = masked_bias(rn)
        cc_n = c_ref[0, pl.ds(rn, kc), :]
        cct = ct_ref[0, :, pl.ds(r0, kc)]
        for h in range(ATT_HEADS):
            m_cur = m_ref[cur, h]
            logits(h, cc_n, bias_n, m_cur, nxt)
            p = jnp.exp2(a_ref[cur, h] - m_cur)
            l_ref[h] = alpha_ref[cur, h] * l_ref[h] + jnp.sum(p, axis=0, keepdims=True)
            p_ref[h] = p.astype(BF16)
            acc_ref[h] = alpha_ref[cur, h] * acc_ref[h] + _dot(cct, p_ref[h])

    def att_chunk(k, carry):
        for parity in range(2):
            pl.when(k % 2 == parity)(functools.partial(att_step, k, parity))
        return carry

    lax.fori_loop(0, nkc, att_chunk, 0)

    for h in range(ATT_HEADS):
        o_lat = (acc_ref[h] / l_ref[h]).astype(BF16)
        ot_ref[h * ATT_HEAD_DIM:(h + 1) * ATT_HEAD_DIM, :] = _dot(wuvt_ref[h], o_lat)
    out_ref[0] = ot_ref[...].T.astype(BF16)


def _dsa(kidx, c, ct, qidxt, widxt, qlatt, w_uv, *, tq, ks, kc, interpret=False):
    B, L, _ = c.shape
    top_k = min(TOPK_MAX, L // 4)
    assert tq % ks == 0 and tq % kc == 0 and top_k <= tq and L % (2 * kc) == 0 and 2 * kc // BF16_ROWS <= 256
    wuvt = jnp.swapaxes(w_uv, 1, 2).astype(BF16)
    kernel = functools.partial(_dsa_kernel, tq=tq, ks=ks, kc=kc, top_k=top_k, seq=L)
    return pl.pallas_call(
        kernel,
        out_shape=jax.ShapeDtypeStruct((B, L, ATT_Q), BF16),
        grid=(B, L // tq),
        in_specs=[
            pl.BlockSpec((1, L, IDX_DIM), lambda b, q: (b, 0, 0)),
            pl.BlockSpec((1, L, KV_LATENT), lambda b, q: (b, 0, 0)),
            pl.BlockSpec((1, KV_LATENT, L), lambda b, q: (b, 0, 0)),
            pl.BlockSpec((1, IDX_Q, tq), lambda b, q: (b, 0, q)),
            pl.BlockSpec((1, IDX_HEADS, tq), lambda b, q: (b, 0, q)),
            pl.BlockSpec((1, ATT_HEADS * KV_LATENT, tq), lambda b, q: (b, 0, q)),
            pl.BlockSpec((ATT_HEADS, ATT_HEAD_DIM, KV_LATENT), lambda b, q: (0, 0, 0)),
        ],
        out_specs=pl.BlockSpec((1, tq, ATT_Q), lambda b, q: (b, q, 0)),
        scratch_shapes=[
            pltpu.VMEM((L, tq), I32),
            pltpu.VMEM((L, tq), I16),
            pltpu.VMEM((L, tq), I16),
            pltpu.VMEM((ATT_HEADS, KV_LATENT, tq), F32),
            pltpu.VMEM((2, ATT_HEADS, 1, tq), F32),
            pltpu.VMEM((ATT_HEADS, 1, tq), F32),
            pltpu.VMEM((2, ATT_HEADS, 1, tq), F32),
            pltpu.VMEM((1, tq), I32),
            pltpu.VMEM((ATT_Q, tq), F32),
            pltpu.VMEM((2, ATT_HEADS, kc, tq), F32),
            pltpu.VMEM((ATT_HEADS, kc, tq), BF16),
        ],
        compiler_params=pltpu.CompilerParams(dimension_semantics=("parallel", "parallel"),
                                             vmem_limit_bytes=VMEM_LIMIT_BYTES),
        name="dsa",
        interpret=interpret,
    )(kidx, c, ct, qidxt, widxt, qlatt, wuvt)


HG_GROUP = 128
HG_PAIRS = HG_HEADS // 2


def _hgrn_kernel(lbl_ref, gn_ref, hq_ref, hf_ref, hi_ref, hg_ref, out_ref, st_ref, *, ts, layer):
    @pl.when(pl.program_id(1) == 0)
    def _():
        st_ref[...] = jnp.zeros(st_ref.shape, F32)

    logits = lbl_ref[...]
    e = jnp.exp(logits - jnp.max(logits, axis=0, keepdims=True))
    lb = jnp.sum(e[0:layer + 1, :], axis=0, keepdims=True) / jnp.sum(e, axis=0, keepdims=True)

    g = HG_GROUP
    chunk_shift = HG_CHUNK.bit_length() - 1
    head_shift = HG_DIM.bit_length() - 1
    r = lax.broadcasted_iota(I32, (g, g), 0)
    cidx = lax.broadcasted_iota(I32, (g, g), 1)
    same_chunk = (r >> chunk_shift) == (cidx >> chunk_shift)
    causal = same_chunk & (cidx <= r)
    tri = jnp.where(causal, 1.0, 0.0).astype(F32)
    ones_blk = jnp.where(same_chunk, 1.0, 0.0).astype(F32)
    same_head = (r >> head_shift) == (cidx >> head_shift)
    head_blk = jnp.where(same_head, 1.0, 0.0).astype(F32)
    lane_lo = lax.broadcasted_iota(I32, (g, LANES), 1) < HG_DIM
    col_chunk = cidx >> chunk_shift
    gn = gn_ref[...]

    def group(gi, carry):
        r0 = pl.multiple_of(gi * g, g)
        rows = pl.ds(r0, g)
        f = lb + (1.0 - lb) * jax.nn.sigmoid(hf_ref[0, rows, :])
        logf = jnp.log(f)
        kgate = 1.0 - f
        b = _dot_exact(tri, logf)
        btot = _dot_exact(ones_blk, logf)
        q_t = jax.nn.silu(hq_ref[0, rows, :]) * jnp.exp(b)
        k_t = kgate * jnp.exp(-b)
        k_dec = kgate * jnp.exp(btot - b)
        decay = jnp.exp(btot)
        v = hi_ref[0, rows, :]
        gate = hg_ref[0, rows, :]
        for p in range(HG_PAIRS):
            ls = slice(p * LANES, (p + 1) * LANES)
            qp = q_t[:, ls]
            kp = k_t[:, ls].astype(BF16)
            vp = v[:, ls].astype(BF16)
            kdp = k_dec[:, ls].astype(BF16)
            q_lo = jnp.where(lane_lo, qp, 0.0).astype(BF16)
            q_hi = jnp.where(lane_lo, 0.0, qp).astype(BF16)
            a_lo = jnp.where(causal, _dot_nt(q_lo, kp), 0.0).astype(BF16)
            a_hi = jnp.where(causal, _dot_nt(q_hi, kp), 0.0).astype(BF16)
            o = jnp.where(lane_lo, _dot(a_lo, vp), _dot(a_hi, vp))
            st = st_ref[p]
            qpb = qp.astype(BF16)
            vt = v[:, ls].T
            inter = []
            for ch in range(g // HG_CHUNK):
                rs = slice(ch * HG_CHUNK, (ch + 1) * HG_CHUNK)
                inter.append(_dot_nt(qpb[rs], st.astype(BF16)))
                vt_ch = jnp.where(col_chunk == ch, vt, 0.0).astype(BF16)
                ds = _dot(vt_ch, kdp)
                st = st * decay[ch * HG_CHUNK:ch * HG_CHUNK + 1, ls] + jnp.where(same_head, ds, 0.0)
            st_ref[p] = st
            o = o + jnp.concatenate(inter, axis=0)
            ms = _dot_exact(o * o, head_blk) * (1.0 / HG_DIM)
            gt = gate[:, ls]
            on = o * lax.rsqrt(ms + EPS) * gn * (gt * jax.nn.sigmoid(gt))
            out_ref[0, rows, ls] = on.astype(BF16)
        return carry

    for gi in range(ts // g):
        group(gi, 0)


def _hgrn(hq, hf, hi, hg, lb_logits, g_norm, *, ts, layer, interpret=False):
    B, L, W = hq.shape
    gn = jnp.tile(g_norm.reshape(1, HG_DIM).astype(F32), (1, 2))
    blk = pl.BlockSpec((1, ts, W), lambda b, t: (b, t, 0))
    return pl.pallas_call(
        functools.partial(_hgrn_kernel, ts=ts, layer=layer),
        out_shape=jax.ShapeDtypeStruct((B, L, W), BF16),
        grid=(B, L // ts),
        in_specs=[pl.BlockSpec(lb_logits.shape, lambda b, t: (0, 0)),
                  pl.BlockSpec((1, LANES), lambda b, t: (0, 0)), blk, blk, blk, blk],
        out_specs=blk,
        scratch_shapes=[pltpu.VMEM((HG_PAIRS, LANES, LANES), F32)],
        compiler_params=pltpu.CompilerParams(dimension_semantics=("parallel", "arbitrary"),
                                             vmem_limit_bytes=VMEM_LIMIT_BYTES),
        name="hgrn2",
        interpret=interpret,
    )(lb_logits.astype(F32), gn, hq, hf, hi, hg)


def _layer_norm(y, g, b):
    mu = jnp.mean(y, axis=-1, keepdims=True)
    d = y - mu
    var = jnp.mean(d * d, axis=-1, keepdims=True)
    return d * lax.rsqrt(var + EPS) * g + b


def _top_desc(s, n, with_rank=False):
    vals = []
    cur = s
    rank = jnp.full(s.shape, float(n), F32) if with_rank else None
    for i in range(n):
        mx = jnp.max(cur, axis=0, keepdims=True)
        vals.append(mx)
        eq = cur == mx
        if with_rank:
            rank = jnp.where(eq, float(i), rank)
        if i + 1 < n:
            cur = jnp.where(eq, -jnp.inf, cur)
    return (vals, rank) if with_rank else vals


PEER_KEEP = PEER_TOPK + 1
_PAIR_RANKS = [(i, j) for i in range(1, PEER_KEEP + 1) for j in range(1, PEER_KEEP + 1) if i * j <= PEER_KEEP]


def _route_kernel(att_ref, hg_ref, x_ref, woa_ref, wob_ref, g1_ref, b1_ref, wq_ref, sk_ref,
                  h_ref, hb_ref, cnt_ref, at_ref, rk_ref, bm_ref, *, alpha):
    mix = _dot(att_ref[...], woa_ref[...]) + _dot(hg_ref[...], wob_ref[...])
    h = _layer_norm(alpha * x_ref[...] + mix, g1_ref[...], b1_ref[...])
    h_ref[...] = h
    hb = h.astype(BF16)
    hb_ref[...] = hb
    qp = _dot(hb, wq_ref[...]).astype(BF16)
    dk = PEER_N_KEYS
    for hd in range(PEER_HEADS):
        s1 = _dot_nt(sk_ref[hd, 0], qp[:, (2 * hd) * dk:(2 * hd + 1) * dk])
        s2 = _dot_nt(sk_ref[hd, 1], qp[:, (2 * hd + 1) * dk:(2 * hd + 2) * dk])
        a = _top_desc(s1, PEER_KEEP)
        bb, rank2 = _top_desc(s2, PEER_KEEP, with_rank=True)
        cands = [a[i - 1] + bb[j - 1] for i, j in _PAIR_RANKS]
        pad = (-len(cands)) % 8
        cand = jnp.concatenate(cands + [jnp.full_like(cands[0], -jnp.inf)] * pad, axis=0)
        top = _top_desc(cand, PEER_KEEP)
        c_max, c_k, c_next = top[0], top[PEER_TOPK - 1], top[PEER_TOPK]
        thr = 0.5 * (c_k + c_next)
        z = jnp.sum(jnp.where(cand >= c_k, jnp.exp(cand - c_max), 0.0), axis=0, keepdims=True)
        cnt = jnp.zeros(s1.shape, F32)
        for r in range(PEER_KEEP):
            cnt = cnt + jnp.where(s1 >= thr - bb[r], 1.0, 0.0)
        cnt_ref[hd] = cnt
        at_ref[hd] = jnp.exp(s1 - a[0]) / z
        rk_ref[hd] = rank2.astype(BF16)
        bm_ref[hd] = jnp.exp(s2 - bb[0]).astype(BF16)


def _route(att, hgo, x2, w_out, g1, b1, w_q, sub_keys, *, tb, alpha, interpret=False):
    T, D = x2.shape
    woa = w_out[:ATT_Q].astype(BF16)
    wob = w_out[ATT_Q:].astype(BF16)

    def full(a):
        zeros = (0,) * a.ndim
        return pl.BlockSpec(a.shape, lambda t: zeros)

    ins = [att, hgo, x2, woa, wob, g1.reshape(1, D).astype(F32), b1.reshape(1, D).astype(F32),
           w_q.astype(BF16), sub_keys.astype(BF16)]
    in_specs = [pl.BlockSpec((tb, ATT_Q), lambda t: (t, 0)), pl.BlockSpec((tb, HG_W), lambda t: (t, 0)),
                pl.BlockSpec((tb, D), lambda t: (t, 0))] + [full(a) for a in ins[3:]]
    rt_shape = lambda dt: jax.ShapeDtypeStruct((PEER_HEADS, PEER_N_KEYS, T), dt)
    rt_spec = pl.BlockSpec((PEER_HEADS, PEER_N_KEYS, tb), lambda t: (0, 0, t))
    return pl.pallas_call(
        functools.partial(_route_kernel, alpha=alpha),
        out_shape=[jax.ShapeDtypeStruct((T, D), F32), jax.ShapeDtypeStruct((T, D), BF16),
                   rt_shape(F32), rt_shape(F32), rt_shape(BF16), rt_shape(BF16)],
        grid=(T // tb,),
        in_specs=in_specs,
        out_specs=[pl.BlockSpec((tb, D), lambda t: (t, 0)), pl.BlockSpec((tb, D), lambda t: (t, 0)),
                   rt_spec, rt_spec, rt_spec, rt_spec],
        compiler_params=pltpu.CompilerParams(dimension_semantics=("parallel",),
                                             vmem_limit_bytes=VMEM_LIMIT_BYTES),
        name="route",
        interpret=interpret,
    )(*ins)


GELU_K = 2.0 * (2.0 / 3.141592653589793) ** 0.5


def _gelu_tanh(x):
    z = x * ((-GELU_K * LOG2E) + (-GELU_K * 0.044715 * LOG2E) * (x * x))
    return x / (1.0 + jnp.exp2(z))


def _peer_kernel(hb_ref, h_ref, u_ref, vt_ref, cnt_ref, at_ref, rk_ref, bm_ref, g2_ref, b2_ref,
                 out_ref, acc_ref, pre_ref, p_ref, *, ic, tw, alpha):
    c = pl.program_id(1)

    @pl.when(c == 0)
    def _():
        acc_ref[...] = jnp.zeros(acc_ref.shape, F32)

    nk = PEER_N_KEYS
    tb = hb_ref.shape[0]
    grp = nk // BF16_ROWS
    pre_ref[...] = _dot_nt(u_ref[...], hb_ref[...])
    for ii in range(ic):
        rows = slice(ii * nk, (ii + 1) * nk)
        for s in range(tb // tw):
            ls = slice(s * tw, (s + 1) * tw)
            gsum = jnp.zeros((grp, BF16_ROWS, tw), BF16)
            for hd in range(PEER_HEADS):
                cnt = jnp.broadcast_to(cnt_ref[hd, ii:ii + 1, ls], (BF16_ROWS, tw)).astype(BF16)
                wgt = jnp.broadcast_to(at_ref[hd, ii:ii + 1, ls], (BF16_ROWS, tw)).astype(BF16)
                rk = rk_ref[hd, :, ls].reshape(grp, BF16_ROWS, tw)
                bm = bm_ref[hd, :, ls].reshape(grp, BF16_ROWS, tw)
                gsum = gsum + jnp.where(rk < cnt[None], bm, jnp.zeros_like(bm)) * wgt[None]
            act = _gelu_tanh(pre_ref[rows, ls].astype(BF16))
            p_ref[rows, ls] = gsum.reshape(nk, tw) * act
    acc_ref[...] += _dot(vt_ref[...], p_ref[...])

    @pl.when(c == pl.num_programs(1) - 1)
    def _():
        y = alpha * h_ref[...] + acc_ref[...].T
        out_ref[...] = _layer_norm(y, g2_ref[...], b2_ref[...])


def _peer(hb, h, u, v, cnt, at, rk, bm, g2, b2, *, tb, ic, tw, alpha, interpret=False):
    T, D = h.shape
    n_exp = u.shape[0]
    ub = u.astype(BF16)
    vtb = v.astype(BF16).T
    ec = ic * PEER_N_KEYS
    rows = pl.BlockSpec((PEER_HEADS, ic, tb), lambda t, c: (0, c, t))
    whole = pl.BlockSpec((PEER_HEADS, PEER_N_KEYS, tb), lambda t, c: (0, 0, t))
    vec = pl.BlockSpec((1, D), lambda t, c: (0, 0))
    return pl.pallas_call(
        functools.partial(_peer_kernel, ic=ic, tw=tw, alpha=alpha),
        out_shape=jax.ShapeDtypeStruct((T, D), F32),
        grid=(T // tb, n_exp // ec),
        in_specs=[pl.BlockSpec((tb, D), lambda t, c: (t, 0)), pl.BlockSpec((tb, D), lambda t, c: (t, 0)),
                  pl.BlockSpec((ec, D), lambda t, c: (c, 0)), pl.BlockSpec((D, ec), lambda t, c: (0, c)),
                  rows, rows, whole, whole, vec, vec],
        out_specs=pl.BlockSpec((tb, D), lambda t, c: (t, 0)),
        scratch_shapes=[pltpu.VMEM((D, tb), F32), pltpu.VMEM((ec, tb), F32), pltpu.VMEM((ec, tb), BF16)],
        compiler_params=pltpu.CompilerParams(dimension_semantics=("parallel", "arbitrary"),
                                             vmem_limit_bytes=VMEM_LIMIT_BYTES),
        name="peer",
        interpret=interpret,
    )(hb, h, ub, vtb, cnt, at, rk, bm, g2.reshape(1, D).astype(F32), b2.reshape(1, D).astype(F32))


def _tile(n, pref):
    t = min(pref, n)
    assert n % t == 0, (n, t)
    return t


def kernel(x, w_in, kv_norm_g, w_uk, w_uv, hg_lb_logits, hg_norm_g, w_out, ln1_g, ln1_b,
           peer_w_q, peer_sub_keys, peer_u, peer_v, ln2_g, ln2_b):
    B, L, D = x.shape
    depth = w_in.shape[0]
    alpha = (2.0 * depth) ** 0.25
    T = B * L
    for layer in range(depth):
        (c, ct, kidx, qidxt, widxt, qlatt, hq, hf, hi, hg) = _proj(
            x, w_in[layer], kv_norm_g[layer], w_uk[layer], tb=_tile(L, 512))
        tq = _tile(L, 256)
        att = _dsa(kidx, c, ct, qidxt, widxt, qlatt, w_uv[layer], tq=tq, ks=min(128, tq), kc=tq)
        hgo = _hgrn(hq, hf, hi, hg, hg_lb_logits, hg_norm_g[layer], ts=_tile(L, 512), layer=layer)
        h, hb, rcnt, rat, rrk, rbm = _route(
            att.reshape(T, ATT_Q), hgo.reshape(T, HG_W), x.reshape(T, D), w_out[layer], ln1_g[layer], ln1_b[layer],
            peer_w_q[layer], peer_sub_keys[layer], tb=_tile(T, 512), alpha=alpha)
        y = _peer(hb, h, peer_u[layer], peer_v[layer], rcnt, rat, rrk, rbm, ln2_g[layer], ln2_b[layer],
                  tb=_tile(T, 512), ic=16, tw=256, alpha=alpha)
        x = y.reshape(B, L, D)
    return x
```

```python
import functools

import jax
import jax.numpy as jnp
from jax import lax
from jax.experimental import pallas as pl
from jax.experimental.pallas import tpu as pltpu

F32 = jnp.float32
BF16 = jnp.bfloat16
I32 = jnp.int32
I16 = jnp.int16

ATT_HEADS = 8
ATT_HEAD_DIM = 64
KV_LATENT = 128
IDX_HEADS = 8
IDX_DIM = 64
TOPK_MAX = 256
HG_HEADS = 8
HG_DIM = 64
HG_CHUNK = 32
PEER_HEADS = 8
PEER_N_KEYS = 128
PEER_TOPK = 16
EPS = 1e-5

ATT_Q = ATT_HEADS * ATT_HEAD_DIM
IDX_Q = IDX_HEADS * IDX_DIM
HG_W = HG_HEADS * HG_DIM

LANES = 128
VMEM_LIMIT_BYTES = 56 * 1024 * 1024

LOG2E = 1.4426950408889634
NEG_BIG = -1e30
INT_MIN = -(2 ** 31)
HALF16 = 1 << 15
BF16_ROWS = 16
KEY_NEG_INF = INT_MIN + 0x7FFFFF


def _dot(a, b):
    return jnp.dot(a, b, preferred_element_type=F32)


def _dot_nt(a, b):
    return lax.dot_general(a, b, (((1,), (1,)), ((), ())), preferred_element_type=F32)


def _dot_exact(a, b):
    return jnp.dot(a, b, preferred_element_type=F32, precision=lax.Precision.HIGHEST)


def _proj_kernel(x_ref, wc_ref, wk_ref, whq_ref, whf_ref, whi_ref, whg_ref,
                 wtq_ref, wtqi_ref, wtwi_ref, wtc_ref, grow_ref, gcol_ref, wuk_ref,
                 c_ref, ct_ref, kidx_ref, qidxt_ref, widxt_ref, qlatt_ref,
                 hq_ref, hf_ref, hi_ref, hg_ref):
    xb = x_ref[0].astype(BF16)

    c = _dot(xb, wc_ref[...])
    c = c * lax.rsqrt(jnp.mean(c * c, axis=-1, keepdims=True) + EPS) * grow_ref[...]
    c_ref[0] = c.astype(BF16)
    ct = _dot_nt(wtc_ref[...], xb)
    ct = ct * lax.rsqrt(jnp.mean(ct * ct, axis=0, keepdims=True) + EPS) * gcol_ref[...]
    ct_ref[0] = ct.astype(BF16)

    kidx_ref[0] = _dot(xb, wk_ref[...]).astype(BF16)
    qidxt_ref[0] = _dot_nt(wtqi_ref[...], xb).astype(BF16)
    widxt_ref[0] = _dot_nt(wtwi_ref[...], xb)

    qt = _dot_nt(wtq_ref[...], xb).astype(BF16)
    scale = ATT_HEAD_DIM ** -0.5 * LOG2E
    for h in range(ATT_HEADS):
        ql = _dot(wuk_ref[h], qt[h * ATT_HEAD_DIM:(h + 1) * ATT_HEAD_DIM, :]) * scale
        qlatt_ref[0, h * KV_LATENT:(h + 1) * KV_LATENT, :] = ql.astype(BF16)

    hq_ref[0] = _dot(xb, whq_ref[...])
    hf_ref[0] = _dot(xb, whf_ref[...])
    hi_ref[0] = _dot(xb, whi_ref[...])
    hg_ref[0] = _dot(xb, whg_ref[...])


def _proj(x, w, kv_g, w_uk, *, tb, interpret=False):
    B, L, D = x.shape
    o = 0
    cols = {}
    for name, width in (("q", ATT_Q), ("c", KV_LATENT), ("qi", IDX_Q), ("k", IDX_DIM), ("wi", IDX_HEADS),
                        ("hq", HG_W), ("hf", HG_W), ("hi", HG_W), ("hg", HG_W)):
        cols[name] = w[:, o:o + width].astype(BF16)
        o += width
    assert o == w.shape[1]
    wt = lambda n: cols[n].T
    ins = [x, cols["c"], cols["k"], cols["hq"], cols["hf"], cols["hi"], cols["hg"],
           wt("q"), wt("qi"), wt("wi"), wt("c"),
           kv_g.reshape(1, KV_LATENT).astype(F32), kv_g.reshape(KV_LATENT, 1).astype(F32),
           w_uk.astype(BF16)]

    def full(a):
        zeros = (0,) * a.ndim
        return pl.BlockSpec(a.shape, lambda b, t: zeros)

    in_specs = [pl.BlockSpec((1, tb, D), lambda b, t: (b, t, 0))] + [full(a) for a in ins[1:]]
    nat = lambda wd, dt: (jax.ShapeDtypeStruct((B, L, wd), dt), pl.BlockSpec((1, tb, wd), lambda b, t: (b, t, 0)))
    tr = lambda wd, dt: (jax.ShapeDtypeStruct((B, wd, L), dt), pl.BlockSpec((1, wd, tb), lambda b, t: (b, 0, t)))
    outs = [nat(KV_LATENT, BF16), tr(KV_LATENT, BF16), nat(IDX_DIM, BF16), tr(IDX_Q, BF16), tr(IDX_HEADS, F32),
            tr(ATT_HEADS * KV_LATENT, BF16), nat(HG_W, F32), nat(HG_W, F32), nat(HG_W, F32), nat(HG_W, F32)]
    return pl.pallas_call(
        _proj_kernel,
        out_shape=[s for s, _ in outs],
        grid=(B, L // tb),
        in_specs=in_specs,
        out_specs=[s for _, s in outs],
        compiler_params=pltpu.CompilerParams(dimension_semantics=("parallel", "parallel"),
                                             vmem_limit_bytes=VMEM_LIMIT_BYTES),
        name="proj",
        interpret=interpret,
    )(*ins)


def _dsa_kernel(kidx_ref, c_ref, ct_ref, qidxt_ref, widxt_ref, qlatt_ref, wuvt_ref,
                out_ref, keys_ref, khi_ref, klo_ref, acc_ref, m_ref, l_ref, alpha_ref, jb_ref, ot_ref, a_ref, p_ref,
                *, tq, ks, kc, top_k, seq):
    qi = pl.program_id(1)
    n_keys = (qi + 1) * tq
    nks = (qi + 1) * (tq // ks)
    nkc = (qi + 1) * (tq // kc)
    q_pos = lambda rows: qi * tq + lax.broadcasted_iota(I32, (rows, tq), 1)
    k_off = lambda rows: lax.broadcasted_iota(I32, (rows, tq), 0)

    def score_chunk(k, carry):
        r0 = pl.multiple_of(k * ks, ks)
        kk = kidx_ref[0, pl.ds(r0, ks), :]
        s = jnp.zeros((ks, tq), F32)
        for h in range(IDX_HEADS):
            z = _dot(kk, qidxt_ref[0, h * IDX_DIM:(h + 1) * IDX_DIM, :])
            s = s + widxt_ref[0, h:h + 1, :] * jnp.maximum(z, 0.0)
        s = jnp.where(r0 + k_off(ks) <= q_pos(ks), s, -jnp.inf)
        bits = pltpu.bitcast(s, I32)
        key = bits ^ ((bits >> 31) & 0x7FFFFFFF)
        keys_ref[pl.ds(r0, ks), :] = key
        khi_ref[pl.ds(r0, ks), :] = (key >> 16).astype(I16)
        klo_ref[pl.ds(r0, ks), :] = ((key & 0xFFFF) - HALF16).astype(I16)
        return carry

    lax.fori_loop(0, nks, score_chunk, 0)

    kcc = 2 * kc
    nkcc = (nkc + 1) >> 1

    @pl.when(n_keys < seq)
    def _():
        khi_ref[pl.ds(pl.multiple_of(n_keys, kc), kc), :] = jnp.full((kc, tq), -HALF16, I16)

    def count(pred):
        def body(k, cnt):
            r0 = pl.multiple_of(k * kc, kc)
            blk = keys_ref[pl.ds(r0, kc), :]
            return cnt + jnp.sum(jnp.where(pred(blk, r0), 1, 0).astype(I32), axis=0, keepdims=True)
        return lax.fori_loop(0, nkc, body, jnp.zeros((1, tq), I32))

    def count16(ref, cand):
        c16 = cand.astype(I16)

        def body(k, cnt):
            r0 = pl.multiple_of(k * kcc, kcc)
            blk = ref[pl.ds(r0, kcc), :]
            ones = jnp.where(blk >= c16, jnp.ones(blk.shape, BF16), jnp.zeros(blk.shape, BF16))
            parts = [ones[g * BF16_ROWS:(g + 1) * BF16_ROWS] for g in range(kcc // BF16_ROWS)]
            while len(parts) > 1:
                parts = [parts[i] + parts[i + 1] for i in range(0, len(parts), 2)]
            return cnt + parts[0].astype(F32)
        cnt = lax.fori_loop(0, nkcc, body, jnp.zeros((BF16_ROWS, tq), F32))
        return jnp.sum(cnt, axis=0, keepdims=True).astype(I32)

    def hi_step(i, carry):
        t, n_t = carry
        cand = t + lax.shift_left(jnp.int32(1), 15 - i)
        cnt = count16(khi_ref, cand)
        take = cnt >= top_k
        return jnp.where(take, cand, t), jnp.where(take, cnt, n_t)

    h, n_ge_h = lax.fori_loop(0, 16, hi_step,
                              (jnp.full((1, tq), -HALF16, I32), jnp.full((1, tq), n_keys, I32)))
    h16 = h.astype(I16)

    def mark_boundary(k, carry):
        rows = pl.ds(pl.multiple_of(k * kcc, kcc), kcc)
        hi = khi_ref[rows, :]
        klo_ref[rows, :] = jnp.where(hi > h16, jnp.int16(HALF16 - 1),
                                     jnp.where(hi == h16, klo_ref[rows, :], jnp.int16(-HALF16)))
        return carry

    lax.fori_loop(0, nkcc, mark_boundary, 0)

    def lo_bit(i, carry):
        t, n_t = carry
        cand = t + lax.shift_left(jnp.int32(1), 15 - i)
        cnt = count16(klo_ref, cand)
        take = cnt >= top_k
        return jnp.where(take, cand, t), jnp.where(take, cnt, n_t)

    def lo_cond(carry):
        r, _, _, open_lanes = carry
        return jnp.logical_and(r < 4, open_lanes > 0)

    def lo_round(carry):
        r, t, n_t, _ = carry
        t, n_t = lax.fori_loop(r * 4, r * 4 + 4, lo_bit, (t, n_t))
        return r + 1, t, n_t, jnp.max(jnp.abs(n_t - top_k))

    _, t_lo, n_ge, _ = lax.while_loop(
        lo_cond, lo_round,
        (jnp.int32(0), jnp.full((1, tq), -HALF16, I32), n_ge_h, jnp.max(jnp.abs(n_ge_h - top_k))))
    thr = lax.shift_left(h, 16) + (t_lo + HALF16)

    has_tie = jnp.where(n_ge > top_k, jnp.where(thr > KEY_NEG_INF, 1, 0), 0)
    any_tie = jnp.max(has_tie)
    jb_ref[...] = jnp.full((1, tq), -1, I32)

    @pl.when(any_tie > 0)
    def _():
        need = top_k - count(lambda blk, r0: blk > thr)

        def idx_step(i, v):
            cand = v + lax.shift_left(jnp.int32(1), (seq.bit_length() - 1) - i)
            cnt = count(lambda blk, r0: (blk == thr) & (r0 + k_off(kc) < cand))
            return jnp.where(cnt < need, cand, v)
        v = lax.fori_loop(0, seq.bit_length(), idx_step, jnp.zeros((1, tq), I32))
        jb_ref[...] = jnp.where(has_tie > 0, v, -1)

    jb = jb_ref[...]
    gt_thr = jnp.where(has_tie > 0, thr, jnp.maximum(thr - 1, KEY_NEG_INF))

    l_ref[...] = jnp.zeros(l_ref.shape, F32)
    acc_ref[...] = jnp.zeros(acc_ref.shape, F32)

    def masked_bias(r0):
        key = keys_ref[pl.ds(r0, kc), :]
        return lax.cond(
            any_tie > 0,
            lambda: jnp.where(key > gt_thr, 0.0,
                              jnp.where(key == thr, jnp.where(r0 + k_off(kc) <= jb, 0.0, NEG_BIG), NEG_BIG)),
            lambda: jnp.where(key > gt_thr, 0.0, NEG_BIG))

    def logits(h, cc, bias, m_prev, slot):
        a = _dot(cc, qlatt_ref[0, h * KV_LATENT:(h + 1) * KV_LATENT, :]) + bias
        a_ref[slot, h] = a
        m_new = jnp.maximum(m_prev, jnp.max(a, axis=0, keepdims=True))
        alpha_ref[slot, h] = jnp.exp2(m_prev - m_new)
        m_ref[slot, h] = m_new

    bias0 = masked_bias(0)
    cc0 = c_ref[0, pl.ds(0, kc), :]
    for h in range(ATT_HEADS):
        logits(h, cc0, bias0, jnp.full((1, tq), NEG_BIG, F32), 0)

    def att_step(k, cur):
        nxt = 1 - cur
        r0 = pl.multiple_of(k * kc, kc)
        rn = pl.multiple_of(jnp.minimum(k + 1, nkc - 1) * kc, kc)
        bias_n = masked_bias(rn)
        cc_n = c_ref[0, pl.ds(rn, kc), :]
        cct = ct_ref[0, :, pl.ds(r0, kc)]
        for h in range(ATT_HEADS):
            m_cur = m_ref[cur, h]
            logits(h, cc_n, bias_n, m_cur, nxt)
            p = jnp.exp2(a_ref[cur, h] - m_cur)
            l_ref[h] = alpha_ref[cur, h] * l_ref[h] + jnp.sum(p, axis=0, keepdims=True)
            p_ref[h] = p.astype(BF16)
            acc_ref[h] = alpha_ref[cur, h] * acc_ref[h] + _dot(cct, p_ref[h])

    def att_chunk(k, carry):
        for parity in range(2):
            pl.when(k % 2 == parity)(functools.partial(att_step, k, parity))
        return carry

    lax.fori_loop(0, nkc, att_chunk, 0)

    for h in range(ATT_HEADS):
        o_lat = (acc_ref[h] / l_ref[h]).astype(BF16)
        ot_ref[h * ATT_HEAD_DIM:(h + 1) * ATT_HEAD_DIM, :] = _dot(wuvt_ref[h], o_lat)
    out_ref[0] = ot_ref[...].T.astype(BF16)


def _dsa(kidx, c, ct, qidxt, widxt, qlatt, w_uv, *, tq, ks, kc, interpret=False):
    B, L, _ = c.shape
    top_k = min(TOPK_MAX, L // 4)
    assert tq % ks == 0 and tq % kc == 0 and top_k <= tq and L % (2 * kc) == 0 and 2 * kc // BF16_ROWS <= 256
    wuvt = jnp.swapaxes(w_uv, 1, 2).astype(BF16)
    kernel = functools.partial(_dsa_kernel, tq=tq, ks=ks, kc=kc, top_k=top_k, seq=L)
    return pl.pallas_call(
        kernel,
        out_shape=jax.ShapeDtypeStruct((B, L, ATT_Q), BF16),
        grid=(B, L // tq),
        in_specs=[
            pl.BlockSpec((1, L, IDX_DIM), lambda b, q: (b, 0, 0)),
            pl.BlockSpec((1, L, KV_LATENT), lambda b, q: (b, 0, 0)),
            pl.BlockSpec((1, KV_LATENT, L), lambda b, q: (b, 0, 0)),
            pl.BlockSpec((1, IDX_Q, tq), lambda b, q: (b, 0, q)),
            pl.BlockSpec((1, IDX_HEADS, tq), lambda b, q: (b, 0, q)),
            pl.BlockSpec((1, ATT_HEADS * KV_LATENT, tq), lambda b, q: (b, 0, q)),
            pl.BlockSpec((ATT_HEADS, ATT_HEAD_DIM, KV_LATENT), lambda b, q: (0, 0, 0)),
        ],
        out_specs=pl.BlockSpec((1, tq, ATT_Q), lambda b, q: (b, q, 0)),
        scratch_shapes=[
            pltpu.VMEM((L, tq), I32),
            pltpu.VMEM((L, tq), I16),
            pltpu.VMEM((L, tq), I16),
            pltpu.VMEM((ATT_HEADS, KV_LATENT, tq), F32),
            pltpu.VMEM((2, ATT_HEADS, 1, tq), F32),
            pltpu.VMEM((ATT_HEADS, 1, tq), F32),
            pltpu.VMEM((2, ATT_HEADS, 1, tq), F32),
            pltpu.VMEM((1, tq), I32),
            pltpu.VMEM((ATT_Q, tq), F32),
            pltpu.VMEM((2, ATT_HEADS, kc, tq), F32),
            pltpu.VMEM((ATT_HEADS, kc, tq), BF16),
        ],
        compiler_params=pltpu.CompilerParams(dimension_semantics=("parallel", "parallel"),
                                             vmem_limit_bytes=VMEM_LIMIT_BYTES),
        name="dsa",
        interpret=interpret,
    )(kidx, c, ct, qidxt, widxt, qlatt, wuvt)


HG_GROUP = 128
HG_PAIRS = HG_HEADS // 2


def _hgrn_kernel(lbl_ref, gn_ref, hq_ref, hf_ref, hi_ref, hg_ref, out_ref, st_ref, *, ts, layer):
    @pl.when(pl.program_id(1) == 0)
    def _():
        st_ref[...] = jnp.zeros(st_ref.shape, F32)

    logits = lbl_ref[...]
    e = jnp.exp(logits - jnp.max(logits, axis=0, keepdims=True))
    lb = jnp.sum(e[0:layer + 1, :], axis=0, keepdims=True) / jnp.sum(e, axis=0, keepdims=True)

    g = HG_GROUP
    chunk_shift = HG_CHUNK.bit_length() - 1
    head_shift = HG_DIM.bit_length() - 1
    r = lax.broadcasted_iota(I32, (g, g), 0)
    cidx = lax.broadcasted_iota(I32, (g, g), 1)
    same_chunk = (r >> chunk_shift) == (cidx >> chunk_shift)
    causal = same_chunk & (cidx <= r)
    tri = jnp.where(causal, 1.0, 0.0).astype(F32)
    ones_blk = jnp.where(same_chunk, 1.0, 0.0).astype(F32)
    same_head = (r >> head_shift) == (cidx >> head_shift)
    head_blk = jnp.where(same_head, 1.0, 0.0).astype(F32)
    lane_lo = lax.broadcasted_iota(I32, (g, LANES), 1) < HG_DIM
    col_chunk = cidx >> chunk_shift
    gn = gn_ref[...]

    def group(gi, carry):
        r0 = pl.multiple_of(gi * g, g)
        rows = pl.ds(r0, g)
        f = lb + (1.0 - lb) * jax.nn.sigmoid(hf_ref[0, rows, :])
        logf = jnp.log(f)
        kgate = 1.0 - f
        b = _dot_exact(tri, logf)
        btot = _dot_exact(ones_blk, logf)
        q_t = jax.nn.silu(hq_ref[0, rows, :]) * jnp.exp(b)
        k_t = kgate * jnp.exp(-b)
        k_dec = kgate * jnp.exp(btot - b)
        decay = jnp.exp(btot)
        v = hi_ref[0, rows, :]
        gate = hg_ref[0, rows, :]
        for p in range(HG_PAIRS):
            ls = slice(p * LANES, (p + 1) * LANES)
            qp = q_t[:, ls]
            kp = k_t[:, ls].astype(BF16)
            vp = v[:, ls].astype(BF16)
            kdp = k_dec[:, ls].astype(BF16)
            q_lo = jnp.where(lane_lo, qp, 0.0).astype(BF16)
            q_hi = jnp.where(lane_lo, 0.0, qp).astype(BF16)
            a_lo = jnp.where(causal, _dot_nt(q_lo, kp), 0.0).astype(BF16)
            a_hi = jnp.where(causal, _dot_nt(q_hi, kp), 0.0).astype(BF16)
            o = jnp.where(lane_lo, _dot(a_lo, vp), _dot(a_hi, vp))
            st = st_ref[p]
            qpb = qp.astype(BF16)
            vt = v[:, ls].T
            inter = []
            for ch in range(g // HG_CHUNK):
                rs = slice(ch * HG_CHUNK, (ch + 1) * HG_CHUNK)
                inter.append(_dot_nt(qpb[rs], st.astype(BF16)))
                vt_ch = jnp.where(col_chunk == ch, vt, 0.0).astype(BF16)
                ds = _dot(vt_ch, kdp)
                st = st * decay[ch * HG_CHUNK:ch * HG_CHUNK + 1, ls] + jnp.where(same_head, ds, 0.0)
            st_ref[p] = st
            o = o + jnp.concatenate(inter, axis=0)
            ms = _dot_exact(o * o, head_blk) * (1.0 / HG_DIM)
            gt = gate[:, ls]
            on = o * lax.rsqrt(ms + EPS) * gn * (gt * jax.nn.sigmoid(gt))
            out_ref[0, rows, ls] = on.astype(BF16)
        return carry

    for gi in range(ts // g):
        group(gi, 0)


def _hgrn(hq, hf, hi, hg, lb_logits, g_norm, *, ts, layer, interpret=False):
    B, L, W = hq.shape
    gn = jnp.tile(g_norm.reshape(1, HG_DIM).astype(F32), (1, 2))
    blk = pl.BlockSpec((1, ts, W), lambda b, t: (b, t, 0))
    return pl.pallas_call(
        functools.partial(_hgrn_kernel, ts=ts, layer=layer),
        out_shape=jax.ShapeDtypeStruct((B, L, W), BF16),
        grid=(B, L // ts),
        in_specs=[pl.BlockSpec(lb_logits.shape, lambda b, t: (0, 0)),
                  pl.BlockSpec((1, LANES), lambda b, t: (0, 0)), blk, blk, blk, blk],
        out_specs=blk,
        scratch_shapes=[pltpu.VMEM((HG_PAIRS, LANES, LANES), F32)],
        compiler_params=pltpu.CompilerParams(dimension_semantics=("parallel", "arbitrary"),
                                             vmem_limit_bytes=VMEM_LIMIT_BYTES),
        name="hgrn2",
        interpret=interpret,
    )(lb_logits.astype(F32), gn, hq, hf, hi, hg)


def _layer_norm(y, g, b):
    mu = jnp.mean(y, axis=-1, keepdims=True)
    d = y - mu
    var = jnp.mean(d * d, axis=-1, keepdims=True)
    return d * lax.rsqrt(var + EPS) * g + b


def _top_desc(s, n, with_rank=False):
    vals = []
    cur = s
    rank = jnp.full(s.shape, float(n), F32) if with_rank else None
    for i in range(n):
        mx = jnp.max(cur, axis=0, keepdims=True)
        vals.append(mx)
        eq = cur == mx
        if with_rank:
            rank = jnp.where(eq, float(i), rank)
        if i + 1 < n:
            cur = jnp.where(eq, -jnp.inf, cur)
    return (vals, rank) if with_rank else vals


PEER_KEEP = PEER_TOPK + 1
_PAIR_RANKS = [(i, j) for i in range(1, PEER_KEEP + 1) for j in range(1, PEER_KEEP + 1) if i * j <= PEER_KEEP]


def _route_kernel(att_ref, hg_ref, x_ref, woa_ref, wob_ref, g1_ref, b1_ref, wq_ref, sk_ref,
                  h_ref, hb_ref, cnt_ref, at_ref, rk_ref, bm_ref, *, alpha):
    mix = _dot(att_ref[...], woa_ref[...]) + _dot(hg_ref[...], wob_ref[...])
    h = _layer_norm(alpha * x_ref[...] + mix, g1_ref[...], b1_ref[...])
    h_ref[...] = h
    hb = h.astype(BF16)
    hb_ref[...] = hb
    qp = _dot(hb, wq_ref[...]).astype(BF16)
    dk = PEER_N_KEYS
    tb = h.shape[0]
    for hd in range(PEER_HEADS):
        s1_all = _dot_nt(sk_ref[hd, 0], qp[:, (2 * hd) * dk:(2 * hd + 1) * dk])
        s2_all = _dot_nt(sk_ref[hd, 1], qp[:, (2 * hd + 1) * dk:(2 * hd + 2) * dk])
        for sub in range(tb // LANES):
            ls = slice(sub * LANES, (sub + 1) * LANES)
            s1, s2 = s1_all[:, ls], s2_all[:, ls]
            a = _top_desc(s1, PEER_KEEP)
            bb, rank2 = _top_desc(s2, PEER_KEEP, with_rank=True)
            cands = [a[i - 1] + bb[j - 1] for i, j in _PAIR_RANKS]
            pad = (-len(cands)) % 8
            cand = jnp.concatenate(cands + [jnp.full_like(cands[0], -jnp.inf)] * pad, axis=0)
            top = _top_desc(cand, PEER_KEEP)
            c_max, c_k, c_next = top[0], top[PEER_TOPK - 1], top[PEER_TOPK]
            thr = 0.5 * (c_k + c_next)
            z = jnp.sum(jnp.where(cand >= c_k, jnp.exp(cand - c_max), 0.0), axis=0, keepdims=True)
            cnt = jnp.zeros(s1.shape, F32)
            for r in range(PEER_KEEP):
                cnt = cnt + jnp.where(s1 >= thr - bb[r], 1.0, 0.0)
            cnt_ref[hd, :, ls] = cnt
            at_ref[hd, :, ls] = jnp.exp(s1 - a[0]) / z
            rk_ref[hd, :, ls] = rank2.astype(BF16)
            bm_ref[hd, :, ls] = jnp.exp(s2 - bb[0]).astype(BF16)


def _route(att, hgo, x2, w_out, g1, b1, w_q, sub_keys, *, tb, alpha, interpret=False):
    T, D = x2.shape
    woa = w_out[:ATT_Q].astype(BF16)
    wob = w_out[ATT_Q:].astype(BF16)

    def full(a):
        zeros = (0,) * a.ndim
        return pl.BlockSpec(a.shape, lambda t: zeros)

    ins = [att, hgo, x2, woa, wob, g1.reshape(1, D).astype(F32), b1.reshape(1, D).astype(F32),
           w_q.astype(BF16), sub_keys.astype(BF16)]
    in_specs = [pl.BlockSpec((tb, ATT_Q), lambda t: (t, 0)), pl.BlockSpec((tb, HG_W), lambda t: (t, 0)),
                pl.BlockSpec((tb, D), lambda t: (t, 0))] + [full(a) for a in ins[3:]]
    rt_shape = lambda dt: jax.ShapeDtypeStruct((PEER_HEADS, PEER_N_KEYS, T), dt)
    rt_spec = pl.BlockSpec((PEER_HEADS, PEER_N_KEYS, tb), lambda t: (0, 0, t))
    return pl.pallas_call(
        functools.partial(_route_kernel, alpha=alpha),
        out_shape=[jax.ShapeDtypeStruct((T, D), F32), jax.ShapeDtypeStruct((T, D), BF16),
                   rt_shape(F32), rt_shape(F32), rt_shape(BF16), rt_shape(BF16)],
        grid=(T // tb,),
        in_specs=in_specs,
        out_specs=[pl.BlockSpec((tb, D), lambda t: (t, 0)), pl.BlockSpec((tb, D), lambda t: (t, 0)),
                   rt_spec, rt_spec, rt_spec, rt_spec],
        compiler_params=pltpu.CompilerParams(dimension_semantics=("parallel",),
                                             vmem_limit_bytes=VMEM_LIMIT_BYTES),
        name="route",
        interpret=interpret,
    )(*ins)


GELU_K = 2.0 * (2.0 / 3.141592653589793) ** 0.5


def _gelu_tanh(x):
    z = x * ((-GELU_K * LOG2E) + (-GELU_K * 0.044715 * LOG2E) * (x * x))
    return x / (1.0 + jnp.exp2(z))


def _peer_kernel(hb_ref, h_ref, u_ref, vt_ref, cnt_ref, at_ref, rk_ref, bm_ref, g2_ref, b2_ref,
                 out_ref, acc_ref, pre_ref, p_ref, *, ic, tw, alpha):
    c = pl.program_id(1)

    @pl.when(c == 0)
    def _():
        acc_ref[...] = jnp.zeros(acc_ref.shape, F32)

    nk = PEER_N_KEYS
    tb = hb_ref.shape[0]
    grp = nk // BF16_ROWS
    pre_ref[...] = _dot_nt(u_ref[...], hb_ref[...])
    for ii in range(ic):
        rows = slice(ii * nk, (ii + 1) * nk)
        for s in range(tb // tw):
            ls = slice(s * tw, (s + 1) * tw)
            gsum = jnp.zeros((grp, BF16_ROWS, tw), BF16)
            for hd in range(PEER_HEADS):
                cnt = jnp.broadcast_to(cnt_ref[hd, ii:ii + 1, ls], (BF16_ROWS, tw)).astype(BF16)
                wgt = jnp.broadcast_to(at_ref[hd, ii:ii + 1, ls], (BF16_ROWS, tw)).astype(BF16)
                rk = rk_ref[hd, :, ls].reshape(grp, BF16_ROWS, tw)
                bm = bm_ref[hd, :, ls].reshape(grp, BF16_ROWS, tw)
                gsum = gsum + jnp.where(rk < cnt[None], bm, jnp.zeros_like(bm)) * wgt[None]
            act = _gelu_tanh(pre_ref[rows, ls].astype(BF16))
            p_ref[rows, ls] = gsum.reshape(nk, tw) * act
    acc_ref[...] += _dot(vt_ref[...], p_ref[...])

    @pl.when(c == pl.num_programs(1) - 1)
    def _():
        y = alpha * h_ref[...] + acc_ref[...].T
        out_ref[...] = _layer_norm(y, g2_ref[...], b2_ref[...])


def _peer(hb, h, u, v, cnt, at, rk, bm, g2, b2, *, tb, ic, tw, alpha, interpret=False):
    T, D = h.shape
    n_exp = u.shape[0]
    ub = u.astype(BF16)
    vtb = v.astype(BF16).T
    ec = ic * PEER_N_KEYS
    rows = pl.BlockSpec((PEER_HEADS, ic, tb), lambda t, c: (0, c, t))
    whole = pl.BlockSpec((PEER_HEADS, PEER_N_KEYS, tb), lambda t, c: (0, 0, t))
    vec = pl.BlockSpec((1, D), lambda t, c: (0, 0))
    return pl.pallas_call(
        functools.partial(_peer_kernel, ic=ic, tw=tw, alpha=alpha),
        out_shape=jax.ShapeDtypeStruct((T, D), F32),
        grid=(T // tb, n_exp // ec),
        in_specs=[pl.BlockSpec((tb, D), lambda t, c: (t, 0)), pl.BlockSpec((tb, D), lambda t, c: (t, 0)),
                  pl.BlockSpec((ec, D), lambda t, c: (c, 0)), pl.BlockSpec((D, ec), lambda t, c: (0, c)),
                  rows, rows, whole, whole, vec, vec],
        out_specs=pl.BlockSpec((tb, D), lambda t, c: (t, 0)),
        scratch_shapes=[pltpu.VMEM((D, tb), F32), pltpu.VMEM((ec, tb), F32), pltpu.VMEM((ec, tb), BF16)],
        compiler_params=pltpu.CompilerParams(dimension_semantics=("parallel", "arbitrary"),
                                             vmem_limit_bytes=VMEM_LIMIT_BYTES),
        name="peer",
        interpret=interpret,
    )(hb, h, ub, vtb, cnt, at, rk, bm, g2.reshape(1, D).astype(F32), b2.reshape(1, D).astype(F32))


def _tile(n, pref):
    t = min(pref, n)
    assert n % t == 0, (n, t)
    return t


def kernel(x, w_in, kv_norm_g, w_uk, w_uv, hg_lb_logits, hg_norm_g, w_out, ln1_g, ln1_b,
           peer_w_q, peer_sub_keys, peer_u, peer_v, ln2_g, ln2_b):
    B, L, D = x.shape
    depth = w_in.shape[0]
    alpha = (2.0 * depth) ** 0.25
    T = B * L
    for layer in range(depth):
        (c, ct, kidx, qidxt, widxt, qlatt, hq, hf, hi, hg) = _proj(
            x, w_in[layer], kv_norm_g[layer], w_uk[layer], tb=_tile(L, 512))
        tq = _tile(L, 256)
        att = _dsa(kidx, c, ct, qidxt, widxt, qlatt, w_uv[layer], tq=tq, ks=tq, kc=tq)
        hgo = _hgrn(hq, hf, hi, hg, hg_lb_logits, hg_norm_g[layer], ts=_tile(L, 512), layer=layer)
        h, hb, rcnt, rat, rrk, rbm = _route(
            att.reshape(T, ATT_Q), hgo.reshape(T, HG_W), x.reshape(T, D), w_out[layer], ln1_g[layer], ln1_b[layer],
            peer_w_q[layer], peer_sub_keys[layer], tb=_tile(T, 512), alpha=alpha)
        y = _peer(hb, h, peer_u[layer], peer_v[layer], rcnt, rat, rrk, rbm, ln2_g[layer], ln2_b[layer],
                  tb=_tile(T, 512), ic=16, tw=256, alpha=alpha)
        x = y.reshape(B, L, D)
    return x
```

```python
import functools

import jax
import jax.numpy as jnp
from jax import lax
from jax.experimental import pallas as pl
from jax.experimental.pallas import tpu as pltpu

F32 = jnp.float32
BF16 = jnp.bfloat16
I32 = jnp.int32
I16 = jnp.int16

ATT_HEADS = 8
ATT_HEAD_DIM = 64
KV_LATENT = 128
IDX_HEADS = 8
IDX_DIM = 64
TOPK_MAX = 256
HG_HEADS = 8
HG_DIM = 64
HG_CHUNK = 32
PEER_HEADS = 8
PEER_N_KEYS = 128
PEER_TOPK = 16
EPS = 1e-5

ATT_Q = ATT_HEADS * ATT_HEAD_DIM
IDX_Q = IDX_HEADS * IDX_DIM
HG_W = HG_HEADS * HG_DIM

LANES = 128
VMEM_LIMIT_BYTES = 56 * 1024 * 1024

LOG2E = 1.4426950408889634
NEG_BIG = -1e30
INT_MIN = -(2 ** 31)
HALF16 = 1 << 15
BF16_ROWS = 16
KEY_NEG_INF = INT_MIN + 0x7FFFFF


def _dot(a, b):
    return jnp.dot(a, b, preferred_element_type=F32)


def _dot_nt(a, b):
    return lax.dot_general(a, b, (((1,), (1,)), ((), ())), preferred_element_type=F32)


def _dot_exact(a, b):
    return jnp.dot(a, b, preferred_element_type=F32, precision=lax.Precision.HIGHEST)


def _proj_kernel(x_ref, wc_ref, wk_ref, whq_ref, whf_ref, whi_ref, whg_ref,
                 wtq_ref, wtqi_ref, wtwi_ref, wtc_ref, grow_ref, gcol_ref, wuk_ref,
                 c_ref, ct_ref, kidx_ref, qidxt_ref, widxt_ref, qlatt_ref,
                 hq_ref, hf_ref, hi_ref, hg_ref):
    xb = x_ref[0].astype(BF16)

    c = _dot(xb, wc_ref[...])
    c = c * lax.rsqrt(jnp.mean(c * c, axis=-1, keepdims=True) + EPS) * grow_ref[...]
    c_ref[0] = c.astype(BF16)
    ct = _dot_nt(wtc_ref[...], xb)
    ct = ct * lax.rsqrt(jnp.mean(ct * ct, axis=0, keepdims=True) + EPS) * gcol_ref[...]
    ct_ref[0] = ct.astype(BF16)

    kidx_ref[0] = _dot(xb, wk_ref[...]).astype(BF16)
    qidxt_ref[0] = _dot_nt(wtqi_ref[...], xb).astype(BF16)
    widxt_ref[0] = _dot_nt(wtwi_ref[...], xb)

    qt = _dot_nt(wtq_ref[...], xb).astype(BF16)
    scale = ATT_HEAD_DIM ** -0.5 * LOG2E
    for h in range(ATT_HEADS):
        ql = _dot(wuk_ref[h], qt[h * ATT_HEAD_DIM:(h + 1) * ATT_HEAD_DIM, :]) * scale
        qlatt_ref[0, h * KV_LATENT:(h + 1) * KV_LATENT, :] = ql.astype(BF16)

    hq_ref[0] = _dot(xb, whq_ref[...])
    hf_ref[0] = _dot(xb, whf_ref[...])
    hi_ref[0] = _dot(xb, whi_ref[...])
    hg_ref[0] = _dot(xb, whg_ref[...])


def _proj(x, w, kv_g, w_uk, *, tb, interpret=False):
    B, L, D = x.shape
    o = 0
    cols = {}
    for name, width in (("q", ATT_Q), ("c", KV_LATENT), ("qi", IDX_Q), ("k", IDX_DIM), ("wi", IDX_HEADS),
                        ("hq", HG_W), ("hf", HG_W), ("hi", HG_W), ("hg", HG_W)):
        cols[name] = w[:, o:o + width].astype(BF16)
        o += width
    assert o == w.shape[1]
    wt = lambda n: cols[n].T
    ins = [x, cols["c"], cols["k"], cols["hq"], cols["hf"], cols["hi"], cols["hg"],
           wt("q"), wt("qi"), wt("wi"), wt("c"),
           kv_g.reshape(1, KV_LATENT).astype(F32), kv_g.reshape(KV_LATENT, 1).astype(F32),
           w_uk.astype(BF16)]

    def full(a):
        zeros = (0,) * a.ndim
        return pl.BlockSpec(a.shape, lambda b, t: zeros)

    in_specs = [pl.BlockSpec((1, tb, D), lambda b, t: (b, t, 0))] + [full(a) for a in ins[1:]]
    nat = lambda wd, dt: (jax.ShapeDtypeStruct((B, L, wd), dt), pl.BlockSpec((1, tb, wd), lambda b, t: (b, t, 0)))
    tr = lambda wd, dt: (jax.ShapeDtypeStruct((B, wd, L), dt), pl.BlockSpec((1, wd, tb), lambda b, t: (b, 0, t)))
    outs = [nat(KV_LATENT, BF16), tr(KV_LATENT, BF16), nat(IDX_DIM, BF16), tr(IDX_Q, BF16), tr(IDX_HEADS, F32),
            tr(ATT_HEADS * KV_LATENT, BF16), nat(HG_W, F32), nat(HG_W, F32), nat(HG_W, F32), nat(HG_W, F32)]
    return pl.pallas_call(
        _proj_kernel,
        out_shape=[s for s, _ in outs],
        grid=(B, L // tb),
        in_specs=in_specs,
        out_specs=[s for _, s in outs],
        compiler_params=pltpu.CompilerParams(dimension_semantics=("parallel", "parallel"),
                                             vmem_limit_bytes=VMEM_LIMIT_BYTES),
        name="proj",
        interpret=interpret,
    )(*ins)


def _dsa_kernel(kidx_ref, c_ref, ct_ref, qidxt_ref, widxt_ref, qlatt_ref, wuvt_ref,
                out_ref, keys_ref, khi_ref, klo_ref, acc_ref, m_ref, l_ref, alpha_ref, jb_ref, ot_ref, a_ref, p_ref,
                *, tq, ks, kc, top_k, seq):
    qi = pl.program_id(1)
    n_keys = (qi + 1) * tq
    nks = (qi + 1) * (tq // ks)
    nkc = (qi + 1) * (tq // kc)
    q_pos = lambda rows: qi * tq + lax.broadcasted_iota(I32, (rows, tq), 1)
    k_off = lambda rows: lax.broadcasted_iota(I32, (rows, tq), 0)

    def score_chunk(k, carry):
        r0 = pl.multiple_of(k * ks, ks)
        kk = kidx_ref[0, pl.ds(r0, ks), :]
        s = jnp.zeros((ks, tq), F32)
        for h in range(IDX_HEADS):
            z = _dot(kk, qidxt_ref[0, h * IDX_DIM:(h + 1) * IDX_DIM, :])
            s = s + widxt_ref[0, h:h + 1, :] * jnp.maximum(z, 0.0)
        s = jnp.where(r0 + k_off(ks) <= q_pos(ks), s, -jnp.inf)
        bits = pltpu.bitcast(s, I32)
        key = bits ^ ((bits >> 31) & 0x7FFFFFFF)
        keys_ref[pl.ds(r0, ks), :] = key
        khi_ref[pl.ds(r0, ks), :] = (key >> 16).astype(I16)
        klo_ref[pl.ds(r0, ks), :] = ((key & 0xFFFF) - HALF16).astype(I16)
        return carry

    lax.fori_loop(0, nks, score_chunk, 0)

    kcc = 2 * kc
    nkcc = (nkc + 1) >> 1

    @pl.when(n_keys < seq)
    def _():
        khi_ref[pl.ds(pl.multiple_of(n_keys, kc), kc), :] = jnp.full((kc, tq), -HALF16, I16)

    def count(pred):
        def body(k, cnt):
            r0 = pl.multiple_of(k * kc, kc)
            blk = keys_ref[pl.ds(r0, kc), :]
            return cnt + jnp.sum(jnp.where(pred(blk, r0), 1, 0).astype(I32), axis=0, keepdims=True)
        return lax.fori_loop(0, nkc, body, jnp.zeros((1, tq), I32))

    def count16(ref, cand):
        c16 = cand.astype(I16)

        def body(k, cnt):
            r0 = pl.multiple_of(k * kcc, kcc)
            blk = ref[pl.ds(r0, kcc), :]
            ones = jnp.where(blk >= c16, jnp.ones(blk.shape, BF16), jnp.zeros(blk.shape, BF16))
            parts = [ones[g * BF16_ROWS:(g + 1) * BF16_ROWS] for g in range(kcc // BF16_ROWS)]
            while len(parts) > 1:
                parts = [parts[i] + parts[i + 1] for i in range(0, len(parts), 2)]
            return cnt + parts[0].astype(F32)
        cnt = lax.fori_loop(0, nkcc, body, jnp.zeros((BF16_ROWS, tq), F32))
        return jnp.sum(cnt, axis=0, keepdims=True).astype(I32)

    def hi_step(i, carry):
        t, n_t = carry
        cand = t + lax.shift_left(jnp.int32(1), 15 - i)
        cnt = count16(khi_ref, cand)
        take = cnt >= top_k
        return jnp.where(take, cand, t), jnp.where(take, cnt, n_t)

    h, n_ge_h = lax.fori_loop(0, 16, hi_step,
                              (jnp.full((1, tq), -HALF16, I32), jnp.full((1, tq), n_keys, I32)))
    h16 = h.astype(I16)

    def mark_boundary(k, carry):
        rows = pl.ds(pl.multiple_of(k * kcc, kcc), kcc)
        hi = khi_ref[rows, :]
        klo_ref[rows, :] = jnp.where(hi > h16, jnp.int16(HALF16 - 1),
                                     jnp.where(hi == h16, klo_ref[rows, :], jnp.int16(-HALF16)))
        return carry

    lax.fori_loop(0, nkcc, mark_boundary, 0)

    def lo_bit(i, carry):
        t, n_t = carry
        cand = t + lax.shift_left(jnp.int32(1), 15 - i)
        cnt = count16(klo_ref, cand)
        take = cnt >= top_k
        return jnp.where(take, cand, t), jnp.where(take, cnt, n_t)

    def lo_cond(carry):
        r, _, _, open_lanes = carry
        return jnp.logical_and(r < 4, open_lanes > 0)

    def lo_round(carry):
        r, t, n_t, _ = carry
        t, n_t = lax.fori_loop(r * 4, r * 4 + 4, lo_bit, (t, n_t))
        return r + 1, t, n_t, jnp.max(jnp.abs(n_t - top_k))

    _, t_lo, n_ge, _ = lax.while_loop(
        lo_cond, lo_round,
        (jnp.int32(0), jnp.full((1, tq), -HALF16, I32), n_ge_h, jnp.max(jnp.abs(n_ge_h - top_k))))
    thr = lax.shift_left(h, 16) + (t_lo + HALF16)

    has_tie = jnp.where(n_ge > top_k, jnp.where(thr > KEY_NEG_INF, 1, 0), 0)
    any_tie = jnp.max(has_tie)
    jb_ref[...] = jnp.full((1, tq), -1, I32)

    @pl.when(any_tie > 0)
    def _():
        need = top_k - count(lambda blk, r0: blk > thr)

        def idx_step(i, v):
            cand = v + lax.shift_left(jnp.int32(1), (seq.bit_length() - 1) - i)
            cnt = count(lambda blk, r0: (blk == thr) & (r0 + k_off(kc) < cand))
            return jnp.where(cnt < need, cand, v)
        v = lax.fori_loop(0, seq.bit_length(), idx_step, jnp.zeros((1, tq), I32))
        jb_ref[...] = jnp.where(has_tie > 0, v, -1)

    jb = jb_ref[...]
    gt_thr = jnp.where(has_tie > 0, thr, jnp.maximum(thr - 1, KEY_NEG_INF))

    l_ref[...] = jnp.zeros(l_ref.shape, F32)
    acc_ref[...] = jnp.zeros(acc_ref.shape, F32)

    def masked_bias(r0):
        key = keys_ref[pl.ds(r0, kc), :]
        return lax.cond(
            any_tie > 0,
            lambda: jnp.where(key > gt_thr, 0.0,
                              jnp.where(key == thr, jnp.where(r0 + k_off(kc) <= jb, 0.0, NEG_BIG), NEG_BIG)),
            lambda: jnp.where(key > gt_thr, 0.0, NEG_BIG))

    def logits(h, cc, bias, m_prev, slot):
        a = _dot(cc, qlatt_ref[0, h * KV_LATENT:(h + 1) * KV_LATENT, :]) + bias
        a_ref[slot, h] = a
        m_new = jnp.maximum(m_prev, jnp.max(a, axis=0, keepdims=True))
        alpha_ref[slot, h] = jnp.exp2(m_prev - m_new)
        m_ref[slot, h] = m_new

    bias0 = masked_bias(0)
    cc0 = c_ref[0, pl.ds(0, kc), :]
    for h in range(ATT_HEADS):
        logits(h, cc0, bias0, jnp.full((1, tq), NEG_BIG, F32), 0)

    def att_step(k, cur):
        nxt = 1 - cur
        r0 = pl.multiple_of(k * kc, kc)
        rn = pl.multiple_of(jnp.minimum(k + 1, nkc - 1) * kc, kc)
        bias_n = masked_bias(rn)
        cc_n = c_ref[0, pl.ds(rn, kc), :]
        cct = ct_ref[0, :, pl.ds(r0, kc)]
        for h in range(ATT_HEADS):
            m_cur = m_ref[cur, h]
            logits(h, cc_n, bias_n, m_cur, nxt)
            p = jnp.exp2(a_ref[cur, h] - m_cur)
            l_ref[h] = alpha_ref[cur, h] * l_ref[h] + jnp.sum(p, axis=0, keepdims=True)
            p_ref[h] = p.astype(BF16)
            acc_ref[h] = alpha_ref[cur, h] * acc_ref[h] + _dot(cct, p_ref[h])

    def att_chunk(k, carry):
        for parity in range(2):
            pl.when(k % 2 == parity)(functools.partial(att_step, k, parity))
        return carry

    lax.fori_loop(0, nkc, att_chunk, 0)

    for h in range(ATT_HEADS):
        o_lat = (acc_ref[h] / l_ref[h]).astype(BF16)
        ot_ref[h * ATT_HEAD_DIM:(h + 1) * ATT_HEAD_DIM, :] = _dot(wuvt_ref[h], o_lat)
    out_ref[0] = ot_ref[...].T.astype(BF16)


def _dsa(kidx, c, ct, qidxt, widxt, qlatt, w_uv, *, tq, ks, kc, interpret=False):
    B, L, _ = c.shape
    top_k = min(TOPK_MAX, L // 4)
    assert tq % ks == 0 and tq % kc == 0 and top_k <= tq and L % (2 * kc) == 0 and 2 * kc // BF16_ROWS <= 256
    wuvt = jnp.swapaxes(w_uv, 1, 2).astype(BF16)
    kernel = functools.partial(_dsa_kernel, tq=tq, ks=ks, kc=kc, top_k=top_k, seq=L)
    return pl.pallas_call(
        kernel,
        out_shape=jax.ShapeDtypeStruct((B, L, ATT_Q), BF16),
        grid=(B, L // tq),
        in_specs=[
            pl.BlockSpec((1, L, IDX_DIM), lambda b, q: (b, 0, 0)),
            pl.BlockSpec((1, L, KV_LATENT), lambda b, q: (b, 0, 0)),
            pl.BlockSpec((1, KV_LATENT, L), lambda b, q: (b, 0, 0)),
            pl.BlockSpec((1, IDX_Q, tq), lambda b, q: (b, 0, q)),
            pl.BlockSpec((1, IDX_HEADS, tq), lambda b, q: (b, 0, q)),
            pl.BlockSpec((1, ATT_HEADS * KV_LATENT, tq), lambda b, q: (b, 0, q)),
            pl.BlockSpec((ATT_HEADS, ATT_HEAD_DIM, KV_LATENT), lambda b, q: (0, 0, 0)),
        ],
        out_specs=pl.BlockSpec((1, tq, ATT_Q), lambda b, q: (b, q, 0)),
        scratch_shapes=[
            pltpu.VMEM((L, tq), I32),
            pltpu.VMEM((L, tq), I16),
            pltpu.VMEM((L, tq), I16),
            pltpu.VMEM((ATT_HEADS, KV_LATENT, tq), F32),
            pltpu.VMEM((2, ATT_HEADS, 1, tq), F32),
            pltpu.VMEM((ATT_HEADS, 1, tq), F32),
            pltpu.VMEM((2, ATT_HEADS, 1, tq), F32),
            pltpu.VMEM((1, tq), I32),
            pltpu.VMEM((ATT_Q, tq), F32),
            pltpu.VMEM((2, ATT_HEADS, kc, tq), F32),
            pltpu.VMEM((ATT_HEADS, kc, tq), BF16),
        ],
        compiler_params=pltpu.CompilerParams(dimension_semantics=("parallel", "parallel"),
                                             vmem_limit_bytes=VMEM_LIMIT_BYTES),
        name="dsa",
        interpret=interpret,
    )(kidx, c, ct, qidxt, widxt, qlatt, wuvt)


HG_GROUP = 128
HG_PAIRS = HG_HEADS // 2


def _hgrn_kernel(lbl_ref, gn_ref, hq_ref, hf_ref, hi_ref, hg_ref, out_ref, st_ref, *, ts, layer):
    @pl.when(pl.program_id(1) == 0)
    def _():
        st_ref[...] = jnp.zeros(st_ref.shape, F32)

    logits = lbl_ref[...]
    e = jnp.exp(logits - jnp.max(logits, axis=0, keepdims=True))
    lb = jnp.sum(e[0:layer + 1, :], axis=0, keepdims=True) / jnp.sum(e, axis=0, keepdims=True)

    g = HG_GROUP
    chunk_shift = HG_CHUNK.bit_length() - 1
    head_shift = HG_DIM.bit_length() - 1
    r = lax.broadcasted_iota(I32, (g, g), 0)
    cidx = lax.broadcasted_iota(I32, (g, g), 1)
    same_chunk = (r >> chunk_shift) == (cidx >> chunk_shift)
    causal = same_chunk & (cidx <= r)
    tri = jnp.where(causal, 1.0, 0.0).astype(F32)
    ones_blk = jnp.where(same_chunk, 1.0, 0.0).astype(F32)
    same_head = (r >> head_shift) == (cidx >> head_shift)
    head_blk = jnp.where(same_head, 1.0, 0.0).astype(F32)
    lane_lo = lax.broadcasted_iota(I32, (g, LANES), 1) < HG_DIM
    col_chunk = cidx >> chunk_shift
    gn = gn_ref[...]

    def group(gi, carry):
        r0 = pl.multiple_of(gi * g, g)
        rows = pl.ds(r0, g)
        f = lb + (1.0 - lb) * jax.nn.sigmoid(hf_ref[0, rows, :])
        logf = jnp.log(f)
        kgate = 1.0 - f
        b = _dot_exact(tri, logf)
        btot = _dot_exact(ones_blk, logf)
        q_t = jax.nn.silu(hq_ref[0, rows, :]) * jnp.exp(b)
        k_t = kgate * jnp.exp(-b)
        k_dec = kgate * jnp.exp(btot - b)
        decay = jnp.exp(btot)
        v = hi_ref[0, rows, :]
        gate = hg_ref[0, rows, :]
        for p in range(HG_PAIRS):
            ls = slice(p * LANES, (p + 1) * LANES)
            qp = q_t[:, ls]
            kp = k_t[:, ls].astype(BF16)
            vp = v[:, ls].astype(BF16)
            kdp = k_dec[:, ls].astype(BF16)
            q_lo = jnp.where(lane_lo, qp, 0.0).astype(BF16)
            q_hi = jnp.where(lane_lo, 0.0, qp).astype(BF16)
            a_lo = jnp.where(causal, _dot_nt(q_lo, kp), 0.0).astype(BF16)
            a_hi = jnp.where(causal, _dot_nt(q_hi, kp), 0.0).astype(BF16)
            o = jnp.where(lane_lo, _dot(a_lo, vp), _dot(a_hi, vp))
            st = st_ref[p]
            qpb = qp.astype(BF16)
            vt = v[:, ls].T
            inter = []
            for ch in range(g // HG_CHUNK):
                rs = slice(ch * HG_CHUNK, (ch + 1) * HG_CHUNK)
                inter.append(_dot_nt(qpb[rs], st.astype(BF16)))
                vt_ch = jnp.where(col_chunk == ch, vt, 0.0).astype(BF16)
                ds = _dot(vt_ch, kdp)
                st = st * decay[ch * HG_CHUNK:ch * HG_CHUNK + 1, ls] + jnp.where(same_head, ds, 0.0)
            st_ref[p] = st
            o = o + jnp.concatenate(inter, axis=0)
            ms = _dot_exact(o * o, head_blk) * (1.0 / HG_DIM)
            gt = gate[:, ls]
            on = o * lax.rsqrt(ms + EPS) * gn * (gt * jax.nn.sigmoid(gt))
            out_ref[0, rows, ls] = on.astype(BF16)
        return carry

    for gi in range(ts // g):
        group(gi, 0)


def _hgrn(hq, hf, hi, hg, lb_logits, g_norm, *, ts, layer, interpret=False):
    B, L, W = hq.shape
    gn = jnp.tile(g_norm.reshape(1, HG_DIM).astype(F32), (1, 2))
    blk = pl.BlockSpec((1, ts, W), lambda b, t: (b, t, 0))
    return pl.pallas_call(
        functools.partial(_hgrn_kernel, ts=ts, layer=layer),
        out_shape=jax.ShapeDtypeStruct((B, L, W), BF16),
        grid=(B, L // ts),
        in_specs=[pl.BlockSpec(lb_logits.shape, lambda b, t: (0, 0)),
                  pl.BlockSpec((1, LANES), lambda b, t: (0, 0)), blk, blk, blk, blk],
        out_specs=blk,
        scratch_shapes=[pltpu.VMEM((HG_PAIRS, LANES, LANES), F32)],
        compiler_params=pltpu.CompilerParams(dimension_semantics=("parallel", "arbitrary"),
                                             vmem_limit_bytes=VMEM_LIMIT_BYTES),
        name="hgrn2",
        interpret=interpret,
    )(lb_logits.astype(F32), gn, hq, hf, hi, hg)


def _layer_norm(y, g, b):
    mu = jnp.mean(y, axis=-1, keepdims=True)
    d = y - mu
    var = jnp.mean(d * d, axis=-1, keepdims=True)
    return d * lax.rsqrt(var + EPS) * g + b


def _sort_network(n):
    pairs = []

    def merge(lo, hi, r):
        step = r * 2
        if step < hi - lo:
            merge(lo, hi, step)
            merge(lo + r, hi, step)
            pairs.extend((i, i + r) for i in range(lo + r, hi - r, step))
        else:
            pairs.append((lo, lo + r))

    def sort(lo, hi):
        if hi - lo >= 1:
            mid = lo + (hi - lo) // 2
            sort(lo, mid)
            sort(mid + 1, hi)
            merge(lo, hi, 1)

    sort(0, n - 1)
    return pairs


SUBLANES = 8


def _top_sorted(s, n):
    regs = [s[r:r + SUBLANES] for r in range(0, s.shape[0], SUBLANES)]
    for i, j in _sort_network(len(regs)):
        regs[i], regs[j] = jnp.maximum(regs[i], regs[j]), jnp.minimum(regs[i], regs[j])
    vals = []
    for it in range(n):
        mx = jnp.max(regs[0], axis=0, keepdims=True)
        vals.append(mx)
        if it + 1 < n:
            took = regs[0] == mx
            depth = min(len(regs), n - it)
            for d in range(depth - 1):
                regs[d] = jnp.where(took, regs[d + 1], regs[d])
            regs[depth - 1] = jnp.where(took, regs[depth] if depth < len(regs) else -jnp.inf, regs[depth - 1])
    return vals


PEER_KEEP = PEER_TOPK + 1
_PAIR_RANKS = [(i, j) for i in range(1, PEER_KEEP + 1) for j in range(1, PEER_KEEP + 1) if i * j <= PEER_KEEP]
PEER_CAND_ROWS = SUBLANES * (1 << (-(-len(_PAIR_RANKS) // SUBLANES) - 1).bit_length())


def _route_kernel(att_ref, hg_ref, x_ref, woa_ref, wob_ref, g1_ref, b1_ref, wq_ref, sk_ref,
                  h_ref, hb_ref, cnt_ref, at_ref, rk_ref, bm_ref, *, alpha):
    mix = _dot(att_ref[...], woa_ref[...]) + _dot(hg_ref[...], wob_ref[...])
    h = _layer_norm(alpha * x_ref[...] + mix, g1_ref[...], b1_ref[...])
    h_ref[...] = h
    hb = h.astype(BF16)
    hb_ref[...] = hb
    qp = _dot(hb, wq_ref[...]).astype(BF16)
    dk = PEER_N_KEYS
    tb = h.shape[0]
    for hd in range(PEER_HEADS):
        s1_all = _dot_nt(sk_ref[hd, 0], qp[:, (2 * hd) * dk:(2 * hd + 1) * dk])
        s2_all = _dot_nt(sk_ref[hd, 1], qp[:, (2 * hd + 1) * dk:(2 * hd + 2) * dk])
        for sub in range(tb // LANES):
            ls = slice(sub * LANES, (sub + 1) * LANES)
            s1, s2 = s1_all[:, ls], s2_all[:, ls]
            a = _top_sorted(s1, PEER_KEEP)
            bb = _top_sorted(s2, PEER_KEEP)
            sums = {(i, j): a[i - 1] + bb[j - 1] for i, j in _PAIR_RANKS}
            cands = list(sums.values())
            pad = PEER_CAND_ROWS - len(cands)
            cand = jnp.concatenate(cands + [jnp.full_like(cands[0], -jnp.inf)] * pad, axis=0)
            top = _top_sorted(cand, PEER_KEEP)
            c_max, c_k, c_next = top[0], top[PEER_TOPK - 1], top[PEER_TOPK]
            thr = 0.5 * (c_k + c_next)
            z = jnp.sum(jnp.where(cand >= c_k, jnp.exp(cand - c_max), 0.0), axis=0, keepdims=True)
            rank2 = jnp.full(s2.shape, float(PEER_KEEP), F32)
            cnt = jnp.zeros(s1.shape, F32)
            for q in range(PEER_KEEP, 0, -1):
                rank2 = jnp.where(s2 == bb[q - 1], float(q - 1), rank2)
                n_q = sum(jnp.where(sums[(q, j)] >= thr, 1.0, 0.0) for j in range(1, PEER_KEEP // q + 1))
                cnt = jnp.where(s1 == a[q - 1], n_q, cnt)
            cnt_ref[hd, :, ls] = cnt
            at_ref[hd, :, ls] = jnp.exp(s1 - a[0]) / z
            rk_ref[hd, :, ls] = rank2.astype(BF16)
            bm_ref[hd, :, ls] = jnp.exp(s2 - bb[0]).astype(BF16)


def _route(att, hgo, x2, w_out, g1, b1, w_q, sub_keys, *, tb, alpha, interpret=False):
    T, D = x2.shape
    woa = w_out[:ATT_Q].astype(BF16)
    wob = w_out[ATT_Q:].astype(BF16)

    def full(a):
        zeros = (0,) * a.ndim
        return pl.BlockSpec(a.shape, lambda t: zeros)

    ins = [att, hgo, x2, woa, wob, g1.reshape(1, D).astype(F32), b1.reshape(1, D).astype(F32),
           w_q.astype(BF16), sub_keys.astype(BF16)]
    in_specs = [pl.BlockSpec((tb, ATT_Q), lambda t: (t, 0)), pl.BlockSpec((tb, HG_W), lambda t: (t, 0)),
                pl.BlockSpec((tb, D), lambda t: (t, 0))] + [full(a) for a in ins[3:]]
    rt_shape = lambda dt: jax.ShapeDtypeStruct((PEER_HEADS, PEER_N_KEYS, T), dt)
    rt_spec = pl.BlockSpec((PEER_HEADS, PEER_N_KEYS, tb), lambda t: (0, 0, t))
    return pl.pallas_call(
        functools.partial(_route_kernel, alpha=alpha),
        out_shape=[jax.ShapeDtypeStruct((T, D), F32), jax.ShapeDtypeStruct((T, D), BF16),
                   rt_shape(F32), rt_shape(F32), rt_shape(BF16), rt_shape(BF16)],
        grid=(T // tb,),
        in_specs=in_specs,
        out_specs=[pl.BlockSpec((tb, D), lambda t: (t, 0)), pl.BlockSpec((tb, D), lambda t: (t, 0)),
                   rt_spec, rt_spec, rt_spec, rt_spec],
        compiler_params=pltpu.CompilerParams(dimension_semantics=("parallel",),
                                             vmem_limit_bytes=VMEM_LIMIT_BYTES),
        name="route",
        interpret=interpret,
    )(*ins)


GELU_K = 2.0 * (2.0 / 3.141592653589793) ** 0.5


def _gelu_tanh(x):
    z = x * ((-GELU_K * LOG2E) + (-GELU_K * 0.044715 * LOG2E) * (x * x))
    return x / (1.0 + jnp.exp2(z))


def _peer_kernel(hb_ref, h_ref, u_ref, vt_ref, cnt_ref, at_ref, rk_ref, bm_ref, g2_ref, b2_ref,
                 out_ref, acc_ref, act_ref, p_ref, *, ic, tw, alpha):
    c = pl.program_id(1)

    @pl.when(c == 0)
    def _():
        acc_ref[...] = jnp.zeros(acc_ref.shape, F32)

    nk = PEER_N_KEYS
    tb = hb_ref.shape[0]
    grp = nk // BF16_ROWS
    act_ref[...] = _gelu_tanh(_dot_nt(u_ref[...], hb_ref[...]).astype(BF16))
    for ii in range(ic):
        rows = slice(ii * nk, (ii + 1) * nk)
        for s in range(tb // tw):
            ls = slice(s * tw, (s + 1) * tw)
            gsum = jnp.zeros((grp, BF16_ROWS, tw), BF16)
            for hd in range(PEER_HEADS):
                cnt = jnp.broadcast_to(cnt_ref[hd, ii:ii + 1, ls], (BF16_ROWS, tw)).astype(BF16)
                wgt = jnp.broadcast_to(at_ref[hd, ii:ii + 1, ls], (BF16_ROWS, tw)).astype(BF16)
                rk = rk_ref[hd, :, ls].reshape(grp, BF16_ROWS, tw)
                bm = bm_ref[hd, :, ls].reshape(grp, BF16_ROWS, tw)
                gsum = gsum + jnp.where(rk < cnt[None], bm, jnp.zeros_like(bm)) * wgt[None]
            p_ref[rows, ls] = gsum.reshape(nk, tw) * act_ref[rows, ls]
    acc_ref[...] += _dot(vt_ref[...], p_ref[...])

    @pl.when(c == pl.num_programs(1) - 1)
    def _():
        y = alpha * h_ref[...] + acc_ref[...].T
        out_ref[...] = _layer_norm(y, g2_ref[...], b2_ref[...])


def _peer(hb, h, u, v, cnt, at, rk, bm, g2, b2, *, tb, ic, tw, alpha, interpret=False):
    T, D = h.shape
    n_exp = u.shape[0]
    ub = u.astype(BF16)
    vtb = v.astype(BF16).T
    ec = ic * PEER_N_KEYS
    rows = pl.BlockSpec((PEER_HEADS, ic, tb), lambda t, c: (0, c, t))
    whole = pl.BlockSpec((PEER_HEADS, PEER_N_KEYS, tb), lambda t, c: (0, 0, t))
    vec = pl.BlockSpec((1, D), lambda t, c: (0, 0))
    return pl.pallas_call(
        functools.partial(_peer_kernel, ic=ic, tw=tw, alpha=alpha),
        out_shape=jax.ShapeDtypeStruct((T, D), F32),
        grid=(T // tb, n_exp // ec),
        in_specs=[pl.BlockSpec((tb, D), lambda t, c: (t, 0)), pl.BlockSpec((tb, D), lambda t, c: (t, 0)),
                  pl.BlockSpec((ec, D), lambda t, c: (c, 0)), pl.BlockSpec((D, ec), lambda t, c: (0, c)),
                  rows, rows, whole, whole, vec, vec],
        out_specs=pl.BlockSpec((tb, D), lambda t, c: (t, 0)),
        scratch_shapes=[pltpu.VMEM((D, tb), F32), pltpu.VMEM((ec, tb), BF16), pltpu.VMEM((ec, tb), BF16)],
        compiler_params=pltpu.CompilerParams(dimension_semantics=("parallel", "arbitrary"),
                                             vmem_limit_bytes=VMEM_LIMIT_BYTES),
        name="peer",
        interpret=interpret,
    )(hb, h, ub, vtb, cnt, at, rk, bm, g2.reshape(1, D).astype(F32), b2.reshape(1, D).astype(F32))


def _tile(n, pref):
    t = min(pref, n)
    assert n % t == 0, (n, t)
    return t


def kernel(x, w_in, kv_norm_g, w_uk, w_uv, hg_lb_logits, hg_norm_g, w_out, ln1_g, ln1_b,
           peer_w_q, peer_sub_keys, peer_u, peer_v, ln2_g, ln2_b):
    B, L, D = x.shape
    depth = w_in.shape[0]
    alpha = (2.0 * depth) ** 0.25
    T = B * L
    for layer in range(depth):
        (c, ct, kidx, qidxt, widxt, qlatt, hq, hf, hi, hg) = _proj(
            x, w_in[layer], kv_norm_g[layer], w_uk[layer], tb=_tile(L, 512))
        tq = _tile(L, 256)
        att = _dsa(kidx, c, ct, qidxt, widxt, qlatt, w_uv[layer], tq=tq, ks=tq, kc=tq)
        hgo = _hgrn(hq, hf, hi, hg, hg_lb_logits, hg_norm_g[layer], ts=_tile(L, 512), layer=layer)
        h, hb, rcnt, rat, rrk, rbm = _route(
            att.reshape(T, ATT_Q), hgo.reshape(T, HG_W), x.reshape(T, D), w_out[layer], ln1_g[layer], ln1_b[layer],
            peer_w_q[layer], peer_sub_keys[layer], tb=_tile(T, 512), alpha=alpha)
        y = _peer(hb, h, peer_u[layer], peer_v[layer], rcnt, rat, rrk, rbm, ln2_g[layer], ln2_b[layer],
                  tb=_tile(T, 512), ic=16, tw=256, alpha=alpha)
        x = y.reshape(B, L, D)
    return x
```

```python
import functools

import jax
import jax.numpy as jnp
from jax import lax
from jax.experimental import pallas as pl
from jax.experimental.pallas import tpu as pltpu

F32 = jnp.float32
BF16 = jnp.bfloat16
I32 = jnp.int32
I16 = jnp.int16

ATT_HEADS = 8
ATT_HEAD_DIM = 64
KV_LATENT = 128
IDX_HEADS = 8
IDX_DIM = 64
TOPK_MAX = 256
HG_HEADS = 8
HG_DIM = 64
HG_CHUNK = 32
PEER_HEADS = 8
PEER_N_KEYS = 128
PEER_TOPK = 16
EPS = 1e-5

ATT_Q = ATT_HEADS * ATT_HEAD_DIM
IDX_Q = IDX_HEADS * IDX_DIM
HG_W = HG_HEADS * HG_DIM

LANES = 128
VMEM_LIMIT_BYTES = 56 * 1024 * 1024

LOG2E = 1.4426950408889634
NEG_BIG = -1e30
INT_MIN = -(2 ** 31)
HALF16 = 1 << 15
BF16_ROWS = 16
KEY_NEG_INF = INT_MIN + 0x7FFFFF


def _dot(a, b):
    return jnp.dot(a, b, preferred_element_type=F32)


def _dot_nt(a, b):
    return lax.dot_general(a, b, (((1,), (1,)), ((), ())), preferred_element_type=F32)


def _dot_exact(a, b):
    return jnp.dot(a, b, preferred_element_type=F32, precision=lax.Precision.HIGHEST)


def _proj_kernel(x_ref, wc_ref, wk_ref, whq_ref, whf_ref, whi_ref, whg_ref,
                 wtq_ref, wtqi_ref, wtwi_ref, wtc_ref, grow_ref, gcol_ref, wuk_ref,
                 c_ref, ct_ref, kidx_ref, qidxt_ref, widxt_ref, qlatt_ref,
                 hq_ref, hf_ref, hi_ref, hg_ref):
    xb = x_ref[0].astype(BF16)

    c = _dot(xb, wc_ref[...])
    c = c * lax.rsqrt(jnp.mean(c * c, axis=-1, keepdims=True) + EPS) * grow_ref[...]
    c_ref[0] = c.astype(BF16)
    ct = _dot_nt(wtc_ref[...], xb)
    ct = ct * lax.rsqrt(jnp.mean(ct * ct, axis=0, keepdims=True) + EPS) * gcol_ref[...]
    ct_ref[0] = ct.astype(BF16)

    kidx_ref[0] = _dot(xb, wk_ref[...]).astype(BF16)
    qidxt_ref[0] = _dot_nt(wtqi_ref[...], xb).astype(BF16)
    widxt_ref[0] = _dot_nt(wtwi_ref[...], xb)

    qt = _dot_nt(wtq_ref[...], xb).astype(BF16)
    scale = ATT_HEAD_DIM ** -0.5 * LOG2E
    for h in range(ATT_HEADS):
        ql = _dot(wuk_ref[h], qt[h * ATT_HEAD_DIM:(h + 1) * ATT_HEAD_DIM, :]) * scale
        qlatt_ref[0, h * KV_LATENT:(h + 1) * KV_LATENT, :] = ql.astype(BF16)

    hq_ref[0] = _dot(xb, whq_ref[...])
    hf_ref[0] = _dot(xb, whf_ref[...])
    hi_ref[0] = _dot(xb, whi_ref[...])
    hg_ref[0] = _dot(xb, whg_ref[...])


def _proj(x, w, kv_g, w_uk, *, tb, interpret=False):
    B, L, D = x.shape
    o = 0
    cols = {}
    for name, width in (("q", ATT_Q), ("c", KV_LATENT), ("qi", IDX_Q), ("k", IDX_DIM), ("wi", IDX_HEADS),
                        ("hq", HG_W), ("hf", HG_W), ("hi", HG_W), ("hg", HG_W)):
        cols[name] = w[:, o:o + width].astype(BF16)
        o += width
    assert o == w.shape[1]
    wt = lambda n: cols[n].T
    ins = [x, cols["c"], cols["k"], cols["hq"], cols["hf"], cols["hi"], cols["hg"],
           wt("q"), wt("qi"), wt("wi"), wt("c"),
           kv_g.reshape(1, KV_LATENT).astype(F32), kv_g.reshape(KV_LATENT, 1).astype(F32),
           w_uk.astype(BF16)]

    def full(a):
        zeros = (0,) * a.ndim
        return pl.BlockSpec(a.shape, lambda b, t: zeros)

    in_specs = [pl.BlockSpec((1, tb, D), lambda b, t: (b, t, 0))] + [full(a) for a in ins[1:]]
    nat = lambda wd, dt: (jax.ShapeDtypeStruct((B, L, wd), dt), pl.BlockSpec((1, tb, wd), lambda b, t: (b, t, 0)))
    tr = lambda wd, dt: (jax.ShapeDtypeStruct((B, wd, L), dt), pl.BlockSpec((1, wd, tb), lambda b, t: (b, 0, t)))
    outs = [nat(KV_LATENT, BF16), tr(KV_LATENT, BF16), nat(IDX_DIM, BF16), tr(IDX_Q, BF16), tr(IDX_HEADS, F32),
            tr(ATT_HEADS * KV_LATENT, BF16), nat(HG_W, F32), nat(HG_W, F32), nat(HG_W, F32), nat(HG_W, F32)]
    return pl.pallas_call(
        _proj_kernel,
        out_shape=[s for s, _ in outs],
        grid=(B, L // tb),
        in_specs=in_specs,
        out_specs=[s for _, s in outs],
        compiler_params=pltpu.CompilerParams(dimension_semantics=("parallel", "parallel"),
                                             vmem_limit_bytes=VMEM_LIMIT_BYTES),
        name="proj",
        interpret=interpret,
    )(*ins)


def _dsa_kernel(kidx_ref, c_ref, ct_ref, qidxt_ref, widxt_ref, qlatt_ref, wuvt_ref,
                out_ref, keys_ref, khi_ref, klo_ref, acc_ref, m_ref, l_ref, alpha_ref, jb_ref, ot_ref, a_ref, p_ref,
                *, tq, ks, kc, top_k, seq):
    qi = pl.program_id(1)
    n_keys = (qi + 1) * tq
    nks = (qi + 1) * (tq // ks)
    nkc = (qi + 1) * (tq // kc)
    q_pos = lambda rows: qi * tq + lax.broadcasted_iota(I32, (rows, tq), 1)
    k_off = lambda rows: lax.broadcasted_iota(I32, (rows, tq), 0)

    def score_chunk(k, carry):
        r0 = pl.multiple_of(k * ks, ks)
        kk = kidx_ref[0, pl.ds(r0, ks), :]
        s = jnp.zeros((ks, tq), F32)
        for h in range(IDX_HEADS):
            z = _dot(kk, qidxt_ref[0, h * IDX_DIM:(h + 1) * IDX_DIM, :])
            s = s + widxt_ref[0, h:h + 1, :] * jnp.maximum(z, 0.0)
        s = jnp.where(r0 + k_off(ks) <= q_pos(ks), s, -jnp.inf)
        bits = pltpu.bitcast(s, I32)
        key = bits ^ ((bits >> 31) & 0x7FFFFFFF)
        keys_ref[pl.ds(r0, ks), :] = key
        khi_ref[pl.ds(r0, ks), :] = (key >> 16).astype(I16)
        klo_ref[pl.ds(r0, ks), :] = ((key & 0xFFFF) - HALF16).astype(I16)
        return carry

    lax.fori_loop(0, nks, score_chunk, 0)

    kcc = 2 * kc
    nkcc = (nkc + 1) >> 1

    @pl.when(n_keys < seq)
    def _():
        pad_rows = pl.ds(pl.multiple_of(n_keys, kc), kc)
        khi_ref[pad_rows, :] = jnp.full((kc, tq), -HALF16, I16)
        klo_ref[pad_rows, :] = jnp.full((kc, tq), -HALF16, I16)

    def count(pred):
        def body(k, cnt):
            r0 = pl.multiple_of(k * kc, kc)
            blk = keys_ref[pl.ds(r0, kc), :]
            return cnt + jnp.sum(jnp.where(pred(blk, r0), 1, 0).astype(I32), axis=0, keepdims=True)
        return lax.fori_loop(0, nkc, body, jnp.zeros((1, tq), I32))

    def count16(ref, cand):
        c16 = cand.astype(I16)

        def body(k, cnt):
            r0 = pl.multiple_of(k * kcc, kcc)
            blk = ref[pl.ds(r0, kcc), :]
            ones = jnp.where(blk >= c16, jnp.ones(blk.shape, BF16), jnp.zeros(blk.shape, BF16))
            parts = [ones[g * BF16_ROWS:(g + 1) * BF16_ROWS] for g in range(kcc // BF16_ROWS)]
            while len(parts) > 1:
                parts = [parts[i] + parts[i + 1] for i in range(0, len(parts), 2)]
            return cnt + parts[0].astype(F32)
        cnt = lax.fori_loop(0, nkcc, body, jnp.zeros((BF16_ROWS, tq), F32))
        return jnp.sum(cnt, axis=0, keepdims=True).astype(I32)

    def hi_step(i, carry):
        t, n_t = carry
        cand = t + lax.shift_left(jnp.int32(1), 15 - i)
        cnt = count16(khi_ref, cand)
        take = cnt >= top_k
        return jnp.where(take, cand, t), jnp.where(take, cnt, n_t)

    h, n_ge_h = lax.fori_loop(0, 16, hi_step,
                              (jnp.full((1, tq), -HALF16, I32), jnp.full((1, tq), n_keys, I32)))
    h16 = h.astype(I16)

    def mark_boundary(k, carry):
        rows = pl.ds(pl.multiple_of(k * kcc, kcc), kcc)
        hi = khi_ref[rows, :]
        klo_ref[rows, :] = jnp.where(hi > h16, jnp.int16(HALF16 - 1),
                                     jnp.where(hi == h16, klo_ref[rows, :], jnp.int16(-HALF16)))
        return carry

    lax.fori_loop(0, nkcc, mark_boundary, 0)

    def lo_bit(i, carry):
        t, n_t = carry
        cand = t + lax.shift_left(jnp.int32(1), 15 - i)
        cnt = count16(klo_ref, cand)
        take = cnt >= top_k
        return jnp.where(take, cand, t), jnp.where(take, cnt, n_t)

    def lo_cond(carry):
        r, _, _, open_lanes = carry
        return jnp.logical_and(r < 4, open_lanes > 0)

    def lo_round(carry):
        r, t, n_t, _ = carry
        t, n_t = lax.fori_loop(r * 4, r * 4 + 4, lo_bit, (t, n_t))
        return r + 1, t, n_t, jnp.max(jnp.abs(n_t - top_k))

    _, t_lo, n_ge, _ = lax.while_loop(
        lo_cond, lo_round,
        (jnp.int32(0), jnp.full((1, tq), -HALF16, I32), n_ge_h, jnp.max(jnp.abs(n_ge_h - top_k))))
    thr = lax.shift_left(h, 16) + (t_lo + HALF16)

    has_tie = jnp.where(n_ge > top_k, jnp.where(thr > KEY_NEG_INF, 1, 0), 0)
    any_tie = jnp.max(has_tie)
    jb_ref[...] = jnp.full((1, tq), -1, I32)

    @pl.when(any_tie > 0)
    def _():
        need = top_k - count(lambda blk, r0: blk > thr)

        def idx_step(i, v):
            cand = v + lax.shift_left(jnp.int32(1), (seq.bit_length() - 1) - i)
            cnt = count(lambda blk, r0: (blk == thr) & (r0 + k_off(kc) < cand))
            return jnp.where(cnt < need, cand, v)
        v = lax.fori_loop(0, seq.bit_length(), idx_step, jnp.zeros((1, tq), I32))
        jb_ref[...] = jnp.where(has_tie > 0, v, -1)

    jb = jb_ref[...]
    gt_thr = jnp.where(has_tie > 0, thr, jnp.maximum(thr - 1, KEY_NEG_INF))

    l_ref[...] = jnp.zeros(l_ref.shape, F32)
    acc_ref[...] = jnp.zeros(acc_ref.shape, F32)

    def masked_bias(r0):
        key = keys_ref[pl.ds(r0, kc), :]
        return lax.cond(
            any_tie > 0,
            lambda: jnp.where(key > gt_thr, 0.0,
                              jnp.where(key == thr, jnp.where(r0 + k_off(kc) <= jb, 0.0, NEG_BIG), NEG_BIG)),
            lambda: jnp.where(key > gt_thr, 0.0, NEG_BIG))

    def logits(h, cc, bias, m_prev, slot):
        a = _dot(cc, qlatt_ref[0, h * KV_LATENT:(h + 1) * KV_LATENT, :]) + bias
        a_ref[slot, h] = a
        m_new = jnp.maximum(m_prev, jnp.max(a, axis=0, keepdims=True))
        alpha_ref[slot, h] = jnp.exp2(m_prev - m_new)
        m_ref[slot, h] = m_new

    bias0 = masked_bias(0)
    cc0 = c_ref[0, pl.ds(0, kc), :]
    for h in range(ATT_HEADS):
        logits(h, cc0, bias0, jnp.full((1, tq), NEG_BIG, F32), 0)

    def att_step(k, cur):
        nxt = 1 - cur
        r0 = pl.multiple_of(k * kc, kc)
        rn = pl.multiple_of(jnp.minimum(k + 1, nkc - 1) * kc, kc)
        bias_n = masked_bias(rn)
        cc_n = c_ref[0, pl.ds(rn, kc), :]
        cct = ct_ref[0, :, pl.ds(r0, kc)]
        for h in range(ATT_HEADS):
            m_cur = m_ref[cur, h]
            logits(h, cc_n, bias_n, m_cur, nxt)
            p = jnp.exp2(a_ref[cur, h] - m_cur)
            l_ref[h] = alpha_ref[cur, h] * l_ref[h] + jnp.sum(p, axis=0, keepdims=True)
            p_ref[h] = p.astype(BF16)
            acc_ref[h] = alpha_ref[cur, h] * acc_ref[h] + _dot(cct, p_ref[h])

    def att_chunk(k, carry):
        for parity in range(2):
            pl.when(k % 2 == parity)(functools.partial(att_step, k, parity))
        return carry

    lax.fori_loop(0, nkc, att_chunk, 0)

    for h in range(ATT_HEADS):
        o_lat = (acc_ref[h] / l_ref[h]).astype(BF16)
        ot_ref[h * ATT_HEAD_DIM:(h + 1) * ATT_HEAD_DIM, :] = _dot(wuvt_ref[h], o_lat)
    out_ref[0] = ot_ref[...].T.astype(BF16)


def _dsa(kidx, c, ct, qidxt, widxt, qlatt, w_uv, *, tq, ks, kc, interpret=False):
    B, L, _ = c.shape
    top_k = min(TOPK_MAX, L // 4)
    assert tq % ks == 0 and tq % kc == 0 and top_k <= tq and L % (2 * kc) == 0 and 2 * kc // BF16_ROWS <= 256
    wuvt = jnp.swapaxes(w_uv, 1, 2).astype(BF16)
    kernel = functools.partial(_dsa_kernel, tq=tq, ks=ks, kc=kc, top_k=top_k, seq=L)
    return pl.pallas_call(
        kernel,
        out_shape=jax.ShapeDtypeStruct((B, L, ATT_Q), BF16),
        grid=(B, L // tq),
        in_specs=[
            pl.BlockSpec((1, L, IDX_DIM), lambda b, q: (b, 0, 0)),
            pl.BlockSpec((1, L, KV_LATENT), lambda b, q: (b, 0, 0)),
            pl.BlockSpec((1, KV_LATENT, L), lambda b, q: (b, 0, 0)),
            pl.BlockSpec((1, IDX_Q, tq), lambda b, q: (b, 0, q)),
            pl.BlockSpec((1, IDX_HEADS, tq), lambda b, q: (b, 0, q)),
            pl.BlockSpec((1, ATT_HEADS * KV_LATENT, tq), lambda b, q: (b, 0, q)),
            pl.BlockSpec((ATT_HEADS, ATT_HEAD_DIM, KV_LATENT), lambda b, q: (0, 0, 0)),
        ],
        out_specs=pl.BlockSpec((1, tq, ATT_Q), lambda b, q: (b, q, 0)),
        scratch_shapes=[
            pltpu.VMEM((L, tq), I32),
            pltpu.VMEM((L, tq), I16),
            pltpu.VMEM((L, tq), I16),
            pltpu.VMEM((ATT_HEADS, KV_LATENT, tq), F32),
            pltpu.VMEM((2, ATT_HEADS, 1, tq), F32),
            pltpu.VMEM((ATT_HEADS, 1, tq), F32),
            pltpu.VMEM((2, ATT_HEADS, 1, tq), F32),
            pltpu.VMEM((1, tq), I32),
            pltpu.VMEM((ATT_Q, tq), F32),
            pltpu.VMEM((2, ATT_HEADS, kc, tq), F32),
            pltpu.VMEM((ATT_HEADS, kc, tq), BF16),
        ],
        compiler_params=pltpu.CompilerParams(dimension_semantics=("parallel", "parallel"),
                                             vmem_limit_bytes=VMEM_LIMIT_BYTES),
        name="dsa",
        interpret=interpret,
    )(kidx, c, ct, qidxt, widxt, qlatt, wuvt)


HG_GROUP = 128
HG_PAIRS = HG_HEADS // 2


def _hgrn_kernel(lbl_ref, gn_ref, hq_ref, hf_ref, hi_ref, hg_ref, out_ref, st_ref, *, ts, layer):
    @pl.when(pl.program_id(1) == 0)
    def _():
        st_ref[...] = jnp.zeros(st_ref.shape, F32)

    logits = lbl_ref[...]
    e = jnp.exp(logits - jnp.max(logits, axis=0, keepdims=True))
    lb = jnp.sum(e[0:layer + 1, :], axis=0, keepdims=True) / jnp.sum(e, axis=0, keepdims=True)

    g = HG_GROUP
    chunk_shift = HG_CHUNK.bit_length() - 1
    head_shift = HG_DIM.bit_length() - 1
    r = lax.broadcasted_iota(I32, (g, g), 0)
    cidx = lax.broadcasted_iota(I32, (g, g), 1)
    same_chunk = (r >> chunk_shift) == (cidx >> chunk_shift)
    causal = same_chunk & (cidx <= r)
    tri = jnp.where(causal, 1.0, 0.0).astype(F32)
    ones_blk = jnp.where(same_chunk, 1.0, 0.0).astype(F32)
    same_head = (r >> head_shift) == (cidx >> head_shift)
    head_blk = jnp.where(same_head, 1.0, 0.0).astype(F32)
    lane_lo = lax.broadcasted_iota(I32, (g, LANES), 1) < HG_DIM
    col_chunk = cidx >> chunk_shift
    gn = gn_ref[...]

    def group(gi, carry):
        r0 = pl.multiple_of(gi * g, g)
        rows = pl.ds(r0, g)
        f = lb + (1.0 - lb) * jax.nn.sigmoid(hf_ref[0, rows, :])
        logf = jnp.log(f)
        kgate = 1.0 - f
        b = _dot_exact(tri, logf)
        btot = _dot_exact(ones_blk, logf)
        q_t = jax.nn.silu(hq_ref[0, rows, :]) * jnp.exp(b)
        k_t = kgate * jnp.exp(-b)
        k_dec = kgate * jnp.exp(btot - b)
        decay = jnp.exp(btot)
        v = hi_ref[0, rows, :]
        gate = hg_ref[0, rows, :]
        for p in range(HG_PAIRS):
            ls = slice(p * LANES, (p + 1) * LANES)
            qp = q_t[:, ls]
            kp = k_t[:, ls].astype(BF16)
            vp = v[:, ls].astype(BF16)
            kdp = k_dec[:, ls].astype(BF16)
            q_lo = jnp.where(lane_lo, qp, 0.0).astype(BF16)
            q_hi = jnp.where(lane_lo, 0.0, qp).astype(BF16)
            a_lo = jnp.where(causal, _dot_nt(q_lo, kp), 0.0).astype(BF16)
            a_hi = jnp.where(causal, _dot_nt(q_hi, kp), 0.0).astype(BF16)
            o = jnp.where(lane_lo, _dot(a_lo, vp), _dot(a_hi, vp))
            st = st_ref[p]
            qpb = qp.astype(BF16)
            vt = v[:, ls].T
            inter = []
            for ch in range(g // HG_CHUNK):
                rs = slice(ch * HG_CHUNK, (ch + 1) * HG_CHUNK)
                inter.append(_dot_nt(qpb[rs], st.astype(BF16)))
                vt_ch = jnp.where(col_chunk == ch, vt, 0.0).astype(BF16)
                ds = _dot(vt_ch, kdp)
                st = st * decay[ch * HG_CHUNK:ch * HG_CHUNK + 1, ls] + jnp.where(same_head, ds, 0.0)
            st_ref[p] = st
            o = o + jnp.concatenate(inter, axis=0)
            ms = _dot_exact(o * o, head_blk) * (1.0 / HG_DIM)
            gt = gate[:, ls]
            on = o * lax.rsqrt(ms + EPS) * gn * (gt * jax.nn.sigmoid(gt))
            out_ref[0, rows, ls] = on.astype(BF16)
        return carry

    for gi in range(ts // g):
        group(gi, 0)


def _hgrn(hq, hf, hi, hg, lb_logits, g_norm, *, ts, layer, interpret=False):
    B, L, W = hq.shape
    gn = jnp.tile(g_norm.reshape(1, HG_DIM).astype(F32), (1, 2))
    blk = pl.BlockSpec((1, ts, W), lambda b, t: (b, t, 0))
    return pl.pallas_call(
        functools.partial(_hgrn_kernel, ts=ts, layer=layer),
        out_shape=jax.ShapeDtypeStruct((B, L, W), BF16),
        grid=(B, L // ts),
        in_specs=[pl.BlockSpec(lb_logits.shape, lambda b, t: (0, 0)),
                  pl.BlockSpec((1, LANES), lambda b, t: (0, 0)), blk, blk, blk, blk],
        out_specs=blk,
        scratch_shapes=[pltpu.VMEM((HG_PAIRS, LANES, LANES), F32)],
        compiler_params=pltpu.CompilerParams(dimension_semantics=("parallel", "arbitrary"),
                                             vmem_limit_bytes=VMEM_LIMIT_BYTES),
        name="hgrn2",
        interpret=interpret,
    )(lb_logits.astype(F32), gn, hq, hf, hi, hg)


def _layer_norm(y, g, b):
    mu = jnp.mean(y, axis=-1, keepdims=True)
    d = y - mu
    var = jnp.mean(d * d, axis=-1, keepdims=True)
    return d * lax.rsqrt(var + EPS) * g + b


def _sort_network(n):
    pairs = []

    def merge(lo, hi, r):
        step = r * 2
        if step < hi - lo:
            merge(lo, hi, step)
            merge(lo + r, hi, step)
            pairs.extend((i, i + r) for i in range(lo + r, hi - r, step))
        else:
            pairs.append((lo, lo + r))

    def sort(lo, hi):
        if hi - lo >= 1:
            mid = lo + (hi - lo) // 2
            sort(lo, mid)
            sort(mid + 1, hi)
            merge(lo, hi, 1)

    sort(0, n - 1)
    return pairs


SUBLANES = 8


def _top_sorted(s, n):
    regs = [s[r:r + SUBLANES] for r in range(0, s.shape[0], SUBLANES)]
    for i, j in _sort_network(len(regs)):
        regs[i], regs[j] = jnp.maximum(regs[i], regs[j]), jnp.minimum(regs[i], regs[j])
    vals = []
    for it in range(n):
        mx = jnp.max(regs[0], axis=0, keepdims=True)
        vals.append(mx)
        if it + 1 < n:
            took = regs[0] == mx
            depth = min(len(regs), n - it)
            for d in range(depth - 1):
                regs[d] = jnp.where(took, regs[d + 1], regs[d])
            regs[depth - 1] = jnp.where(took, regs[depth] if depth < len(regs) else -jnp.inf, regs[depth - 1])
    return vals


PEER_KEEP = PEER_TOPK + 1
_PAIR_RANKS = [(i, j) for i in range(1, PEER_KEEP + 1) for j in range(1, PEER_KEEP + 1) if i * j <= PEER_KEEP]
PEER_CAND_ROWS = SUBLANES * (1 << (-(-len(_PAIR_RANKS) // SUBLANES) - 1).bit_length())


def _route_kernel(att_ref, hg_ref, x_ref, woa_ref, wob_ref, g1_ref, b1_ref, wq_ref, sk_ref,
                  h_ref, hb_ref, cnt_ref, at_ref, rk_ref, bm_ref, *, alpha):
    mix = _dot(att_ref[...], woa_ref[...]) + _dot(hg_ref[...], wob_ref[...])
    h = _layer_norm(alpha * x_ref[...] + mix, g1_ref[...], b1_ref[...])
    h_ref[...] = h
    hb = h.astype(BF16)
    hb_ref[...] = hb
    qp = _dot(hb, wq_ref[...]).astype(BF16)
    dk = PEER_N_KEYS
    tb = h.shape[0]
    for hd in range(PEER_HEADS):
        s1_all = _dot_nt(sk_ref[hd, 0], qp[:, (2 * hd) * dk:(2 * hd + 1) * dk])
        s2_all = _dot_nt(sk_ref[hd, 1], qp[:, (2 * hd + 1) * dk:(2 * hd + 2) * dk])
        for sub in range(tb // LANES):
            ls = slice(sub * LANES, (sub + 1) * LANES)
            s1, s2 = s1_all[:, ls], s2_all[:, ls]
            a = _top_sorted(s1, PEER_KEEP)
            bb = _top_sorted(s2, PEER_KEEP)
            sums = {(i, j): a[i - 1] + bb[j - 1] for i, j in _PAIR_RANKS}
            cands = list(sums.values())
            pad = PEER_CAND_ROWS - len(cands)
            cand = jnp.concatenate(cands + [jnp.full_like(cands[0], -jnp.inf)] * pad, axis=0)
            top = _top_sorted(cand, PEER_KEEP)
            c_max, c_k, c_next = top[0], top[PEER_TOPK - 1], top[PEER_TOPK]
            thr = 0.5 * (c_k + c_next)
            z = jnp.sum(jnp.where(cand >= c_k, jnp.exp(cand - c_max), 0.0), axis=0, keepdims=True)
            rank2 = jnp.full(s2.shape, float(PEER_KEEP), F32)
            cnt = jnp.zeros(s1.shape, F32)
            for q in range(PEER_KEEP, 0, -1):
                rank2 = jnp.where(s2 == bb[q - 1], float(q - 1), rank2)
                n_q = sum(jnp.where(sums[(q, j)] >= thr, 1.0, 0.0) for j in range(1, PEER_KEEP // q + 1))
                cnt = jnp.where(s1 == a[q - 1], n_q, cnt)
            cnt_ref[hd, :, ls] = cnt
            at_ref[hd, :, ls] = jnp.exp(s1 - a[0]) / z
            rk_ref[hd, :, ls] = rank2.astype(BF16)
            bm_ref[hd, :, ls] = jnp.exp(s2 - bb[0]).astype(BF16)


def _route(att, hgo, x2, w_out, g1, b1, w_q, sub_keys, *, tb, alpha, interpret=False):
    T, D = x2.shape
    woa = w_out[:ATT_Q].astype(BF16)
    wob = w_out[ATT_Q:].astype(BF16)

    def full(a):
        zeros = (0,) * a.ndim
        return pl.BlockSpec(a.shape, lambda t: zeros)

    ins = [att, hgo, x2, woa, wob, g1.reshape(1, D).astype(F32), b1.reshape(1, D).astype(F32),
           w_q.astype(BF16), sub_keys.astype(BF16)]
    in_specs = [pl.BlockSpec((tb, ATT_Q), lambda t: (t, 0)), pl.BlockSpec((tb, HG_W), lambda t: (t, 0)),
                pl.BlockSpec((tb, D), lambda t: (t, 0))] + [full(a) for a in ins[3:]]
    rt_shape = lambda dt: jax.ShapeDtypeStruct((PEER_HEADS, PEER_N_KEYS, T), dt)
    rt_spec = pl.BlockSpec((PEER_HEADS, PEER_N_KEYS, tb), lambda t: (0, 0, t))
    return pl.pallas_call(
        functools.partial(_route_kernel, alpha=alpha),
        out_shape=[jax.ShapeDtypeStruct((T, D), F32), jax.ShapeDtypeStruct((T, D), BF16),
                   rt_shape(F32), rt_shape(F32), rt_shape(BF16), rt_shape(BF16)],
        grid=(T // tb,),
        in_specs=in_specs,
        out_specs=[pl.BlockSpec((tb, D), lambda t: (t, 0)), pl.BlockSpec((tb, D), lambda t: (t, 0)),
                   rt_spec, rt_spec, rt_spec, rt_spec],
        compiler_params=pltpu.CompilerParams(dimension_semantics=("parallel",),
                                             vmem_limit_bytes=VMEM_LIMIT_BYTES),
        name="route",
        interpret=interpret,
    )(*ins)


GELU_K = 2.0 * (2.0 / 3.141592653589793) ** 0.5


def _gelu_tanh(x):
    z = x * ((-GELU_K * LOG2E) + (-GELU_K * 0.044715 * LOG2E) * (x * x))
    return x / (1.0 + jnp.exp2(z))


def _peer_kernel(hb_ref, h_ref, u_ref, vt_ref, cnt_ref, at_ref, rk_ref, bm_ref, g2_ref, b2_ref,
                 out_ref, acc_ref, act_ref, p_ref, *, ic, tw, alpha):
    c = pl.program_id(1)

    @pl.when(c == 0)
    def _():
        acc_ref[...] = jnp.zeros(acc_ref.shape, F32)

    nk = PEER_N_KEYS
    tb = hb_ref.shape[0]
    grp = nk // BF16_ROWS
    act_ref[...] = _gelu_tanh(_dot_nt(u_ref[...], hb_ref[...]).astype(BF16))
    for ii in range(ic):
        rows = slice(ii * nk, (ii + 1) * nk)
        for s in range(tb // tw):
            ls = slice(s * tw, (s + 1) * tw)
            gsum = jnp.zeros((grp, BF16_ROWS, tw), BF16)
            for hd in range(PEER_HEADS):
                cnt = jnp.broadcast_to(cnt_ref[hd, ii:ii + 1, ls], (BF16_ROWS, tw)).astype(BF16)
                wgt = jnp.broadcast_to(at_ref[hd, ii:ii + 1, ls], (BF16_ROWS, tw)).astype(BF16)
                rk = rk_ref[hd, :, ls].reshape(grp, BF16_ROWS, tw)
                bm = bm_ref[hd, :, ls].reshape(grp, BF16_ROWS, tw)
                gsum = gsum + jnp.where(rk < cnt[None], bm, jnp.zeros_like(bm)) * wgt[None]
            p_ref[rows, ls] = gsum.reshape(nk, tw) * act_ref[rows, ls]
    acc_ref[...] += _dot(vt_ref[...], p_ref[...])

    @pl.when(c == pl.num_programs(1) - 1)
    def _():
        y = alpha * h_ref[...] + acc_ref[...].T
        out_ref[...] = _layer_norm(y, g2_ref[...], b2_ref[...])


def _peer(hb, h, u, v, cnt, at, rk, bm, g2, b2, *, tb, ic, tw, alpha, interpret=False):
    T, D = h.shape
    n_exp = u.shape[0]
    ub = u.astype(BF16)
    vtb = v.astype(BF16).T
    ec = ic * PEER_N_KEYS
    rows = pl.BlockSpec((PEER_HEADS, ic, tb), lambda t, c: (0, c, t))
    whole = pl.BlockSpec((PEER_HEADS, PEER_N_KEYS, tb), lambda t, c: (0, 0, t))
    vec = pl.BlockSpec((1, D), lambda t, c: (0, 0))
    return pl.pallas_call(
        functools.partial(_peer_kernel, ic=ic, tw=tw, alpha=alpha),
        out_shape=jax.ShapeDtypeStruct((T, D), F32),
        grid=(T // tb, n_exp // ec),
        in_specs=[pl.BlockSpec((tb, D), lambda t, c: (t, 0)), pl.BlockSpec((tb, D), lambda t, c: (t, 0)),
                  pl.BlockSpec((ec, D), lambda t, c: (c, 0)), pl.BlockSpec((D, ec), lambda t, c: (0, c)),
                  rows, rows, whole, whole, vec, vec],
        out_specs=pl.BlockSpec((tb, D), lambda t, c: (t, 0)),
        scratch_shapes=[pltpu.VMEM((D, tb), F32), pltpu.VMEM((ec, tb), BF16), pltpu.VMEM((ec, tb), BF16)],
        compiler_params=pltpu.CompilerParams(dimension_semantics=("parallel", "arbitrary"),
                                             vmem_limit_bytes=VMEM_LIMIT_BYTES),
        name="peer",
        interpret=interpret,
    )(hb, h, ub, vtb, cnt, at, rk, bm, g2.reshape(1, D).astype(F32), b2.reshape(1, D).astype(F32))


def _tile(n, pref):
    t = min(pref, n)
    assert n % t == 0, (n, t)
    return t


PROJ_TOKENS = 512
DSA_QUERIES = 256
HGRN_TOKENS = 512
ROUTE_TOKENS = 512
PEER_TOKENS = 512
PEER_I_PER_STEP = 16
PEER_GATE_LANES = 256


def kernel(x, w_in, kv_norm_g, w_uk, w_uv, hg_lb_logits, hg_norm_g, w_out, ln1_g, ln1_b,
           peer_w_q, peer_sub_keys, peer_u, peer_v, ln2_g, ln2_b):
    B, L, D = x.shape
    depth = w_in.shape[0]
    alpha = (2.0 * depth) ** 0.25
    T = B * L
    for layer in range(depth):
        (c, ct, kidx, qidxt, widxt, qlatt, hq, hf, hi, hg) = _proj(
            x, w_in[layer], kv_norm_g[layer], w_uk[layer], tb=_tile(L, PROJ_TOKENS))
        tq = _tile(L, DSA_QUERIES)
        att = _dsa(kidx, c, ct, qidxt, widxt, qlatt, w_uv[layer], tq=tq, ks=tq, kc=tq)
        hgo = _hgrn(hq, hf, hi, hg, hg_lb_logits, hg_norm_g[layer], ts=_tile(L, HGRN_TOKENS), layer=layer)
        h, hb, rcnt, rat, rrk, rbm = _route(
            att.reshape(T, ATT_Q), hgo.reshape(T, HG_W), x.reshape(T, D), w_out[layer], ln1_g[layer], ln1_b[layer],
            peer_w_q[layer], peer_sub_keys[layer], tb=_tile(T, ROUTE_TOKENS), alpha=alpha)
        y = _peer(hb, h, peer_u[layer], peer_v[layer], rcnt, rat, rrk, rbm, ln2_g[layer], ln2_b[layer],
                  tb=_tile(T, PEER_TOKENS), ic=PEER_I_PER_STEP, tw=PEER_GATE_LANES, alpha=alpha)
        x = y.reshape(B, L, D)
    return x
```

```python
import functools

import jax
import jax.numpy as jnp
from jax import lax
from jax.experimental import pallas as pl
from jax.experimental.pallas import tpu as pltpu

F32 = jnp.float32
BF16 = jnp.bfloat16
I32 = jnp.int32
I16 = jnp.int16

ATT_HEADS = 8
ATT_HEAD_DIM = 64
KV_LATENT = 128
IDX_HEADS = 8
IDX_DIM = 64
TOPK_MAX = 256
HG_HEADS = 8
HG_DIM = 64
HG_CHUNK = 32
PEER_HEADS = 8
PEER_N_KEYS = 128
PEER_TOPK = 16
EPS = 1e-5

ATT_Q = ATT_HEADS * ATT_HEAD_DIM
IDX_Q = IDX_HEADS * IDX_DIM
HG_W = HG_HEADS * HG_DIM

LANES = 128
VMEM_LIMIT_BYTES = 56 * 1024 * 1024

LOG2E = 1.4426950408889634
NEG_BIG = -1e30
INT_MIN = -(2 ** 31)
HALF16 = 1 << 15
BF16_ROWS = 16
KEY_NEG_INF = INT_MIN + 0x7FFFFF


def _dot(a, b):
    return jnp.dot(a, b, preferred_element_type=F32)


def _dot_nt(a, b):
    return lax.dot_general(a, b, (((1,), (1,)), ((), ())), preferred_element_type=F32)


def _dot_exact(a, b):
    return jnp.dot(a, b, preferred_element_type=F32, precision=lax.Precision.HIGHEST)


def _proj_kernel(x_ref, wc_ref, wk_ref, whq_ref, whf_ref, whi_ref, whg_ref,
                 wtq_ref, wtqi_ref, wtwi_ref, wtc_ref, grow_ref, gcol_ref, wuk_ref,
                 c_ref, ct_ref, kidx_ref, qidxt_ref, widxt_ref, qlatt_ref,
                 hq_ref, hf_ref, hi_ref, hg_ref):
    xb = x_ref[0].astype(BF16)

    c = _dot(xb, wc_ref[...])
    c = c * lax.rsqrt(jnp.mean(c * c, axis=-1, keepdims=True) + EPS) * grow_ref[...]
    c_ref[0] = c.astype(BF16)
    ct = _dot_nt(wtc_ref[...], xb)
    ct = ct * lax.rsqrt(jnp.mean(ct * ct, axis=0, keepdims=True) + EPS) * gcol_ref[...]
    ct_ref[0] = ct.astype(BF16)

    kidx_ref[0] = _dot(xb, wk_ref[...]).astype(BF16)
    qidxt_ref[0] = _dot_nt(wtqi_ref[...], xb).astype(BF16)
    widxt_ref[0] = _dot_nt(wtwi_ref[...], xb)

    qt = _dot_nt(wtq_ref[...], xb).astype(BF16)
    scale = ATT_HEAD_DIM ** -0.5 * LOG2E
    for h in range(ATT_HEADS):
        ql = _dot(wuk_ref[h], qt[h * ATT_HEAD_DIM:(h + 1) * ATT_HEAD_DIM, :]) * scale
        qlatt_ref[0, h * KV_LATENT:(h + 1) * KV_LATENT, :] = ql.astype(BF16)

    hq_ref[0] = _dot(xb, whq_ref[...])
    hf_ref[0] = _dot(xb, whf_ref[...])
    hi_ref[0] = _dot(xb, whi_ref[...])
    hg_ref[0] = _dot(xb, whg_ref[...])


def _proj(x, w, kv_g, w_uk, *, tb, interpret=False):
    B, L, D = x.shape
    o = 0
    cols = {}
    for name, width in (("q", ATT_Q), ("c", KV_LATENT), ("qi", IDX_Q), ("k", IDX_DIM), ("wi", IDX_HEADS),
                        ("hq", HG_W), ("hf", HG_W), ("hi", HG_W), ("hg", HG_W)):
        cols[name] = w[:, o:o + width].astype(BF16)
        o += width
    assert o == w.shape[1]
    wt = lambda n: cols[n].T
    ins = [x, cols["c"], cols["k"], cols["hq"], cols["hf"], cols["hi"], cols["hg"],
           wt("q"), wt("qi"), wt("wi"), wt("c"),
           kv_g.reshape(1, KV_LATENT).astype(F32), kv_g.reshape(KV_LATENT, 1).astype(F32),
           w_uk.astype(BF16)]

    def full(a):
        zeros = (0,) * a.ndim
        return pl.BlockSpec(a.shape, lambda b, t: zeros)

    in_specs = [pl.BlockSpec((1, tb, D), lambda b, t: (b, t, 0))] + [full(a) for a in ins[1:]]
    nat = lambda wd, dt: (jax.ShapeDtypeStruct((B, L, wd), dt), pl.BlockSpec((1, tb, wd), lambda b, t: (b, t, 0)))
    tr = lambda wd, dt: (jax.ShapeDtypeStruct((B, wd, L), dt), pl.BlockSpec((1, wd, tb), lambda b, t: (b, 0, t)))
    outs = [nat(KV_LATENT, BF16), tr(KV_LATENT, BF16), nat(IDX_DIM, BF16), tr(IDX_Q, BF16), tr(IDX_HEADS, F32),
            tr(ATT_HEADS * KV_LATENT, BF16), nat(HG_W, F32), nat(HG_W, F32), nat(HG_W, F32), nat(HG_W, F32)]
    return pl.pallas_call(
        _proj_kernel,
        out_shape=[s for s, _ in outs],
        grid=(B, L // tb),
        in_specs=in_specs,
        out_specs=[s for _, s in outs],
        compiler_params=pltpu.CompilerParams(dimension_semantics=("parallel", "parallel"),
                                             vmem_limit_bytes=VMEM_LIMIT_BYTES),
        name="proj",
        interpret=interpret,
    )(*ins)


def _dsa_kernel(kidx_ref, c_ref, ct_ref, qidxt_ref, widxt_ref, qlatt_ref, wuvt_ref,
                out_ref, keys_ref, khi_ref, klo_ref, acc_ref, m_ref, l_ref, alpha_ref, jb_ref, ot_ref, a_ref, p_ref,
                *, tq, ks, kc, top_k, seq):
    qi = pl.program_id(1)
    n_keys = (qi + 1) * tq
    nks = (qi + 1) * (tq // ks)
    nkc = (qi + 1) * (tq // kc)
    q_pos = lambda rows: qi * tq + lax.broadcasted_iota(I32, (rows, tq), 1)
    k_off = lambda rows: lax.broadcasted_iota(I32, (rows, tq), 0)

    def score_chunk(k, carry):
        r0 = pl.multiple_of(k * ks, ks)
        kk = kidx_ref[0, pl.ds(r0, ks), :]
        s = jnp.zeros((ks, tq), F32)
        for h in range(IDX_HEADS):
            z = _dot(kk, qidxt_ref[0, h * IDX_DIM:(h + 1) * IDX_DIM, :])
            s = s + widxt_ref[0, h:h + 1, :] * jnp.maximum(z, 0.0)
        s = lax.cond(r0 + ks > qi * tq,
                     lambda v: jnp.where(r0 + k_off(ks) <= q_pos(ks), v, -jnp.inf), lambda v: v, s)
        bits = pltpu.bitcast(s, I32)
        key = bits ^ ((bits >> 31) & 0x7FFFFFFF)
        keys_ref[pl.ds(r0, ks), :] = key
        khi_ref[pl.ds(r0, ks), :] = (key >> 16).astype(I16)
        klo_ref[pl.ds(r0, ks), :] = ((key & 0xFFFF) - HALF16).astype(I16)
        return carry

    lax.fori_loop(0, nks, score_chunk, 0)

    kcc = 2 * kc
    nkcc = (nkc + 1) >> 1

    @pl.when(n_keys < seq)
    def _():
        pad_rows = pl.ds(pl.multiple_of(n_keys, kc), kc)
        khi_ref[pad_rows, :] = jnp.full((kc, tq), -HALF16, I16)
        klo_ref[pad_rows, :] = jnp.full((kc, tq), -HALF16, I16)

    def count(pred):
        def body(k, cnt):
            r0 = pl.multiple_of(k * kc, kc)
            blk = keys_ref[pl.ds(r0, kc), :]
            return cnt + jnp.sum(jnp.where(pred(blk, r0), 1, 0).astype(I32), axis=0, keepdims=True)
        return lax.fori_loop(0, nkc, body, jnp.zeros((1, tq), I32))

    def count16(ref, cand):
        c16 = cand.astype(I16)

        def body(k, cnt):
            r0 = pl.multiple_of(k * kcc, kcc)
            blk = ref[pl.ds(r0, kcc), :]
            ones = jnp.where(blk >= c16, jnp.ones(blk.shape, BF16), jnp.zeros(blk.shape, BF16))
            parts = [ones[g * BF16_ROWS:(g + 1) * BF16_ROWS] for g in range(kcc // BF16_ROWS)]
            while len(parts) > 1:
                parts = [parts[i] + parts[i + 1] for i in range(0, len(parts), 2)]
            return cnt + parts[0].astype(F32)
        cnt = lax.fori_loop(0, nkcc, body, jnp.zeros((BF16_ROWS, tq), F32))
        return jnp.sum(cnt, axis=0, keepdims=True).astype(I32)

    def hi_step(i, carry):
        t, n_t = carry
        cand = t + lax.shift_left(jnp.int32(1), 15 - i)
        cnt = count16(khi_ref, cand)
        take = cnt >= top_k
        return jnp.where(take, cand, t), jnp.where(take, cnt, n_t)

    h, n_ge_h = lax.fori_loop(0, 16, hi_step,
                              (jnp.full((1, tq), -HALF16, I32), jnp.full((1, tq), n_keys, I32)))
    h16 = h.astype(I16)

    def mark_boundary(k, carry):
        rows = pl.ds(pl.multiple_of(k * kcc, kcc), kcc)
        hi = khi_ref[rows, :]
        klo_ref[rows, :] = jnp.where(hi > h16, jnp.int16(HALF16 - 1),
                                     jnp.where(hi == h16, klo_ref[rows, :], jnp.int16(-HALF16)))
        return carry

    lax.fori_loop(0, nkcc, mark_boundary, 0)

    def lo_bit(i, carry):
        t, n_t = carry
        cand = t + lax.shift_left(jnp.int32(1), 15 - i)
        cnt = count16(klo_ref, cand)
        take = cnt >= top_k
        return jnp.where(take, cand, t), jnp.where(take, cnt, n_t)

    def lo_cond(carry):
        r, _, _, open_lanes = carry
        return jnp.logical_and(r < 4, open_lanes > 0)

    def lo_round(carry):
        r, t, n_t, _ = carry
        t, n_t = lax.fori_loop(r * 4, r * 4 + 4, lo_bit, (t, n_t))
        return r + 1, t, n_t, jnp.max(jnp.abs(n_t - top_k))

    _, t_lo, n_ge, _ = lax.while_loop(
        lo_cond, lo_round,
        (jnp.int32(0), jnp.full((1, tq), -HALF16, I32), n_ge_h, jnp.max(jnp.abs(n_ge_h - top_k))))
    thr = lax.shift_left(h, 16) + (t_lo + HALF16)

    has_tie = jnp.where(n_ge > top_k, jnp.where(thr > KEY_NEG_INF, 1, 0), 0)
    any_tie = jnp.max(has_tie)
    jb_ref[...] = jnp.full((1, tq), -1, I32)

    @pl.when(any_tie > 0)
    def _():
        need = top_k - count(lambda blk, r0: blk > thr)

        def idx_step(i, v):
            cand = v + lax.shift_left(jnp.int32(1), (seq.bit_length() - 1) - i)
            cnt = count(lambda blk, r0: (blk == thr) & (r0 + k_off(kc) < cand))
            return jnp.where(cnt < need, cand, v)
        v = lax.fori_loop(0, seq.bit_length(), idx_step, jnp.zeros((1, tq), I32))
        jb_ref[...] = jnp.where(has_tie > 0, v, -1)

    jb = jb_ref[...]
    gt_thr = jnp.where(has_tie > 0, thr, jnp.maximum(thr - 1, KEY_NEG_INF))

    l_ref[...] = jnp.zeros(l_ref.shape, F32)
    acc_ref[...] = jnp.zeros(acc_ref.shape, F32)

    def masked_bias(r0):
        key = keys_ref[pl.ds(r0, kc), :]
        return lax.cond(
            any_tie > 0,
            lambda: jnp.where(key > gt_thr, 0.0,
                              jnp.where(key == thr, jnp.where(r0 + k_off(kc) <= jb, 0.0, NEG_BIG), NEG_BIG)),
            lambda: jnp.where(key > gt_thr, 0.0, NEG_BIG))

    def logits(h, cc, bias, m_prev, slot):
        a = _dot(cc, qlatt_ref[0, h * KV_LATENT:(h + 1) * KV_LATENT, :]) + bias
        a_ref[slot, h] = a
        m_new = jnp.maximum(m_prev, jnp.max(a, axis=0, keepdims=True))
        alpha_ref[slot, h] = jnp.exp2(m_prev - m_new)
        m_ref[slot, h] = m_new

    bias0 = masked_bias(0)
    cc0 = c_ref[0, pl.ds(0, kc), :]
    for h in range(ATT_HEADS):
        logits(h, cc0, bias0, jnp.full((1, tq), NEG_BIG, F32), 0)

    def att_step(k, cur):
        nxt = 1 - cur
        r0 = pl.multiple_of(k * kc, kc)
        rn = pl.multiple_of(jnp.minimum(k + 1, nkc - 1) * kc, kc)
        bias_n = masked_bias(rn)
        cc_n = c_ref[0, pl.ds(rn, kc), :]
        cct = ct_ref[0, :, pl.ds(r0, kc)]
        for h in range(ATT_HEADS):
            m_cur = m_ref[cur, h]
            logits(h, cc_n, bias_n, m_cur, nxt)
            p = jnp.exp2(a_ref[cur, h] - m_cur)
            l_ref[h] = alpha_ref[cur, h] * l_ref[h] + jnp.sum(p, axis=0, keepdims=True)
            p_ref[h] = p.astype(BF16)
            acc_ref[h] = alpha_ref[cur, h] * acc_ref[h] + _dot(cct, p_ref[h])

    def att_chunk(k, carry):
        for parity in range(2):
            pl.when(k % 2 == parity)(functools.partial(att_step, k, parity))
        return carry

    lax.fori_loop(0, nkc, att_chunk, 0)

    for h in range(ATT_HEADS):
        o_lat = (acc_ref[h] / l_ref[h]).astype(BF16)
        ot_ref[h * ATT_HEAD_DIM:(h + 1) * ATT_HEAD_DIM, :] = _dot(wuvt_ref[h], o_lat)
    out_ref[0] = ot_ref[...].T.astype(BF16)


def _dsa(kidx, c, ct, qidxt, widxt, qlatt, w_uv, *, tq, ks, kc, interpret=False):
    B, L, _ = c.shape
    top_k = min(TOPK_MAX, L // 4)
    assert tq % ks == 0 and tq % kc == 0 and top_k <= tq and L % (2 * kc) == 0 and 2 * kc // BF16_ROWS <= 256
    wuvt = jnp.swapaxes(w_uv, 1, 2).astype(BF16)
    kernel = functools.partial(_dsa_kernel, tq=tq, ks=ks, kc=kc, top_k=top_k, seq=L)
    return pl.pallas_call(
        kernel,
        out_shape=jax.ShapeDtypeStruct((B, L, ATT_Q), BF16),
        grid=(B, L // tq),
        in_specs=[
            pl.BlockSpec((1, L, IDX_DIM), lambda b, q: (b, 0, 0)),
            pl.BlockSpec((1, L, KV_LATENT), lambda b, q: (b, 0, 0)),
            pl.BlockSpec((1, KV_LATENT, L), lambda b, q: (b, 0, 0)),
            pl.BlockSpec((1, IDX_Q, tq), lambda b, q: (b, 0, q)),
            pl.BlockSpec((1, IDX_HEADS, tq), lambda b, q: (b, 0, q)),
            pl.BlockSpec((1, ATT_HEADS * KV_LATENT, tq), lambda b, q: (b, 0, q)),
            pl.BlockSpec((ATT_HEADS, ATT_HEAD_DIM, KV_LATENT), lambda b, q: (0, 0, 0)),
        ],
        out_specs=pl.BlockSpec((1, tq, ATT_Q), lambda b, q: (b, q, 0)),
        scratch_shapes=[
            pltpu.VMEM((L, tq), I32),
            pltpu.VMEM((L, tq), I16),
            pltpu.VMEM((L, tq), I16),
            pltpu.VMEM((ATT_HEADS, KV_LATENT, tq), F32),
            pltpu.VMEM((2, ATT_HEADS, 1, tq), F32),
            pltpu.VMEM((ATT_HEADS, 1, tq), F32),
            pltpu.VMEM((2, ATT_HEADS, 1, tq), F32),
            pltpu.VMEM((1, tq), I32),
            pltpu.VMEM((ATT_Q, tq), F32),
            pltpu.VMEM((2, ATT_HEADS, kc, tq), F32),
            pltpu.VMEM((ATT_HEADS, kc, tq), BF16),
        ],
        compiler_params=pltpu.CompilerParams(dimension_semantics=("parallel", "parallel"),
                                             vmem_limit_bytes=VMEM_LIMIT_BYTES),
        name="dsa",
        interpret=interpret,
    )(kidx, c, ct, qidxt, widxt, qlatt, wuvt)


HG_GROUP = 128
HG_PAIRS = HG_HEADS // 2


def _hgrn_kernel(lbl_ref, gn_ref, hq_ref, hf_ref, hi_ref, hg_ref, out_ref, st_ref, *, ts, layer):
    @pl.when(pl.program_id(1) == 0)
    def _():
        st_ref[...] = jnp.zeros(st_ref.shape, F32)

    logits = lbl_ref[...]
    e = jnp.exp(logits - jnp.max(logits, axis=0, keepdims=True))
    lb = jnp.sum(e[0:layer + 1, :], axis=0, keepdims=True) / jnp.sum(e, axis=0, keepdims=True)

    g = HG_GROUP
    chunk_shift = HG_CHUNK.bit_length() - 1
    head_shift = HG_DIM.bit_length() - 1
    r = lax.broadcasted_iota(I32, (g, g), 0)
    cidx = lax.broadcasted_iota(I32, (g, g), 1)
    same_chunk = (r >> chunk_shift) == (cidx >> chunk_shift)
    causal = same_chunk & (cidx <= r)
    tri = jnp.where(causal, 1.0, 0.0).astype(F32)
    ones_blk = jnp.where(same_chunk, 1.0, 0.0).astype(F32)
    same_head = (r >> head_shift) == (cidx >> head_shift)
    head_blk = jnp.where(same_head, 1.0, 0.0).astype(F32)
    lane_lo = lax.broadcasted_iota(I32, (g, LANES), 1) < HG_DIM
    col_chunk = cidx >> chunk_shift
    gn = gn_ref[...]

    def group(gi, carry):
        r0 = pl.multiple_of(gi * g, g)
        rows = pl.ds(r0, g)
        f = lb + (1.0 - lb) * jax.nn.sigmoid(hf_ref[0, rows, :])
        logf = jnp.log(f)
        kgate = 1.0 - f
        b = _dot_exact(tri, logf)
        btot = _dot_exact(ones_blk, logf)
        q_t = jax.nn.silu(hq_ref[0, rows, :]) * jnp.exp(b)
        k_t = kgate * jnp.exp(-b)
        k_dec = kgate * jnp.exp(btot - b)
        decay = jnp.exp(btot)
        v = hi_ref[0, rows, :]
        gate = hg_ref[0, rows, :]
        for p in range(HG_PAIRS):
            ls = slice(p * LANES, (p + 1) * LANES)
            qp = q_t[:, ls]
            kp = k_t[:, ls].astype(BF16)
            vp = v[:, ls].astype(BF16)
            kdp = k_dec[:, ls].astype(BF16)
            q_lo = jnp.where(lane_lo, qp, 0.0).astype(BF16)
            q_hi = jnp.where(lane_lo, 0.0, qp).astype(BF16)
            a_lo = jnp.where(causal, _dot_nt(q_lo, kp), 0.0).astype(BF16)
            a_hi = jnp.where(causal, _dot_nt(q_hi, kp), 0.0).astype(BF16)
            o = jnp.where(lane_lo, _dot(a_lo, vp), _dot(a_hi, vp))
            st = st_ref[p]
            qpb = qp.astype(BF16)
            vt = v[:, ls].T
            inter = []
            for ch in range(g // HG_CHUNK):
                rs = slice(ch * HG_CHUNK, (ch + 1) * HG_CHUNK)
                inter.append(_dot_nt(qpb[rs], st.astype(BF16)))
                vt_ch = jnp.where(col_chunk == ch, vt, 0.0).astype(BF16)
                ds = _dot(vt_ch, kdp)
                st = st * decay[ch * HG_CHUNK:ch * HG_CHUNK + 1, ls] + jnp.where(same_head, ds, 0.0)
            st_ref[p] = st
            o = o + jnp.concatenate(inter, axis=0)
            ms = _dot_exact(o * o, head_blk) * (1.0 / HG_DIM)
            gt = gate[:, ls]
            on = o * lax.rsqrt(ms + EPS) * gn * (gt * jax.nn.sigmoid(gt))
            out_ref[0, rows, ls] = on.astype(BF16)
        return carry

    for gi in range(ts // g):
        group(gi, 0)


def _hgrn(hq, hf, hi, hg, lb_logits, g_norm, *, ts, layer, interpret=False):
    B, L, W = hq.shape
    gn = jnp.tile(g_norm.reshape(1, HG_DIM).astype(F32), (1, 2))
    blk = pl.BlockSpec((1, ts, W), lambda b, t: (b, t, 0))
    return pl.pallas_call(
        functools.partial(_hgrn_kernel, ts=ts, layer=layer),
        out_shape=jax.ShapeDtypeStruct((B, L, W), BF16),
        grid=(B, L // ts),
        in_specs=[pl.BlockSpec(lb_logits.shape, lambda b, t: (0, 0)),
                  pl.BlockSpec((1, LANES), lambda b, t: (0, 0)), blk, blk, blk, blk],
        out_specs=blk,
        scratch_shapes=[pltpu.VMEM((HG_PAIRS, LANES, LANES), F32)],
        compiler_params=pltpu.CompilerParams(dimension_semantics=("parallel", "arbitrary"),
                                             vmem_limit_bytes=VMEM_LIMIT_BYTES),
        name="hgrn2",
        interpret=interpret,
    )(lb_logits.astype(F32), gn, hq, hf, hi, hg)


def _layer_norm(y, g, b):
    mu = jnp.mean(y, axis=-1, keepdims=True)
    d = y - mu
    var = jnp.mean(d * d, axis=-1, keepdims=True)
    return d * lax.rsqrt(var + EPS) * g + b


def _sort_network(n):
    pairs = []

    def merge(lo, hi, r):
        step = r * 2
        if step < hi - lo:
            merge(lo, hi, step)
            merge(lo + r, hi, step)
            pairs.extend((i, i + r) for i in range(lo + r, hi - r, step))
        else:
            pairs.append((lo, lo + r))

    def sort(lo, hi):
        if hi - lo >= 1:
            mid = lo + (hi - lo) // 2
            sort(lo, mid)
            sort(mid + 1, hi)
            merge(lo, hi, 1)

    sort(0, n - 1)
    return pairs


SUBLANES = 8


def _top_sorted(s, n):
    regs = [s[r:r + SUBLANES] for r in range(0, s.shape[0], SUBLANES)]
    for i, j in _sort_network(len(regs)):
        regs[i], regs[j] = jnp.maximum(regs[i], regs[j]), jnp.minimum(regs[i], regs[j])
    vals = []
    for it in range(n):
        mx = jnp.max(regs[0], axis=0, keepdims=True)
        vals.append(mx)
        if it + 1 < n:
            took = regs[0] == mx
            depth = min(len(regs), n - it)
            for d in range(depth - 1):
                regs[d] = jnp.where(took, regs[d + 1], regs[d])
            regs[depth - 1] = jnp.where(took, regs[depth] if depth < len(regs) else -jnp.inf, regs[depth - 1])
    return vals


PEER_KEEP = PEER_TOPK + 1
_PAIR_RANKS = [(i, j) for i in range(1, PEER_KEEP + 1) for j in range(1, PEER_KEEP + 1) if i * j <= PEER_KEEP]
PEER_CAND_ROWS = SUBLANES * (1 << (-(-len(_PAIR_RANKS) // SUBLANES) - 1).bit_length())


def _route_kernel(att_ref, hg_ref, x_ref, woa_ref, wob_ref, g1_ref, b1_ref, wq_ref, sk_ref,
                  h_ref, hb_ref, cnt_ref, at_ref, rk_ref, bm_ref, *, alpha):
    mix = _dot(att_ref[...], woa_ref[...]) + _dot(hg_ref[...], wob_ref[...])
    h = _layer_norm(alpha * x_ref[...] + mix, g1_ref[...], b1_ref[...])
    h_ref[...] = h
    hb = h.astype(BF16)
    hb_ref[...] = hb
    qp = _dot(hb, wq_ref[...]).astype(BF16)
    dk = PEER_N_KEYS
    tb = h.shape[0]
    for hd in range(PEER_HEADS):
        s1_all = _dot_nt(sk_ref[hd, 0], qp[:, (2 * hd) * dk:(2 * hd + 1) * dk])
        s2_all = _dot_nt(sk_ref[hd, 1], qp[:, (2 * hd + 1) * dk:(2 * hd + 2) * dk])
        for sub in range(tb // LANES):
            ls = slice(sub * LANES, (sub + 1) * LANES)
            s1, s2 = s1_all[:, ls], s2_all[:, ls]
            a = _top_sorted(s1, PEER_KEEP)
            bb = _top_sorted(s2, PEER_KEEP)
            sums = {(i, j): a[i - 1] + bb[j - 1] for i, j in _PAIR_RANKS}
            cands = list(sums.values())
            pad = PEER_CAND_ROWS - len(cands)
            cand = jnp.concatenate(cands + [jnp.full_like(cands[0], -jnp.inf)] * pad, axis=0)
            top = _top_sorted(cand, PEER_KEEP)
            c_max, c_k, c_next = top[0], top[PEER_TOPK - 1], top[PEER_TOPK]
            thr = 0.5 * (c_k + c_next)
            z = jnp.sum(jnp.where(cand >= c_k, jnp.exp(cand - c_max), 0.0), axis=0, keepdims=True)
            rank2 = jnp.full(s2.shape, float(PEER_KEEP), F32)
            cnt = jnp.zeros(s1.shape, F32)
            for q in range(PEER_KEEP, 0, -1):
                rank2 = jnp.where(s2 == bb[q - 1], float(q - 1), rank2)
                n_q = sum(jnp.where(sums[(q, j)] >= thr, 1.0, 0.0) for j in range(1, PEER_KEEP // q + 1))
                cnt = jnp.where(s1 == a[q - 1], n_q, cnt)
            cnt_ref[hd, :, ls] = cnt
            at_ref[hd, :, ls] = jnp.exp(s1 - a[0]) / z
            rk_ref[hd, :, ls] = rank2.astype(BF16)
            bm_ref[hd, :, ls] = jnp.exp(s2 - bb[0]).astype(BF16)


def _route(att, hgo, x2, w_out, g1, b1, w_q, sub_keys, *, tb, alpha, interpret=False):
    T, D = x2.shape
    woa = w_out[:ATT_Q].astype(BF16)
    wob = w_out[ATT_Q:].astype(BF16)

    def full(a):
        zeros = (0,) * a.ndim
        return pl.BlockSpec(a.shape, lambda t: zeros)

    ins = [att, hgo, x2, woa, wob, g1.reshape(1, D).astype(F32), b1.reshape(1, D).astype(F32),
           w_q.astype(BF16), sub_keys.astype(BF16)]
    in_specs = [pl.BlockSpec((tb, ATT_Q), lambda t: (t, 0)), pl.BlockSpec((tb, HG_W), lambda t: (t, 0)),
                pl.BlockSpec((tb, D), lambda t: (t, 0))] + [full(a) for a in ins[3:]]
    rt_shape = lambda dt: jax.ShapeDtypeStruct((PEER_HEADS, PEER_N_KEYS, T), dt)
    rt_spec = pl.BlockSpec((PEER_HEADS, PEER_N_KEYS, tb), lambda t: (0, 0, t))
    return pl.pallas_call(
        functools.partial(_route_kernel, alpha=alpha),
        out_shape=[jax.ShapeDtypeStruct((T, D), F32), jax.ShapeDtypeStruct((T, D), BF16),
                   rt_shape(F32), rt_shape(F32), rt_shape(BF16), rt_shape(BF16)],
        grid=(T // tb,),
        in_specs=in_specs,
        out_specs=[pl.BlockSpec((tb, D), lambda t: (t, 0)), pl.BlockSpec((tb, D), lambda t: (t, 0)),
                   rt_spec, rt_spec, rt_spec, rt_spec],
        compiler_params=pltpu.CompilerParams(dimension_semantics=("parallel",),
                                             vmem_limit_bytes=VMEM_LIMIT_BYTES),
        name="route",
        interpret=interpret,
    )(*ins)


GELU_K = 2.0 * (2.0 / 3.141592653589793) ** 0.5


def _gelu_tanh(x):
    z = x * ((-GELU_K * LOG2E) + (-GELU_K * 0.044715 * LOG2E) * (x * x))
    return x / (1.0 + jnp.exp2(z))


def _peer_kernel(hb_ref, h_ref, u_ref, vt_ref, cnt_ref, at_ref, rk_ref, bm_ref, g2_ref, b2_ref,
                 out_ref, acc_ref, act_ref, p_ref, *, ic, tw, alpha):
    c = pl.program_id(1)

    @pl.when(c == 0)
    def _():
        acc_ref[...] = jnp.zeros(acc_ref.shape, F32)

    nk = PEER_N_KEYS
    tb = hb_ref.shape[0]
    grp = nk // BF16_ROWS
    act_ref[...] = _gelu_tanh(_dot_nt(u_ref[...], hb_ref[...]).astype(BF16))
    for ii in range(ic):
        rows = slice(ii * nk, (ii + 1) * nk)
        for s in range(tb // tw):
            ls = slice(s * tw, (s + 1) * tw)
            gsum = jnp.zeros((grp, BF16_ROWS, tw), BF16)
            for hd in range(PEER_HEADS):
                cnt = jnp.broadcast_to(cnt_ref[hd, ii:ii + 1, ls], (BF16_ROWS, tw)).astype(BF16)
                wgt = jnp.broadcast_to(at_ref[hd, ii:ii + 1, ls], (BF16_ROWS, tw)).astype(BF16)
                rk = rk_ref[hd, :, ls].reshape(grp, BF16_ROWS, tw)
                bm = bm_ref[hd, :, ls].reshape(grp, BF16_ROWS, tw)
                gsum = gsum + jnp.where(rk < cnt[None], bm, jnp.zeros_like(bm)) * wgt[None]
            p_ref[rows, ls] = gsum.reshape(nk, tw) * act_ref[rows, ls]
    acc_ref[...] += _dot(vt_ref[...], p_ref[...])

    @pl.when(c == pl.num_programs(1) - 1)
    def _():
        y = alpha * h_ref[...] + acc_ref[...].T
        out_ref[...] = _layer_norm(y, g2_ref[...], b2_ref[...])


def _peer(hb, h, u, v, cnt, at, rk, bm, g2, b2, *, tb, ic, tw, alpha, interpret=False):
    T, D = h.shape
    n_exp = u.shape[0]
    ub = u.astype(BF16)
    vtb = v.astype(BF16).T
    ec = ic * PEER_N_KEYS
    rows = pl.BlockSpec((PEER_HEADS, ic, tb), lambda t, c: (0, c, t))
    whole = pl.BlockSpec((PEER_HEADS, PEER_N_KEYS, tb), lambda t, c: (0, 0, t))
    vec = pl.BlockSpec((1, D), lambda t, c: (0, 0))
    return pl.pallas_call(
        functools.partial(_peer_kernel, ic=ic, tw=tw, alpha=alpha),
        out_shape=jax.ShapeDtypeStruct((T, D), F32),
        grid=(T // tb, n_exp // ec),
        in_specs=[pl.BlockSpec((tb, D), lambda t, c: (t, 0)), pl.BlockSpec((tb, D), lambda t, c: (t, 0)),
                  pl.BlockSpec((ec, D), lambda t, c: (c, 0)), pl.BlockSpec((D, ec), lambda t, c: (0, c)),
                  rows, rows, whole, whole, vec, vec],
        out_specs=pl.BlockSpec((tb, D), lambda t, c: (t, 0)),
        scratch_shapes=[pltpu.VMEM((D, tb), F32), pltpu.VMEM((ec, tb), BF16), pltpu.VMEM((ec, tb), BF16)],
        compiler_params=pltpu.CompilerParams(dimension_semantics=("parallel", "arbitrary"),
                                             vmem_limit_bytes=VMEM_LIMIT_BYTES),
        name="peer",
        interpret=interpret,
    )(hb, h, ub, vtb, cnt, at, rk, bm, g2.reshape(1, D).astype(F32), b2.reshape(1, D).astype(F32))


def _tile(n, pref):
    t = min(pref, n)
    assert n % t == 0, (n, t)
    return t


PROJ_TOKENS = 512
DSA_QUERIES = 256
HGRN_TOKENS = 512
ROUTE_TOKENS = 512
PEER_TOKENS = 512
PEER_I_PER_STEP = 16
PEER_GATE_LANES = 256


def kernel(x, w_in, kv_norm_g, w_uk, w_uv, hg_lb_logits, hg_norm_g, w_out, ln1_g, ln1_b,
           peer_w_q, peer_sub_keys, peer_u, peer_v, ln2_g, ln2_b):
    B, L, D = x.shape
    depth = w_in.shape[0]
    alpha = (2.0 * depth) ** 0.25
    T = B * L
    for layer in range(depth):
        (c, ct, kidx, qidxt, widxt, qlatt, hq, hf, hi, hg) = _proj(
            x, w_in[layer], kv_norm_g[layer], w_uk[layer], tb=_tile(L, PROJ_TOKENS))
        tq = _tile(L, DSA_QUERIES)
        att = _dsa(kidx, c, ct, qidxt, widxt, qlatt, w_uv[layer], tq=tq, ks=tq, kc=tq)
        hgo = _hgrn(hq, hf, hi, hg, hg_lb_logits, hg_norm_g[layer], ts=_tile(L, HGRN_TOKENS), layer=layer)
        h, hb, rcnt, rat, rrk, rbm = _route(
            att.reshape(T, ATT_Q), hgo.reshape(T, HG_W), x.reshape(T, D), w_out[layer], ln1_g[layer], ln1_b[layer],
            peer_w_q[layer], peer_sub_keys[layer], tb=_tile(T, ROUTE_TOKENS), alpha=alpha)
        y = _peer(hb, h, peer_u[layer], peer_v[layer], rcnt, rat, rrk, rbm, ln2_g[layer], ln2_b[layer],
                  tb=_tile(T, PEER_TOKENS), ic=PEER_I_PER_STEP, tw=PEER_GATE_LANES, alpha=alpha)
        x = y.reshape(B, L, D)
    return x
```

```python
import functools

import jax
import jax.numpy as jnp
from jax import lax
from jax.experimental import pallas as pl
from jax.experimental.pallas import tpu as pltpu

F32 = jnp.float32
BF16 = jnp.bfloat16
I32 = jnp.int32
I16 = jnp.int16

ATT_HEADS = 8
ATT_HEAD_DIM = 64
KV_LATENT = 128
IDX_HEADS = 8
IDX_DIM = 64
TOPK_MAX = 256
HG_HEADS = 8
HG_DIM = 64
HG_CHUNK = 32
PEER_HEADS = 8
PEER_N_KEYS = 128
PEER_TOPK = 16
EPS = 1e-5

ATT_Q = ATT_HEADS * ATT_HEAD_DIM
IDX_Q = IDX_HEADS * IDX_DIM
HG_W = HG_HEADS * HG_DIM

LANES = 128
VMEM_LIMIT_BYTES = 56 * 1024 * 1024

LOG2E = 1.4426950408889634
NEG_BIG = -1e30
INT_MIN = -(2 ** 31)
HALF16 = 1 << 15
BF16_ROWS = 16
KEY_NEG_INF = INT_MIN + 0x7FFFFF


def _dot(a, b):
    return jnp.dot(a, b, preferred_element_type=F32)


def _dot_nt(a, b):
    return lax.dot_general(a, b, (((1,), (1,)), ((), ())), preferred_element_type=F32)


def _split3(x):
    hi = x.astype(BF16)
    r = x - hi.astype(F32)
    mid = r.astype(BF16)
    lo = (r - mid.astype(F32)).astype(BF16)
    return hi, mid, lo


def _proj_kernel(x_ref, wc_ref, wk_ref, whq_ref, whf_ref, whi_ref, whg_ref,
                 wtq_ref, wtqi_ref, wtwi_ref, wtc_ref, grow_ref, gcol_ref, wuk_ref,
                 c_ref, ct_ref, kidx_ref, qidxt_ref, widxt_ref, qlatt_ref,
                 hq_ref, hf_ref, hi_ref, hg_ref):
    xb = x_ref[0].astype(BF16)

    c = _dot(xb, wc_ref[...])
    c = c * lax.rsqrt(jnp.mean(c * c, axis=-1, keepdims=True) + EPS) * grow_ref[...]
    c_ref[0] = c.astype(BF16)
    ct = _dot_nt(wtc_ref[...], xb)
    ct = ct * lax.rsqrt(jnp.mean(ct * ct, axis=0, keepdims=True) + EPS) * gcol_ref[...]
    ct_ref[0] = ct.astype(BF16)

    kidx_ref[0] = _dot(xb, wk_ref[...]).astype(BF16)
    qidxt_ref[0] = _dot_nt(wtqi_ref[...], xb).astype(BF16)
    widxt_ref[0] = _dot_nt(wtwi_ref[...], xb)

    qt = _dot_nt(wtq_ref[...], xb).astype(BF16)
    scale = ATT_HEAD_DIM ** -0.5 * LOG2E
    for h in range(ATT_HEADS):
        ql = _dot(wuk_ref[h], qt[h * ATT_HEAD_DIM:(h + 1) * ATT_HEAD_DIM, :]) * scale
        qlatt_ref[0, h * KV_LATENT:(h + 1) * KV_LATENT, :] = ql.astype(BF16)

    hq_ref[0] = _dot(xb, whq_ref[...])
    hf_ref[0] = _dot(xb, whf_ref[...])
    hi_ref[0] = _dot(xb, whi_ref[...])
    hg_ref[0] = _dot(xb, whg_ref[...])


def _proj(x, w, kv_g, w_uk, *, tb, interpret=False):
    B, L, D = x.shape
    o = 0
    cols = {}
    for name, width in (("q", ATT_Q), ("c", KV_LATENT), ("qi", IDX_Q), ("k", IDX_DIM), ("wi", IDX_HEADS),
                        ("hq", HG_W), ("hf", HG_W), ("hi", HG_W), ("hg", HG_W)):
        cols[name] = w[:, o:o + width].astype(BF16)
        o += width
    assert o == w.shape[1]
    wt = lambda n: cols[n].T
    ins = [x, cols["c"], cols["k"], cols["hq"], cols["hf"], cols["hi"], cols["hg"],
           wt("q"), wt("qi"), wt("wi"), wt("c"),
           kv_g.reshape(1, KV_LATENT).astype(F32), kv_g.reshape(KV_LATENT, 1).astype(F32),
           w_uk.astype(BF16)]

    def full(a):
        zeros = (0,) * a.ndim
        return pl.BlockSpec(a.shape, lambda b, t: zeros)

    in_specs = [pl.BlockSpec((1, tb, D), lambda b, t: (b, t, 0))] + [full(a) for a in ins[1:]]
    nat = lambda wd, dt: (jax.ShapeDtypeStruct((B, L, wd), dt), pl.BlockSpec((1, tb, wd), lambda b, t: (b, t, 0)))
    tr = lambda wd, dt: (jax.ShapeDtypeStruct((B, wd, L), dt), pl.BlockSpec((1, wd, tb), lambda b, t: (b, 0, t)))
    outs = [nat(KV_LATENT, BF16), tr(KV_LATENT, BF16), nat(IDX_DIM, BF16), tr(IDX_Q, BF16), tr(IDX_HEADS, F32),
            tr(ATT_HEADS * KV_LATENT, BF16), nat(HG_W, F32), nat(HG_W, F32), nat(HG_W, F32), nat(HG_W, F32)]
    return pl.pallas_call(
        _proj_kernel,
        out_shape=[s for s, _ in outs],
        grid=(B, L // tb),
        in_specs=in_specs,
        out_specs=[s for _, s in outs],
        compiler_params=pltpu.CompilerParams(dimension_semantics=("parallel", "parallel"),
                                             vmem_limit_bytes=VMEM_LIMIT_BYTES),
        name="proj",
        interpret=interpret,
    )(*ins)


def _dsa_kernel(kidx_ref, c_ref, ct_ref, qidxt_ref, widxt_ref, qlatt_ref, wuvt_ref,
                out_ref, keys_ref, khi_ref, klo_ref, acc_ref, m_ref, l_ref, alpha_ref, jb_ref, ot_ref, a_ref, p_ref,
                *, tq, ks, kc, top_k, seq):
    qi = pl.program_id(1)
    n_keys = (qi + 1) * tq
    nks = (qi + 1) * (tq // ks)
    nkc = (qi + 1) * (tq // kc)
    q_pos = lambda rows: qi * tq + lax.broadcasted_iota(I32, (rows, tq), 1)
    k_off = lambda rows: lax.broadcasted_iota(I32, (rows, tq), 0)

    def score_chunk(k, carry):
        r0 = pl.multiple_of(k * ks, ks)
        kk = kidx_ref[0, pl.ds(r0, ks), :]
        s = jnp.zeros((ks, tq), F32)
        for h in range(IDX_HEADS):
            z = _dot(kk, qidxt_ref[0, h * IDX_DIM:(h + 1) * IDX_DIM, :])
            s = s + widxt_ref[0, h:h + 1, :] * jnp.maximum(z, 0.0)
        s = jnp.where(r0 + k_off(ks) <= q_pos(ks), s, -jnp.inf)
        bits = pltpu.bitcast(s, I32)
        key = bits ^ ((bits >> 31) & 0x7FFFFFFF)
        keys_ref[pl.ds(r0, ks), :] = key
        khi_ref[pl.ds(r0, ks), :] = (key >> 16).astype(I16)
        klo_ref[pl.ds(r0, ks), :] = ((key & 0xFFFF) - HALF16).astype(I16)
        return carry

    lax.fori_loop(0, nks, score_chunk, 0)

    kcc = 2 * kc
    nkcc = (nkc + 1) >> 1

    @pl.when(n_keys < seq)
    def _():
        pad_rows = pl.ds(pl.multiple_of(n_keys, kc), kc)
        khi_ref[pad_rows, :] = jnp.full((kc, tq), -HALF16, I16)
        klo_ref[pad_rows, :] = jnp.full((kc, tq), -HALF16, I16)

    def count(pred):
        def body(k, cnt):
            r0 = pl.multiple_of(k * kc, kc)
            blk = keys_ref[pl.ds(r0, kc), :]
            return cnt + jnp.sum(jnp.where(pred(blk, r0), 1, 0).astype(I32), axis=0, keepdims=True)
        return lax.fori_loop(0, nkc, body, jnp.zeros((1, tq), I32))

    def count16(ref, cand):
        c16 = cand.astype(I16)

        def body(k, cnt):
            r0 = pl.multiple_of(k * kcc, kcc)
            blk = ref[pl.ds(r0, kcc), :]
            ones = jnp.where(blk >= c16, jnp.ones(blk.shape, BF16), jnp.zeros(blk.shape, BF16))
            parts = [ones[g * BF16_ROWS:(g + 1) * BF16_ROWS] for g in range(kcc // BF16_ROWS)]
            while len(parts) > 1:
                parts = [parts[i] + parts[i + 1] for i in range(0, len(parts), 2)]
            return cnt + parts[0].astype(F32)
        cnt = lax.fori_loop(0, nkcc, body, jnp.zeros((BF16_ROWS, tq), F32))
        return jnp.sum(cnt, axis=0, keepdims=True).astype(I32)

    def hi_step(i, carry):
        t, n_t = carry
        cand = t + lax.shift_left(jnp.int32(1), 15 - i)
        cnt = count16(khi_ref, cand)
        take = cnt >= top_k
        return jnp.where(take, cand, t), jnp.where(take, cnt, n_t)

    h, n_ge_h = lax.fori_loop(0, 16, hi_step,
                              (jnp.full((1, tq), -HALF16, I32), jnp.full((1, tq), n_keys, I32)))
    h16 = h.astype(I16)

    def mark_boundary(k, carry):
        rows = pl.ds(pl.multiple_of(k * kcc, kcc), kcc)
        hi = khi_ref[rows, :]
        klo_ref[rows, :] = jnp.where(hi > h16, jnp.int16(HALF16 - 1),
                                     jnp.where(hi == h16, klo_ref[rows, :], jnp.int16(-HALF16)))
        return carry

    lax.fori_loop(0, nkcc, mark_boundary, 0)

    def lo_bit(i, carry):
        t, n_t = carry
        cand = t + lax.shift_left(jnp.int32(1), 15 - i)
        cnt = count16(klo_ref, cand)
        take = cnt >= top_k
        return jnp.where(take, cand, t), jnp.where(take, cnt, n_t)

    def lo_cond(carry):
        r, _, _, open_lanes = carry
        return jnp.logical_and(r < 4, open_lanes > 0)

    def lo_round(carry):
        r, t, n_t, _ = carry
        t, n_t = lax.fori_loop(r * 4, r * 4 + 4, lo_bit, (t, n_t))
        return r + 1, t, n_t, jnp.max(jnp.abs(n_t - top_k))

    _, t_lo, n_ge, _ = lax.while_loop(
        lo_cond, lo_round,
        (jnp.int32(0), jnp.full((1, tq), -HALF16, I32), n_ge_h, jnp.max(jnp.abs(n_ge_h - top_k))))
    thr = lax.shift_left(h, 16) + (t_lo + HALF16)

    has_tie = jnp.where(n_ge > top_k, jnp.where(thr > KEY_NEG_INF, 1, 0), 0)
    any_tie = jnp.max(has_tie)
    jb_ref[...] = jnp.full((1, tq), -1, I32)

    @pl.when(any_tie > 0)
    def _():
        need = top_k - count(lambda blk, r0: blk > thr)

        def idx_step(i, v):
            cand = v + lax.shift_left(jnp.int32(1), (seq.bit_length() - 1) - i)
            cnt = count(lambda blk, r0: (blk == thr) & (r0 + k_off(kc) < cand))
            return jnp.where(cnt < need, cand, v)
        v = lax.fori_loop(0, seq.bit_length(), idx_step, jnp.zeros((1, tq), I32))
        jb_ref[...] = jnp.where(has_tie > 0, v, -1)

    jb = jb_ref[...]
    gt_thr = jnp.where(has_tie > 0, thr, jnp.maximum(thr - 1, KEY_NEG_INF))

    l_ref[...] = jnp.zeros(l_ref.shape, F32)
    acc_ref[...] = jnp.zeros(acc_ref.shape, F32)

    def masked_bias(r0):
        key = keys_ref[pl.ds(r0, kc), :]
        return lax.cond(
            any_tie > 0,
            lambda: jnp.where(key > gt_thr, 0.0,
                              jnp.where(key == thr, jnp.where(r0 + k_off(kc) <= jb, 0.0, NEG_BIG), NEG_BIG)),
            lambda: jnp.where(key > gt_thr, 0.0, NEG_BIG))

    def logits(h, cc, bias, m_prev, slot):
        a = _dot(cc, qlatt_ref[0, h * KV_LATENT:(h + 1) * KV_LATENT, :]) + bias
        a_ref[slot, h] = a
        m_new = jnp.maximum(m_prev, jnp.max(a, axis=0, keepdims=True))
        alpha_ref[slot, h] = jnp.exp2(m_prev - m_new)
        m_ref[slot, h] = m_new

    bias0 = masked_bias(0)
    cc0 = c_ref[0, pl.ds(0, kc), :]
    for h in range(ATT_HEADS):
        logits(h, cc0, bias0, jnp.full((1, tq), NEG_BIG, F32), 0)

    def att_step(k, cur):
        nxt = 1 - cur
        r0 = pl.multiple_of(k * kc, kc)
        rn = pl.multiple_of(jnp.minimum(k + 1, nkc - 1) * kc, kc)
        bias_n = masked_bias(rn)
        cc_n = c_ref[0, pl.ds(rn, kc), :]
        cct = ct_ref[0, :, pl.ds(r0, kc)]
        for h in range(ATT_HEADS):
            m_cur = m_ref[cur, h]
            logits(h, cc_n, bias_n, m_cur, nxt)
            p = jnp.exp2(a_ref[cur, h] - m_cur)
            l_ref[h] = alpha_ref[cur, h] * l_ref[h] + jnp.sum(p, axis=0, keepdims=True)
            p_ref[h] = p.astype(BF16)
            acc_ref[h] = alpha_ref[cur, h] * acc_ref[h] + _dot(cct, p_ref[h])

    def att_chunk(k, carry):
        for parity in range(2):
            pl.when(k % 2 == parity)(functools.partial(att_step, k, parity))
        return carry

    lax.fori_loop(0, nkc, att_chunk, 0)

    for h in range(ATT_HEADS):
        o_lat = (acc_ref[h] / l_ref[h]).astype(BF16)
        ot_ref[h * ATT_HEAD_DIM:(h + 1) * ATT_HEAD_DIM, :] = _dot(wuvt_ref[h], o_lat)
    out_ref[0] = ot_ref[...].T.astype(BF16)


def _dsa(kidx, c, ct, qidxt, widxt, qlatt, w_uv, *, tq, ks, kc, interpret=False):
    B, L, _ = c.shape
    top_k = min(TOPK_MAX, L // 4)
    assert tq % ks == 0 and tq % kc == 0 and top_k <= tq and L % (2 * kc) == 0 and 2 * kc // BF16_ROWS <= 256
    wuvt = jnp.swapaxes(w_uv, 1, 2).astype(BF16)
    kernel = functools.partial(_dsa_kernel, tq=tq, ks=ks, kc=kc, top_k=top_k, seq=L)
    return pl.pallas_call(
        kernel,
        out_shape=jax.ShapeDtypeStruct((B, L, ATT_Q), BF16),
        grid=(B, L // tq),
        in_specs=[
            pl.BlockSpec((1, L, IDX_DIM), lambda b, q: (b, 0, 0)),
            pl.BlockSpec((1, L, KV_LATENT), lambda b, q: (b, 0, 0)),
            pl.BlockSpec((1, KV_LATENT, L), lambda b, q: (b, 0, 0)),
            pl.BlockSpec((1, IDX_Q, tq), lambda b, q: (b, 0, q)),
            pl.BlockSpec((1, IDX_HEADS, tq), lambda b, q: (b, 0, q)),
            pl.BlockSpec((1, ATT_HEADS * KV_LATENT, tq), lambda b, q: (b, 0, q)),
            pl.BlockSpec((ATT_HEADS, ATT_HEAD_DIM, KV_LATENT), lambda b, q: (0, 0, 0)),
        ],
        out_specs=pl.BlockSpec((1, tq, ATT_Q), lambda b, q: (b, q, 0)),
        scratch_shapes=[
            pltpu.VMEM((L, tq), I32),
            pltpu.VMEM((L, tq), I16),
            pltpu.VMEM((L, tq), I16),
            pltpu.VMEM((ATT_HEADS, KV_LATENT, tq), F32),
            pltpu.VMEM((2, ATT_HEADS, 1, tq), F32),
            pltpu.VMEM((ATT_HEADS, 1, tq), F32),
            pltpu.VMEM((2, ATT_HEADS, 1, tq), F32),
            pltpu.VMEM((1, tq), I32),
            pltpu.VMEM((ATT_Q, tq), F32),
            pltpu.VMEM((2, ATT_HEADS, kc, tq), F32),
            pltpu.VMEM((ATT_HEADS, kc, tq), BF16),
        ],
        compiler_params=pltpu.CompilerParams(dimension_semantics=("parallel", "parallel"),
                                             vmem_limit_bytes=VMEM_LIMIT_BYTES),
        name="dsa",
        interpret=interpret,
    )(kidx, c, ct, qidxt, widxt, qlatt, wuvt)


HG_GROUP = 128
HG_PAIRS = HG_HEADS // 2


def _hgrn_kernel(lbl_ref, gn_ref, hq_ref, hf_ref, hi_ref, hg_ref, out_ref, st_ref, *, ts, layer):
    @pl.when(pl.program_id(1) == 0)
    def _():
        st_ref[...] = jnp.zeros(st_ref.shape, F32)

    logits = lbl_ref[...]
    e = jnp.exp(logits - jnp.max(logits, axis=0, keepdims=True))
    lb = jnp.sum(e[0:layer + 1, :], axis=0, keepdims=True) / jnp.sum(e, axis=0, keepdims=True)

    g = HG_GROUP
    chunk_shift = HG_CHUNK.bit_length() - 1
    head_shift = HG_DIM.bit_length() - 1
    r = lax.broadcasted_iota(I32, (g, g), 0)
    cidx = lax.broadcasted_iota(I32, (g, g), 1)
    same_chunk = (r >> chunk_shift) == (cidx >> chunk_shift)
    causal = same_chunk & (cidx <= r)
    tri = jnp.where(causal, 1.0, 0.0).astype(BF16)
    ones_blk = jnp.where(same_chunk, 1.0, 0.0).astype(BF16)
    same_head = (r >> head_shift) == (cidx >> head_shift)
    head_blk = jnp.where(same_head, 1.0, 0.0).astype(BF16)
    lane_lo = lax.broadcasted_iota(I32, (g, LANES), 1) < HG_DIM
    col_chunk = cidx >> chunk_shift
    gn = gn_ref[...]

    def group(gi, carry):
        r0 = pl.multiple_of(gi * g, g)
        rows = pl.ds(r0, g)
        f = lb + (1.0 - lb) * jax.nn.sigmoid(hf_ref[0, rows, :])
        logf = jnp.log(f)
        kgate = 1.0 - f
        parts = _split3(logf)
        b = sum(_dot(tri, t) for t in parts)
        btot = sum(_dot(ones_blk, t) for t in parts)
        q_t = jax.nn.silu(hq_ref[0, rows, :]) * jnp.exp(b)
        k_t = kgate * jnp.exp(-b)
        k_dec = kgate * jnp.exp(btot - b)
        decay = jnp.exp(btot)
        v = hi_ref[0, rows, :]
        gate = hg_ref[0, rows, :]
        for p in range(HG_PAIRS):
            ls = slice(p * LANES, (p + 1) * LANES)
            qp = q_t[:, ls]
            kp = k_t[:, ls].astype(BF16)
            vp = v[:, ls].astype(BF16)
            kdp = k_dec[:, ls].astype(BF16)
            q_lo = jnp.where(lane_lo, qp, 0.0).astype(BF16)
            q_hi = jnp.where(lane_lo, 0.0, qp).astype(BF16)
            a_lo = jnp.where(causal, _dot_nt(q_lo, kp), 0.0).astype(BF16)
            a_hi = jnp.where(causal, _dot_nt(q_hi, kp), 0.0).astype(BF16)
            o = jnp.where(lane_lo, _dot(a_lo, vp), _dot(a_hi, vp))
            st = st_ref[p]
            qpb = qp.astype(BF16)
            vt = v[:, ls].T
            inter = []
            for ch in range(g // HG_CHUNK):
                rs = slice(ch * HG_CHUNK, (ch + 1) * HG_CHUNK)
                inter.append(_dot_nt(qpb[rs], st.astype(BF16)))
                vt_ch = jnp.where(col_chunk == ch, vt, 0.0).astype(BF16)
                ds = _dot(vt_ch, kdp)
                st = st * decay[ch * HG_CHUNK:ch * HG_CHUNK + 1, ls] + jnp.where(same_head, ds, 0.0)
            st_ref[p] = st
            o = o + jnp.concatenate(inter, axis=0)
            ms = sum(_dot(t, head_blk) for t in _split3(o * o)) * (1.0 / HG_DIM)
            gt = gate[:, ls]
            on = o * lax.rsqrt(ms + EPS) * gn * (gt * jax.nn.sigmoid(gt))
            out_ref[0, rows, ls] = on.astype(BF16)
        return carry

    for gi in range(ts // g):
        group(gi, 0)


def _hgrn(hq, hf, hi, hg, lb_logits, g_norm, *, ts, layer, interpret=False):
    B, L, W = hq.shape
    gn = jnp.tile(g_norm.reshape(1, HG_DIM).astype(F32), (1, 2))
    blk = pl.BlockSpec((1, ts, W), lambda b, t: (b, t, 0))
    return pl.pallas_call(
        functools.partial(_hgrn_kernel, ts=ts, layer=layer),
        out_shape=jax.ShapeDtypeStruct((B, L, W), BF16),
        grid=(B, L // ts),
        in_specs=[pl.BlockSpec(lb_logits.shape, lambda b, t: (0, 0)),
                  pl.BlockSpec((1, LANES), lambda b, t: (0, 0)), blk, blk, blk, blk],
        out_specs=blk,
        scratch_shapes=[pltpu.VMEM((HG_PAIRS, LANES, LANES), F32)],
        compiler_params=pltpu.CompilerParams(dimension_semantics=("parallel", "arbitrary"),
                                             vmem_limit_bytes=VMEM_LIMIT_BYTES),
        name="hgrn2",
        interpret=interpret,
    )(lb_logits.astype(F32), gn, hq, hf, hi, hg)


def _layer_norm(y, g, b):
    mu = jnp.mean(y, axis=-1, keepdims=True)
    d = y - mu
    var = jnp.mean(d * d, axis=-1, keepdims=True)
    return d * lax.rsqrt(var + EPS) * g + b


def _sort_network(n):
    pairs = []

    def merge(lo, hi, r):
        step = r * 2
        if step < hi - lo:
            merge(lo, hi, step)
            merge(lo + r, hi, step)
            pairs.extend((i, i + r) for i in range(lo + r, hi - r, step))
        else:
            pairs.append((lo, lo + r))

    def sort(lo, hi):
        if hi - lo >= 1:
            mid = lo + (hi - lo) // 2
            sort(lo, mid)
            sort(mid + 1, hi)
            merge(lo, hi, 1)

    sort(0, n - 1)
    return pairs


SUBLANES = 8


def _top_sorted(s, n):
    regs = [s[r:r + SUBLANES] for r in range(0, s.shape[0], SUBLANES)]
    for i, j in _sort_network(len(regs)):
        regs[i], regs[j] = jnp.maximum(regs[i], regs[j]), jnp.minimum(regs[i], regs[j])
    vals = []
    for it in range(n):
        mx = jnp.max(regs[0], axis=0, keepdims=True)
        vals.append(mx)
        if it + 1 < n:
            took = regs[0] == mx
            depth = min(len(regs), n - it)
            for d in range(depth - 1):
                regs[d] = jnp.where(took, regs[d + 1], regs[d])
            regs[depth - 1] = jnp.where(took, regs[depth] if depth < len(regs) else -jnp.inf, regs[depth - 1])
    return vals


PEER_KEEP = PEER_TOPK + 1
_PAIR_RANKS = [(i, j) for i in range(1, PEER_KEEP + 1) for j in range(1, PEER_KEEP + 1) if i * j <= PEER_KEEP]
PEER_CAND_ROWS = SUBLANES * (1 << (-(-len(_PAIR_RANKS) // SUBLANES) - 1).bit_length())


def _route_kernel(att_ref, hg_ref, x_ref, woa_ref, wob_ref, g1_ref, b1_ref, wq_ref, sk_ref,
                  h_ref, hb_ref, cnt_ref, at_ref, rk_ref, bm_ref, *, alpha):
    mix = _dot(att_ref[...], woa_ref[...]) + _dot(hg_ref[...], wob_ref[...])
    h = _layer_norm(alpha * x_ref[...] + mix, g1_ref[...], b1_ref[...])
    h_ref[...] = h
    hb = h.astype(BF16)
    hb_ref[...] = hb
    qp = _dot(hb, wq_ref[...]).astype(BF16)
    dk = PEER_N_KEYS
    tb = h.shape[0]
    for hd in range(PEER_HEADS):
        s1_all = _dot_nt(sk_ref[hd, 0], qp[:, (2 * hd) * dk:(2 * hd + 1) * dk])
        s2_all = _dot_nt(sk_ref[hd, 1], qp[:, (2 * hd + 1) * dk:(2 * hd + 2) * dk])
        for sub in range(tb // LANES):
            ls = slice(sub * LANES, (sub + 1) * LANES)
            s1, s2 = s1_all[:, ls], s2_all[:, ls]
            a = _top_sorted(s1, PEER_KEEP)
            bb = _top_sorted(s2, PEER_KEEP)
            sums = {(i, j): a[i - 1] + bb[j - 1] for i, j in _PAIR_RANKS}
            cands = list(sums.values())
            pad = PEER_CAND_ROWS - len(cands)
            cand = jnp.concatenate(cands + [jnp.full_like(cands[0], -jnp.inf)] * pad, axis=0)
            top = _top_sorted(cand, PEER_KEEP)
            c_max, c_k, c_next = top[0], top[PEER_TOPK - 1], top[PEER_TOPK]
            thr = 0.5 * (c_k + c_next)
            z = jnp.sum(jnp.where(cand >= c_k, jnp.exp(cand - c_max), 0.0), axis=0, keepdims=True)
            rank2 = jnp.full(s2.shape, float(PEER_KEEP), F32)
            cnt = jnp.zeros(s1.shape, F32)
            for q in range(PEER_KEEP, 0, -1):
                rank2 = jnp.where(s2 == bb[q - 1], float(q - 1), rank2)
                n_q = sum(jnp.where(sums[(q, j)] >= thr, 1.0, 0.0) for j in range(1, PEER_KEEP // q + 1))
                cnt = jnp.where(s1 == a[q - 1], n_q, cnt)
            cnt_ref[hd, :, ls] = cnt
            at_ref[hd, :, ls] = jnp.exp(s1 - a[0]) / z
            rk_ref[hd, :, ls] = rank2.astype(BF16)
            bm_ref[hd, :, ls] = jnp.exp(s2 - bb[0]).astype(BF16)


def _route(att, hgo, x2, w_out, g1, b1, w_q, sub_keys, *, tb, alpha, interpret=False):
    T, D = x2.shape
    woa = w_out[:ATT_Q].astype(BF16)
    wob = w_out[ATT_Q:].astype(BF16)

    def full(a):
        zeros = (0,) * a.ndim
        return pl.BlockSpec(a.shape, lambda t: zeros)

    ins = [att, hgo, x2, woa, wob, g1.reshape(1, D).astype(F32), b1.reshape(1, D).astype(F32),
           w_q.astype(BF16), sub_keys.astype(BF16)]
    in_specs = [pl.BlockSpec((tb, ATT_Q), lambda t: (t, 0)), pl.BlockSpec((tb, HG_W), lambda t: (t, 0)),
                pl.BlockSpec((tb, D), lambda t: (t, 0))] + [full(a) for a in ins[3:]]
    rt_shape = lambda dt: jax.ShapeDtypeStruct((PEER_HEADS, PEER_N_KEYS, T), dt)
    rt_spec = pl.BlockSpec((PEER_HEADS, PEER_N_KEYS, tb), lambda t: (0, 0, t))
    return pl.pallas_call(
        functools.partial(_route_kernel, alpha=alpha),
        out_shape=[jax.ShapeDtypeStruct((T, D), F32), jax.ShapeDtypeStruct((T, D), BF16),
                   rt_shape(F32), rt_shape(F32), rt_shape(BF16), rt_shape(BF16)],
        grid=(T // tb,),
        in_specs=in_specs,
        out_specs=[pl.BlockSpec((tb, D), lambda t: (t, 0)), pl.BlockSpec((tb, D), lambda t: (t, 0)),
                   rt_spec, rt_spec, rt_spec, rt_spec],
        compiler_params=pltpu.CompilerParams(dimension_semantics=("parallel",),
                                             vmem_limit_bytes=VMEM_LIMIT_BYTES),
        name="route",
        interpret=interpret,
    )(*ins)


GELU_K = 2.0 * (2.0 / 3.141592653589793) ** 0.5


def _gelu_tanh(x):
    z = x * ((-GELU_K * LOG2E) + (-GELU_K * 0.044715 * LOG2E) * (x * x))
    return x / (1.0 + jnp.exp2(z))


def _peer_kernel(hb_ref, h_ref, u_ref, vt_ref, cnt_ref, at_ref, rk_ref, bm_ref, g2_ref, b2_ref,
                 out_ref, acc_ref, act_ref, p_ref, *, ic, tw, alpha):
    c = pl.program_id(1)

    @pl.when(c == 0)
    def _():
        acc_ref[...] = jnp.zeros(acc_ref.shape, F32)

    nk = PEER_N_KEYS
    tb = hb_ref.shape[0]
    grp = nk // BF16_ROWS
    act_ref[...] = _gelu_tanh(_dot_nt(u_ref[...], hb_ref[...]).astype(BF16))
    for ii in range(ic):
        rows = slice(ii * nk, (ii + 1) * nk)
        for s in range(tb // tw):
            ls = slice(s * tw, (s + 1) * tw)
            gsum = jnp.zeros((grp, BF16_ROWS, tw), BF16)
            for hd in range(PEER_HEADS):
                cnt = jnp.broadcast_to(cnt_ref[hd, ii:ii + 1, ls], (BF16_ROWS, tw)).astype(BF16)
                wgt = jnp.broadcast_to(at_ref[hd, ii:ii + 1, ls], (BF16_ROWS, tw)).astype(BF16)
                rk = rk_ref[hd, :, ls].reshape(grp, BF16_ROWS, tw)
                bm = bm_ref[hd, :, ls].reshape(grp, BF16_ROWS, tw)
                gsum = gsum + jnp.where(rk < cnt[None], bm, jnp.zeros_like(bm)) * wgt[None]
            p_ref[rows, ls] = gsum.reshape(nk, tw) * act_ref[rows, ls]
    acc_ref[...] += _dot(vt_ref[...], p_ref[...])

    @pl.when(c == pl.num_programs(1) - 1)
    def _():
        y = alpha * h_ref[...] + acc_ref[...].T
        out_ref[...] = _layer_norm(y, g2_ref[...], b2_ref[...])


def _peer(hb, h, u, v, cnt, at, rk, bm, g2, b2, *, tb, ic, tw, alpha, interpret=False):
    T, D = h.shape
    n_exp = u.shape[0]
    ub = u.astype(BF16)
    vtb = v.astype(BF16).T
    ec = ic * PEER_N_KEYS
    rows = pl.BlockSpec((PEER_HEADS, ic, tb), lambda t, c: (0, c, t))
    whole = pl.BlockSpec((PEER_HEADS, PEER_N_KEYS, tb), lambda t, c: (0, 0, t))
    vec = pl.BlockSpec((1, D), lambda t, c: (0, 0))
    return pl.pallas_call(
        functools.partial(_peer_kernel, ic=ic, tw=tw, alpha=alpha),
        out_shape=jax.ShapeDtypeStruct((T, D), F32),
        grid=(T // tb, n_exp // ec),
        in_specs=[pl.BlockSpec((tb, D), lambda t, c: (t, 0)), pl.BlockSpec((tb, D), lambda t, c: (t, 0)),
                  pl.BlockSpec((ec, D), lambda t, c: (c, 0)), pl.BlockSpec((D, ec), lambda t, c: (0, c)),
                  rows, rows, whole, whole, vec, vec],
        out_specs=pl.BlockSpec((tb, D), lambda t, c: (t, 0)),
        scratch_shapes=[pltpu.VMEM((D, tb), F32), pltpu.VMEM((ec, tb), BF16), pltpu.VMEM((ec, tb), BF16)],
        compiler_params=pltpu.CompilerParams(dimension_semantics=("parallel", "arbitrary"),
                                             vmem_limit_bytes=VMEM_LIMIT_BYTES),
        name="peer",
        interpret=interpret,
    )(hb, h, ub, vtb, cnt, at, rk, bm, g2.reshape(1, D).astype(F32), b2.reshape(1, D).astype(F32))


def _tile(n, pref):
    t = min(pref, n)
    assert n % t == 0, (n, t)
    return t


PROJ_TOKENS = 512
DSA_QUERIES = 256
HGRN_TOKENS = 512
ROUTE_TOKENS = 512
PEER_TOKENS = 512
PEER_I_PER_STEP = 16
PEER_GATE_LANES = 256


def kernel(x, w_in, kv_norm_g, w_uk, w_uv, hg_lb_logits, hg_norm_g, w_out, ln1_g, ln1_b,
           peer_w_q, peer_sub_keys, peer_u, peer_v, ln2_g, ln2_b):
    B, L, D = x.shape
    depth = w_in.shape[0]
    alpha = (2.0 * depth) ** 0.25
    T = B * L
    for layer in range(depth):
        (c, ct, kidx, qidxt, widxt, qlatt, hq, hf, hi, hg) = _proj(
            x, w_in[layer], kv_norm_g[layer], w_uk[layer], tb=_tile(L, PROJ_TOKENS))
        tq = _tile(L, DSA_QUERIES)
        att = _dsa(kidx, c, ct, qidxt, widxt, qlatt, w_uv[layer], tq=tq, ks=tq, kc=tq)
        hgo = _hgrn(hq, hf, hi, hg, hg_lb_logits, hg_norm_g[layer], ts=_tile(L, HGRN_TOKENS), layer=layer)
        h, hb, rcnt, rat, rrk, rbm = _route(
            att.reshape(T, ATT_Q), hgo.reshape(T, HG_W), x.reshape(T, D), w_out[layer], ln1_g[layer], ln1_b[layer],
            peer_w_q[layer], peer_sub_keys[layer], tb=_tile(T, ROUTE_TOKENS), alpha=alpha)
        y = _peer(hb, h, peer_u[layer], peer_v[layer], rcnt, rat, rrk, rbm, ln2_g[layer], ln2_b[layer],
                  tb=_tile(T, PEER_TOKENS), ic=PEER_I_PER_STEP, tw=PEER_GATE_LANES, alpha=alpha)
        x = y.reshape(B, L, D)
    return x
```

```python
import functools

import jax
import jax.numpy as jnp
from jax import lax
from jax.experimental import pallas as pl
from jax.experimental.pallas import tpu as pltpu

F32 = jnp.float32
BF16 = jnp.bfloat16
I32 = jnp.int32
I16 = jnp.int16

ATT_HEADS = 8
ATT_HEAD_DIM = 64
KV_LATENT = 128
IDX_HEADS = 8
IDX_DIM = 64
TOPK_MAX = 256
HG_HEADS = 8
HG_DIM = 64
HG_CHUNK = 32
PEER_HEADS = 8
PEER_N_KEYS = 128
PEER_TOPK = 16
EPS = 1e-5

ATT_Q = ATT_HEADS * ATT_HEAD_DIM
IDX_Q = IDX_HEADS * IDX_DIM
HG_W = HG_HEADS * HG_DIM

LANES = 128
VMEM_LIMIT_BYTES = 62 * 1024 * 1024

LOG2E = 1.4426950408889634
NEG_BIG = -1e30
INT_MIN = -(2 ** 31)
HALF16 = 1 << 15
BF16_ROWS = 16
KEY_NEG_INF = INT_MIN + 0x7FFFFF


def _dot(a, b):
    return jnp.dot(a, b, preferred_element_type=F32)


def _dot_nt(a, b):
    return lax.dot_general(a, b, (((1,), (1,)), ((), ())), preferred_element_type=F32)


def _dot_exact(a, b):
    return jnp.dot(a, b, preferred_element_type=F32, precision=lax.Precision.HIGHEST)


def _proj_kernel(x_ref, wc_ref, wk_ref, whq_ref, whf_ref, whi_ref, whg_ref,
                 wtq_ref, wtqi_ref, wtwi_ref, wtc_ref, grow_ref, gcol_ref, wuk_ref,
                 c_ref, ct_ref, kidx_ref, qidxt_ref, widxt_ref, qlatt_ref,
                 hq_ref, hf_ref, hi_ref, hg_ref):
    xb = x_ref[0].astype(BF16)

    c = _dot(xb, wc_ref[...])
    c = c * lax.rsqrt(jnp.mean(c * c, axis=-1, keepdims=True) + EPS) * grow_ref[...]
    c_ref[0] = c.astype(BF16)
    ct = _dot_nt(wtc_ref[...], xb)
    ct = ct * lax.rsqrt(jnp.mean(ct * ct, axis=0, keepdims=True) + EPS) * gcol_ref[...]
    ct_ref[0] = ct.astype(BF16)

    kidx_ref[0] = _dot(xb, wk_ref[...]).astype(BF16)
    qidxt_ref[0] = _dot_nt(wtqi_ref[...], xb).astype(BF16)
    widxt_ref[0] = _dot_nt(wtwi_ref[...], xb)

    qt = _dot_nt(wtq_ref[...], xb).astype(BF16)
    scale = ATT_HEAD_DIM ** -0.5 * LOG2E
    for h in range(ATT_HEADS):
        ql = _dot(wuk_ref[h], qt[h * ATT_HEAD_DIM:(h + 1) * ATT_HEAD_DIM, :]) * scale
        qlatt_ref[0, h * KV_LATENT:(h + 1) * KV_LATENT, :] = ql.astype(BF16)

    hq_ref[0] = _dot(xb, whq_ref[...])
    hf_ref[0] = _dot(xb, whf_ref[...])
    hi_ref[0] = _dot(xb, whi_ref[...])
    hg_ref[0] = _dot(xb, whg_ref[...])


def _proj(x, w, kv_g, w_uk, *, tb, interpret=False):
    B, L, D = x.shape
    o = 0
    cols = {}
    for name, width in (("q", ATT_Q), ("c", KV_LATENT), ("qi", IDX_Q), ("k", IDX_DIM), ("wi", IDX_HEADS),
                        ("hq", HG_W), ("hf", HG_W), ("hi", HG_W), ("hg", HG_W)):
        cols[name] = w[:, o:o + width].astype(BF16)
        o += width
    assert o == w.shape[1]
    wt = lambda n: cols[n].T
    ins = [x, cols["c"], cols["k"], cols["hq"], cols["hf"], cols["hi"], cols["hg"],
           wt("q"), wt("qi"), wt("wi"), wt("c"),
           kv_g.reshape(1, KV_LATENT).astype(F32), kv_g.reshape(KV_LATENT, 1).astype(F32),
           w_uk.astype(BF16)]

    def full(a):
        zeros = (0,) * a.ndim
        return pl.BlockSpec(a.shape, lambda b, t: zeros)

    in_specs = [pl.BlockSpec((1, tb, D), lambda b, t: (b, t, 0))] + [full(a) for a in ins[1:]]
    nat = lambda wd, dt: (jax.ShapeDtypeStruct((B, L, wd), dt), pl.BlockSpec((1, tb, wd), lambda b, t: (b, t, 0)))
    tr = lambda wd, dt: (jax.ShapeDtypeStruct((B, wd, L), dt), pl.BlockSpec((1, wd, tb), lambda b, t: (b, 0, t)))
    outs = [nat(KV_LATENT, BF16), tr(KV_LATENT, BF16), nat(IDX_DIM, BF16), tr(IDX_Q, BF16), tr(IDX_HEADS, F32),
            tr(ATT_HEADS * KV_LATENT, BF16), nat(HG_W, F32), nat(HG_W, F32), nat(HG_W, F32), nat(HG_W, F32)]
    return pl.pallas_call(
        _proj_kernel,
        out_shape=[s for s, _ in outs],
        grid=(B, L // tb),
        in_specs=in_specs,
        out_specs=[s for _, s in outs],
        compiler_params=pltpu.CompilerParams(dimension_semantics=("parallel", "parallel"),
                                             vmem_limit_bytes=VMEM_LIMIT_BYTES),
        name="proj",
        interpret=interpret,
    )(*ins)


def _dsa_kernel(kidx_ref, c_ref, ct_ref, qidxt_ref, widxt_ref, qlatt_ref, wuvt_ref,
                out_ref, keys_ref, khi_ref, klo_ref, acc_ref, m_ref, l_ref, alpha_ref, jb_ref, ot_ref, a_ref, p_ref,
                *, tq, ks, kc, top_k, seq):
    qi = pl.program_id(1)
    n_keys = (qi + 1) * tq
    nks = (qi + 1) * (tq // ks)
    nkc = (qi + 1) * (tq // kc)
    q_pos = lambda rows: qi * tq + lax.broadcasted_iota(I32, (rows, tq), 1)
    k_off = lambda rows: lax.broadcasted_iota(I32, (rows, tq), 0)

    def score_chunk(k, carry):
        r0 = pl.multiple_of(k * ks, ks)
        kk = kidx_ref[0, pl.ds(r0, ks), :]
        s = jnp.zeros((ks, tq), F32)
        for h in range(IDX_HEADS):
            z = _dot(kk, qidxt_ref[0, h * IDX_DIM:(h + 1) * IDX_DIM, :])
            s = s + widxt_ref[0, h:h + 1, :] * jnp.maximum(z, 0.0)
        s = jnp.where(r0 + k_off(ks) <= q_pos(ks), s, -jnp.inf)
        bits = pltpu.bitcast(s, I32)
        key = bits ^ ((bits >> 31) & 0x7FFFFFFF)
        keys_ref[pl.ds(r0, ks), :] = key
        khi_ref[pl.ds(r0, ks), :] = (key >> 16).astype(I16)
        klo_ref[pl.ds(r0, ks), :] = ((key & 0xFFFF) - HALF16).astype(I16)
        return carry

    lax.fori_loop(0, nks, score_chunk, 0)

    kcc = 2 * kc
    nkcc = (nkc + 1) >> 1

    @pl.when(n_keys < seq)
    def _():
        pad_rows = pl.ds(pl.multiple_of(n_keys, kc), kc)
        khi_ref[pad_rows, :] = jnp.full((kc, tq), -HALF16, I16)
        klo_ref[pad_rows, :] = jnp.full((kc, tq), -HALF16, I16)

    def count(pred):
        def body(k, cnt):
            r0 = pl.multiple_of(k * kc, kc)
            blk = keys_ref[pl.ds(r0, kc), :]
            return cnt + jnp.sum(jnp.where(pred(blk, r0), 1, 0).astype(I32), axis=0, keepdims=True)
        return lax.fori_loop(0, nkc, body, jnp.zeros((1, tq), I32))

    def count16(ref, cand):
        c16 = cand.astype(I16)

        def body(k, cnt):
            r0 = pl.multiple_of(k * kcc, kcc)
            blk = ref[pl.ds(r0, kcc), :]
            ones = jnp.where(blk >= c16, jnp.ones(blk.shape, BF16), jnp.zeros(blk.shape, BF16))
            parts = [ones[g * BF16_ROWS:(g + 1) * BF16_ROWS] for g in range(kcc // BF16_ROWS)]
            while len(parts) > 1:
                parts = [parts[i] + parts[i + 1] for i in range(0, len(parts), 2)]
            return cnt + parts[0].astype(F32)
        cnt = lax.fori_loop(0, nkcc, body, jnp.zeros((BF16_ROWS, tq), F32))
        return jnp.sum(cnt, axis=0, keepdims=True).astype(I32)

    def hi_step(i, carry):
        t, n_t = carry
        cand = t + lax.shift_left(jnp.int32(1), 15 - i)
        cnt = count16(khi_ref, cand)
        take = cnt >= top_k
        return jnp.where(take, cand, t), jnp.where(take, cnt, n_t)

    h, n_ge_h = lax.fori_loop(0, 16, hi_step,
                              (jnp.full((1, tq), -HALF16, I32), jnp.full((1, tq), n_keys, I32)))
    h16 = h.astype(I16)

    def mark_boundary(k, carry):
        rows = pl.ds(pl.multiple_of(k * kcc, kcc), kcc)
        hi = khi_ref[rows, :]
        klo_ref[rows, :] = jnp.where(hi > h16, jnp.int16(HALF16 - 1),
                                     jnp.where(hi == h16, klo_ref[rows, :], jnp.int16(-HALF16)))
        return carry

    lax.fori_loop(0, nkcc, mark_boundary, 0)

    def lo_bit(i, carry):
        t, n_t = carry
        cand = t + lax.shift_left(jnp.int32(1), 15 - i)
        cnt = count16(klo_ref, cand)
        take = cnt >= top_k
        return jnp.where(take, cand, t), jnp.where(take, cnt, n_t)

    def lo_cond(carry):
        r, _, _, open_lanes = carry
        return jnp.logical_and(r < 4, open_lanes > 0)

    def lo_round(carry):
        r, t, n_t, _ = carry
        t, n_t = lax.fori_loop(r * 4, r * 4 + 4, lo_bit, (t, n_t))
        return r + 1, t, n_t, jnp.max(jnp.abs(n_t - top_k))

    _, t_lo, n_ge, _ = lax.while_loop(
        lo_cond, lo_round,
        (jnp.int32(0), jnp.full((1, tq), -HALF16, I32), n_ge_h, jnp.max(jnp.abs(n_ge_h - top_k))))
    thr = lax.shift_left(h, 16) + (t_lo + HALF16)

    has_tie = jnp.where(n_ge > top_k, jnp.where(thr > KEY_NEG_INF, 1, 0), 0)
    any_tie = jnp.max(has_tie)
    jb_ref[...] = jnp.full((1, tq), -1, I32)

    @pl.when(any_tie > 0)
    def _():
        need = top_k - count(lambda blk, r0: blk > thr)

        def idx_step(i, v):
            cand = v + lax.shift_left(jnp.int32(1), (seq.bit_length() - 1) - i)
            cnt = count(lambda blk, r0: (blk == thr) & (r0 + k_off(kc) < cand))
            return jnp.where(cnt < need, cand, v)
        v = lax.fori_loop(0, seq.bit_length(), idx_step, jnp.zeros((1, tq), I32))
        jb_ref[...] = jnp.where(has_tie > 0, v, -1)

    jb = jb_ref[...]
    gt_thr = jnp.where(has_tie > 0, thr, jnp.maximum(thr - 1, KEY_NEG_INF))

    l_ref[...] = jnp.zeros(l_ref.shape, F32)
    acc_ref[...] = jnp.zeros(acc_ref.shape, F32)

    def masked_bias(r0):
        key = keys_ref[pl.ds(r0, kc), :]
        return lax.cond(
            any_tie > 0,
            lambda: jnp.where(key > gt_thr, 0.0,
                              jnp.where(key == thr, jnp.where(r0 + k_off(kc) <= jb, 0.0, NEG_BIG), NEG_BIG)),
            lambda: jnp.where(key > gt_thr, 0.0, NEG_BIG))

    def logits(h, cc, bias, m_prev, slot):
        a = _dot(cc, qlatt_ref[0, h * KV_LATENT:(h + 1) * KV_LATENT, :]) + bias
        a_ref[slot, h] = a
        m_new = jnp.maximum(m_prev, jnp.max(a, axis=0, keepdims=True))
        alpha_ref[slot, h] = jnp.exp2(m_prev - m_new)
        m_ref[slot, h] = m_new

    bias0 = masked_bias(0)
    cc0 = c_ref[0, pl.ds(0, kc), :]
    for h in range(ATT_HEADS):
        logits(h, cc0, bias0, jnp.full((1, tq), NEG_BIG, F32), 0)

    def att_step(k, cur):
        nxt = 1 - cur
        r0 = pl.multiple_of(k * kc, kc)
        rn = pl.multiple_of(jnp.minimum(k + 1, nkc - 1) * kc, kc)
        bias_n = masked_bias(rn)
        cc_n = c_ref[0, pl.ds(rn, kc), :]
        cct = ct_ref[0, :, pl.ds(r0, kc)]
        for h in range(ATT_HEADS):
            m_cur = m_ref[cur, h]
            logits(h, cc_n, bias_n, m_cur, nxt)
            p = jnp.exp2(a_ref[cur, h] - m_cur)
            l_ref[h] = alpha_ref[cur, h] * l_ref[h] + jnp.sum(p, axis=0, keepdims=True)
            p_ref[h] = p.astype(BF16)
            acc_ref[h] = alpha_ref[cur, h] * acc_ref[h] + _dot(cct, p_ref[h])

    def att_chunk(k, carry):
        for parity in range(2):
            pl.when(k % 2 == parity)(functools.partial(att_step, k, parity))
        return carry

    lax.fori_loop(0, nkc, att_chunk, 0)

    for h in range(ATT_HEADS):
        o_lat = (acc_ref[h] / l_ref[h]).astype(BF16)
        ot_ref[h * ATT_HEAD_DIM:(h + 1) * ATT_HEAD_DIM, :] = _dot(wuvt_ref[h], o_lat)
    out_ref[0] = ot_ref[...].T.astype(BF16)


def _dsa(kidx, c, ct, qidxt, widxt, qlatt, w_uv, *, tq, ks, kc, interpret=False):
    B, L, _ = c.shape
    top_k = min(TOPK_MAX, L // 4)
    assert tq % ks == 0 and tq % kc == 0 and top_k <= tq and L % (2 * kc) == 0 and 2 * kc // BF16_ROWS <= 256
    wuvt = jnp.swapaxes(w_uv, 1, 2).astype(BF16)
    kernel = functools.partial(_dsa_kernel, tq=tq, ks=ks, kc=kc, top_k=top_k, seq=L)
    return pl.pallas_call(
        kernel,
        out_shape=jax.ShapeDtypeStruct((B, L, ATT_Q), BF16),
        grid=(B, L // tq),
        in_specs=[
            pl.BlockSpec((1, L, IDX_DIM), lambda b, q: (b, 0, 0)),
            pl.BlockSpec((1, L, KV_LATENT), lambda b, q: (b, 0, 0)),
            pl.BlockSpec((1, KV_LATENT, L), lambda b, q: (b, 0, 0)),
            pl.BlockSpec((1, IDX_Q, tq), lambda b, q: (b, 0, q)),
            pl.BlockSpec((1, IDX_HEADS, tq), lambda b, q: (b, 0, q)),
            pl.BlockSpec((1, ATT_HEADS * KV_LATENT, tq), lambda b, q: (b, 0, q)),
            pl.BlockSpec((ATT_HEADS, ATT_HEAD_DIM, KV_LATENT), lambda b, q: (0, 0, 0)),
        ],
        out_specs=pl.BlockSpec((1, tq, ATT_Q), lambda b, q: (b, q, 0)),
        scratch_shapes=[
            pltpu.VMEM((L, tq), I32),
            pltpu.VMEM((L, tq), I16),
            pltpu.VMEM((L, tq), I16),
            pltpu.VMEM((ATT_HEADS, KV_LATENT, tq), F32),
            pltpu.VMEM((2, ATT_HEADS, 1, tq), F32),
            pltpu.VMEM((ATT_HEADS, 1, tq), F32),
            pltpu.VMEM((2, ATT_HEADS, 1, tq), F32),
            pltpu.VMEM((1, tq), I32),
            pltpu.VMEM((ATT_Q, tq), F32),
            pltpu.VMEM((2, ATT_HEADS, kc, tq), F32),
            pltpu.VMEM((ATT_HEADS, kc, tq), BF16),
        ],
        compiler_params=pltpu.CompilerParams(dimension_semantics=("parallel", "parallel"),
                                             vmem_limit_bytes=VMEM_LIMIT_BYTES),
        name="dsa",
        interpret=interpret,
    )(kidx, c, ct, qidxt, widxt, qlatt, wuvt)


HG_GROUP = 128
HG_PAIRS = HG_HEADS // 2


def _hgrn_kernel(lbl_ref, gn_ref, hq_ref, hf_ref, hi_ref, hg_ref, out_ref, st_ref, *, ts, layer):
    @pl.when(pl.program_id(1) == 0)
    def _():
        st_ref[...] = jnp.zeros(st_ref.shape, F32)

    logits = lbl_ref[...]
    e = jnp.exp(logits - jnp.max(logits, axis=0, keepdims=True))
    lb = jnp.sum(e[0:layer + 1, :], axis=0, keepdims=True) / jnp.sum(e, axis=0, keepdims=True)

    g = HG_GROUP
    chunk_shift = HG_CHUNK.bit_length() - 1
    head_shift = HG_DIM.bit_length() - 1
    r = lax.broadcasted_iota(I32, (g, g), 0)
    cidx = lax.broadcasted_iota(I32, (g, g), 1)
    same_chunk = (r >> chunk_shift) == (cidx >> chunk_shift)
    causal = same_chunk & (cidx <= r)
    tri = jnp.where(causal, 1.0, 0.0).astype(F32)
    ones_blk = jnp.where(same_chunk, 1.0, 0.0).astype(F32)
    same_head = (r >> head_shift) == (cidx >> head_shift)
    head_blk = jnp.where(same_head, 1.0, 0.0).astype(F32)
    lane_lo = lax.broadcasted_iota(I32, (g, LANES), 1) < HG_DIM
    col_chunk = cidx >> chunk_shift
    gn = gn_ref[...]

    def group(gi, carry):
        r0 = pl.multiple_of(gi * g, g)
        rows = pl.ds(r0, g)
        f = lb + (1.0 - lb) * jax.nn.sigmoid(hf_ref[0, rows, :])
        logf = jnp.log(f)
        kgate = 1.0 - f
        b = _dot_exact(tri, logf)
        btot = _dot_exact(ones_blk, logf)
        q_t = jax.nn.silu(hq_ref[0, rows, :]) * jnp.exp(b)
        k_t = kgate * jnp.exp(-b)
        k_dec = kgate * jnp.exp(btot - b)
        decay = jnp.exp(btot)
        v = hi_ref[0, rows, :]
        gate = hg_ref[0, rows, :]
        for p in range(HG_PAIRS):
            ls = slice(p * LANES, (p + 1) * LANES)
            qp = q_t[:, ls]
            kp = k_t[:, ls].astype(BF16)
            vp = v[:, ls].astype(BF16)
            kdp = k_dec[:, ls].astype(BF16)
            q_lo = jnp.where(lane_lo, qp, 0.0).astype(BF16)
            q_hi = jnp.where(lane_lo, 0.0, qp).astype(BF16)
            a_lo = jnp.where(causal, _dot_nt(q_lo, kp), 0.0).astype(BF16)
            a_hi = jnp.where(causal, _dot_nt(q_hi, kp), 0.0).astype(BF16)
            o = jnp.where(lane_lo, _dot(a_lo, vp), _dot(a_hi, vp))
            st = st_ref[p]
            qpb = qp.astype(BF16)
            vt = v[:, ls].T
            inter = []
            for ch in range(g // HG_CHUNK):
                rs = slice(ch * HG_CHUNK, (ch + 1) * HG_CHUNK)
                inter.append(_dot_nt(qpb[rs], st.astype(BF16)))
                vt_ch = jnp.where(col_chunk == ch, vt, 0.0).astype(BF16)
                ds = _dot(vt_ch, kdp)
                st = st * decay[ch * HG_CHUNK:ch * HG_CHUNK + 1, ls] + jnp.where(same_head, ds, 0.0)
            st_ref[p] = st
            o = o + jnp.concatenate(inter, axis=0)
            ms = _dot_exact(o * o, head_blk) * (1.0 / HG_DIM)
            gt = gate[:, ls]
            on = o * lax.rsqrt(ms + EPS) * gn * (gt * jax.nn.sigmoid(gt))
            out_ref[0, rows, ls] = on.astype(BF16)
        return carry

    for gi in range(ts // g):
        group(gi, 0)


def _hgrn(hq, hf, hi, hg, lb_logits, g_norm, *, ts, layer, interpret=False):
    B, L, W = hq.shape
    gn = jnp.tile(g_norm.reshape(1, HG_DIM).astype(F32), (1, 2))
    blk = pl.BlockSpec((1, ts, W), lambda b, t: (b, t, 0))
    return pl.pallas_call(
        functools.partial(_hgrn_kernel, ts=ts, layer=layer),
        out_shape=jax.ShapeDtypeStruct((B, L, W), BF16),
        grid=(B, L // ts),
        in_specs=[pl.BlockSpec(lb_logits.shape, lambda b, t: (0, 0)),
                  pl.BlockSpec((1, LANES), lambda b, t: (0, 0)), blk, blk, blk, blk],
        out_specs=blk,
        scratch_shapes=[pltpu.VMEM((HG_PAIRS, LANES, LANES), F32)],
        compiler_params=pltpu.CompilerParams(dimension_semantics=("parallel", "arbitrary"),
                                             vmem_limit_bytes=VMEM_LIMIT_BYTES),
        name="hgrn2",
        interpret=interpret,
    )(lb_logits.astype(F32), gn, hq, hf, hi, hg)


def _layer_norm(y, g, b):
    mu = jnp.mean(y, axis=-1, keepdims=True)
    d = y - mu
    var = jnp.mean(d * d, axis=-1, keepdims=True)
    return d * lax.rsqrt(var + EPS) * g + b


def _sort_network(n):
    pairs = []

    def merge(lo, hi, r):
        step = r * 2
        if step < hi - lo:
            merge(lo, hi, step)
            merge(lo + r, hi, step)
            pairs.extend((i, i + r) for i in range(lo + r, hi - r, step))
        else:
            pairs.append((lo, lo + r))

    def sort(lo, hi):
        if hi - lo >= 1:
            mid = lo + (hi - lo) // 2
            sort(lo, mid)
            sort(mid + 1, hi)
            merge(lo, hi, 1)

    sort(0, n - 1)
    return pairs


SUBLANES = 8


def _top_sorted(s, n):
    regs = [s[r:r + SUBLANES] for r in range(0, s.shape[0], SUBLANES)]
    for i, j in _sort_network(len(regs)):
        regs[i], regs[j] = jnp.maximum(regs[i], regs[j]), jnp.minimum(regs[i], regs[j])
    vals = []
    for it in range(n):
        mx = jnp.max(regs[0], axis=0, keepdims=True)
        vals.append(mx)
        if it + 1 < n:
            took = regs[0] == mx
            depth = min(len(regs), n - it)
            for d in range(depth - 1):
                regs[d] = jnp.where(took, regs[d + 1], regs[d])
            regs[depth - 1] = jnp.where(took, regs[depth] if depth < len(regs) else -jnp.inf, regs[depth - 1])
    return vals


PEER_KEEP = PEER_TOPK + 1
_PAIR_RANKS = [(i, j) for i in range(1, PEER_KEEP + 1) for j in range(1, PEER_KEEP + 1) if i * j <= PEER_KEEP]
PEER_CAND_ROWS = SUBLANES * (1 << (-(-len(_PAIR_RANKS) // SUBLANES) - 1).bit_length())


def _route_kernel(att_ref, hg_ref, x_ref, woa_ref, wob_ref, g1_ref, b1_ref, wq_ref, sk_ref,
                  h_ref, hb_ref, cnt_ref, at_ref, rk_ref, bm_ref, *, alpha):
    mix = _dot(att_ref[...], woa_ref[...]) + _dot(hg_ref[...], wob_ref[...])
    h = _layer_norm(alpha * x_ref[...] + mix, g1_ref[...], b1_ref[...])
    h_ref[...] = h
    hb = h.astype(BF16)
    hb_ref[...] = hb
    qp = _dot(hb, wq_ref[...]).astype(BF16)
    dk = PEER_N_KEYS
    tb = h.shape[0]
    for hd in range(PEER_HEADS):
        s1_all = _dot_nt(sk_ref[hd, 0], qp[:, (2 * hd) * dk:(2 * hd + 1) * dk])
        s2_all = _dot_nt(sk_ref[hd, 1], qp[:, (2 * hd + 1) * dk:(2 * hd + 2) * dk])
        for sub in range(tb // LANES):
            ls = slice(sub * LANES, (sub + 1) * LANES)
            s1, s2 = s1_all[:, ls], s2_all[:, ls]
            a = _top_sorted(s1, PEER_KEEP)
            bb = _top_sorted(s2, PEER_KEEP)
            sums = {(i, j): a[i - 1] + bb[j - 1] for i, j in _PAIR_RANKS}
            cands = list(sums.values())
            pad = PEER_CAND_ROWS - len(cands)
            cand = jnp.concatenate(cands + [jnp.full_like(cands[0], -jnp.inf)] * pad, axis=0)
            top = _top_sorted(cand, PEER_KEEP)
            c_max, c_k, c_next = top[0], top[PEER_TOPK - 1], top[PEER_TOPK]
            thr = 0.5 * (c_k + c_next)
            z = jnp.sum(jnp.where(cand >= c_k, jnp.exp(cand - c_max), 0.0), axis=0, keepdims=True)
            rank2 = jnp.full(s2.shape, float(PEER_KEEP), F32)
            cnt = jnp.zeros(s1.shape, F32)
            for q in range(PEER_KEEP, 0, -1):
                rank2 = jnp.where(s2 == bb[q - 1], float(q - 1), rank2)
                n_q = sum(jnp.where(sums[(q, j)] >= thr, 1.0, 0.0) for j in range(1, PEER_KEEP // q + 1))
                cnt = jnp.where(s1 == a[q - 1], n_q, cnt)
            cnt_ref[hd, :, ls] = cnt
            at_ref[hd, :, ls] = jnp.exp(s1 - a[0]) / z
            rk_ref[hd, :, ls] = rank2.astype(BF16)
            bm_ref[hd, :, ls] = jnp.exp(s2 - bb[0]).astype(BF16)


def _route(att, hgo, x2, w_out, g1, b1, w_q, sub_keys, *, tb, alpha, interpret=False):
    T, D = x2.shape
    woa = w_out[:ATT_Q].astype(BF16)
    wob = w_out[ATT_Q:].astype(BF16)

    def full(a):
        zeros = (0,) * a.ndim
        return pl.BlockSpec(a.shape, lambda t: zeros)

    ins = [att, hgo, x2, woa, wob, g1.reshape(1, D).astype(F32), b1.reshape(1, D).astype(F32),
           w_q.astype(BF16), sub_keys.astype(BF16)]
    in_specs = [pl.BlockSpec((tb, ATT_Q), lambda t: (t, 0)), pl.BlockSpec((tb, HG_W), lambda t: (t, 0)),
                pl.BlockSpec((tb, D), lambda t: (t, 0))] + [full(a) for a in ins[3:]]
    rt_shape = lambda dt: jax.ShapeDtypeStruct((PEER_HEADS, PEER_N_KEYS, T), dt)
    rt_spec = pl.BlockSpec((PEER_HEADS, PEER_N_KEYS, tb), lambda t: (0, 0, t))
    return pl.pallas_call(
        functools.partial(_route_kernel, alpha=alpha),
        out_shape=[jax.ShapeDtypeStruct((T, D), F32), jax.ShapeDtypeStruct((T, D), BF16),
                   rt_shape(F32), rt_shape(F32), rt_shape(BF16), rt_shape(BF16)],
        grid=(T // tb,),
        in_specs=in_specs,
        out_specs=[pl.BlockSpec((tb, D), lambda t: (t, 0)), pl.BlockSpec((tb, D), lambda t: (t, 0)),
                   rt_spec, rt_spec, rt_spec, rt_spec],
        compiler_params=pltpu.CompilerParams(dimension_semantics=("parallel",),
                                             vmem_limit_bytes=VMEM_LIMIT_BYTES),
        name="route",
        interpret=interpret,
    )(*ins)


GELU_K = 2.0 * (2.0 / 3.141592653589793) ** 0.5


def _gelu_tanh(x):
    z = x * ((-GELU_K * LOG2E) + (-GELU_K * 0.044715 * LOG2E) * (x * x))
    return x / (1.0 + jnp.exp2(z))


def _peer_kernel(hb_ref, h_ref, u_ref, vt_ref, cnt_ref, at_ref, rk_ref, bm_ref, g2_ref, b2_ref,
                 out_ref, acc_ref, act_ref, p_ref, *, ic, tw, alpha):
    c = pl.program_id(1)

    @pl.when(c == 0)
    def _():
        acc_ref[...] = jnp.zeros(acc_ref.shape, F32)

    nk = PEER_N_KEYS
    tb = hb_ref.shape[0]
    grp = nk // BF16_ROWS
    act_ref[...] = _gelu_tanh(_dot_nt(u_ref[...], hb_ref[...]).astype(BF16))
    for ii in range(ic):
        rows = slice(ii * nk, (ii + 1) * nk)
        for s in range(tb // tw):
            ls = slice(s * tw, (s + 1) * tw)
            gsum = jnp.zeros((grp, BF16_ROWS, tw), BF16)
            for hd in range(PEER_HEADS):
                cnt = jnp.broadcast_to(cnt_ref[hd, ii:ii + 1, ls], (BF16_ROWS, tw)).astype(BF16)
                wgt = jnp.broadcast_to(at_ref[hd, ii:ii + 1, ls], (BF16_ROWS, tw)).astype(BF16)
                rk = rk_ref[hd, :, ls].reshape(grp, BF16_ROWS, tw)
                bm = bm_ref[hd, :, ls].reshape(grp, BF16_ROWS, tw)
                gsum = gsum + jnp.where(rk < cnt[None], bm, jnp.zeros_like(bm)) * wgt[None]
            p_ref[rows, ls] = gsum.reshape(nk, tw) * act_ref[rows, ls]
    acc_ref[...] += _dot(vt_ref[...], p_ref[...])

    @pl.when(c == pl.num_programs(1) - 1)
    def _():
        y = alpha * h_ref[...] + acc_ref[...].T
        out_ref[...] = _layer_norm(y, g2_ref[...], b2_ref[...])


def _peer(hb, h, u, v, cnt, at, rk, bm, g2, b2, *, tb, ic, tw, alpha, interpret=False):
    T, D = h.shape
    n_exp = u.shape[0]
    ub = u.astype(BF16)
    vtb = v.astype(BF16).T
    ec = ic * PEER_N_KEYS
    rows = pl.BlockSpec((PEER_HEADS, ic, tb), lambda t, c: (0, c, t))
    whole = pl.BlockSpec((PEER_HEADS, PEER_N_KEYS, tb), lambda t, c: (0, 0, t))
    vec = pl.BlockSpec((1, D), lambda t, c: (0, 0))
    return pl.pallas_call(
        functools.partial(_peer_kernel, ic=ic, tw=tw, alpha=alpha),
        out_shape=jax.ShapeDtypeStruct((T, D), F32),
        grid=(T // tb, n_exp // ec),
        in_specs=[pl.BlockSpec((tb, D), lambda t, c: (t, 0)), pl.BlockSpec((tb, D), lambda t, c: (t, 0)),
                  pl.BlockSpec((ec, D), lambda t, c: (c, 0)), pl.BlockSpec((D, ec), lambda t, c: (0, c)),
                  rows, rows, whole, whole, vec, vec],
        out_specs=pl.BlockSpec((tb, D), lambda t, c: (t, 0)),
        scratch_shapes=[pltpu.VMEM((D, tb), F32), pltpu.VMEM((ec, tb), BF16), pltpu.VMEM((ec, tb), BF16)],
        compiler_params=pltpu.CompilerParams(dimension_semantics=("parallel", "arbitrary"),
                                             vmem_limit_bytes=VMEM_LIMIT_BYTES),
        name="peer",
        interpret=interpret,
    )(hb, h, ub, vtb, cnt, at, rk, bm, g2.reshape(1, D).astype(F32), b2.reshape(1, D).astype(F32))


def _tile(n, pref):
    t = min(pref, n)
    assert n % t == 0, (n, t)
    return t


PROJ_TOKENS = 512
DSA_QUERIES = 256
HGRN_TOKENS = 512
ROUTE_TOKENS = 512
PEER_TOKENS = 512
PEER_I_PER_STEP = 32
PEER_GATE_LANES = 256


def kernel(x, w_in, kv_norm_g, w_uk, w_uv, hg_lb_logits, hg_norm_g, w_out, ln1_g, ln1_b,
           peer_w_q, peer_sub_keys, peer_u, peer_v, ln2_g, ln2_b):
    B, L, D = x.shape
    depth = w_in.shape[0]
    alpha = (2.0 * depth) ** 0.25
    T = B * L
    for layer in range(depth):
        (c, ct, kidx, qidxt, widxt, qlatt, hq, hf, hi, hg) = _proj(
            x, w_in[layer], kv_norm_g[layer], w_uk[layer], tb=_tile(L, PROJ_TOKENS))
        tq = _tile(L, DSA_QUERIES)
        att = _dsa(kidx, c, ct, qidxt, widxt, qlatt, w_uv[layer], tq=tq, ks=tq, kc=tq)
        hgo = _hgrn(hq, hf, hi, hg, hg_lb_logits, hg_norm_g[layer], ts=_tile(L, HGRN_TOKENS), layer=layer)
        h, hb, rcnt, rat, rrk, rbm = _route(
            att.reshape(T, ATT_Q), hgo.reshape(T, HG_W), x.reshape(T, D), w_out[layer], ln1_g[layer], ln1_b[layer],
            peer_w_q[layer], peer_sub_keys[layer], tb=_tile(T, ROUTE_TOKENS), alpha=alpha)
        y = _peer(hb, h, peer_u[layer], peer_v[layer], rcnt, rat, rrk, rbm, ln2_g[layer], ln2_b[layer],
                  tb=_tile(T, PEER_TOKENS), ic=PEER_I_PER_STEP, tw=PEER_GATE_LANES, alpha=alpha)
        x = y.reshape(B, L, D)
    return x
```
